```python
import math
import jax, jax.numpy as jnp
from jax import lax
import numpy as np

D_MODEL = 1024
BATCH = 32
SEQ = 256
DEPTH = 2
DEC_BATCH = 8
DEC_SEQ = 2048
PAST_LEN = 512

GRID_W = 64
D_RNN = 512
N_RNN_HEADS = 8
RNN_HEAD_DIM = D_RNN // N_RNN_HEADS
CONV_W = 4
LRU_C = 8.0
D_POOL = 512
POOL_WINDOWS = (2, 4, 8, 16)
N_POOL_GROUPS = len(POOL_WINDOWS)
POOL_GROUP_DIM = D_POOL // N_POOL_GROUPS
D_IN = 2 * D_RNN + D_POOL
PEER_HEADS = 8
PEER_NKEYS = 128
PEER_EXPERTS = PEER_NKEYS * PEER_NKEYS
PEER_TOPK = 16
PEER_QDIM = 256
PEER_HALF = PEER_QDIM // 2
PEER_BLOCK = 128
N_MOD = 6
DEEPNORM_ALPHA = (2 * DEPTH) ** 0.25
DEEPNORM_BETA = (8 * DEPTH) ** -0.25
LN_EPS = 1e-5

kernel_name = 'hybrid_rglru_pool_peer_diffusion_step'


def layer_norm(x, g, b):
    xf = x.astype(jnp.float32)
    mu = jnp.mean(xf, -1, keepdims=True)
    var = jnp.mean(jnp.square(xf - mu), -1, keepdims=True)
    return ((xf - mu) * lax.rsqrt(var + LN_EPS) * g.astype(jnp.float32) + b.astype(jnp.float32)).astype(x.dtype)


def modulation(cvec, w_mod, b_mod):
    m = jax.nn.silu(cvec) @ w_mod + b_mod
    return jnp.split(m[:, None, :], N_MOD, axis=-1)


def grid_pos_embed(rows, dtype):
    t = jnp.arange(rows * GRID_W)
    r = (t // GRID_W).astype(jnp.float32)
    col = (t % GRID_W).astype(jnp.float32)
    n_freq = D_MODEL // 4
    omega = 1.0 / (10000.0 ** (jnp.arange(n_freq, dtype=jnp.float32) / n_freq))
    def enc(p):
        ang = p[:, None] * omega[None, :]
        return jnp.concatenate([jnp.sin(ang), jnp.cos(ang)], -1)
    return jnp.concatenate([enc(r), enc(col)], -1).astype(dtype)


def centred_depthwise_conv(x, w, b):
    T = x.shape[1]
    left = CONV_W // 2
    xp = jnp.pad(x, ((0, 0), (left, CONV_W - 1 - left), (0, 0)))
    y = b + xp[:, 0:T] * w[0]
    for k in range(1, CONV_W):
        y = y + xp[:, k:k + T] * w[k]
    return y


def rglru_scan(xc, w_a, b_a, w_x, b_x, lam, h0, reverse):
    B, T, _ = xc.shape
    xh = xc.reshape(B, T, N_RNN_HEADS, RNN_HEAD_DIM)
    r = jax.nn.sigmoid(jnp.einsum('bthi,hij->bthj', xh, w_a).reshape(B, T, D_RNN) + b_a)
    gi = jax.nn.sigmoid(jnp.einsum('bthi,hij->bthj', xh, w_x).reshape(B, T, D_RNN) + b_x)
    log_a = (-LRU_C * jax.nn.softplus(-lam.astype(jnp.float32))) * r.astype(jnp.float32)
    a = jnp.exp(log_a)
    u = jnp.sqrt(-jnp.expm1(2.0 * log_a)) * (gi * xc).astype(jnp.float32)
    if reverse:
        a = jnp.flip(a, 1)
        u = jnp.flip(u, 1)
    u = u.at[:, 0].add(a[:, 0] * h0.astype(jnp.float32))
    def combine(lhs, rhs):
        a1, b1 = lhs
        a2, b2 = rhs
        return a1 * a2, a2 * b1 + b2
    _, h = lax.associative_scan(combine, (a, u), axis=1)
    h_last = h[:, -1]
    if reverse:
        h = jnp.flip(h, 1)
    return h, h_last


def multiscale_pool(p, w_pool, pool_scale):
    B, T, _ = p.shape
    pf = p.astype(jnp.float32)
    cs = jnp.concatenate([jnp.zeros((B, 1, D_POOL), jnp.float32), jnp.cumsum(pf, axis=1)], axis=1)
    t = jnp.arange(T)
    outs = []
    for g, w in enumerate(POOL_WINDOWS):
        lo = jnp.clip(t - w // 2, 0, T)
        hi = jnp.clip(t + w - w // 2, 0, T)
        sl = slice(g * POOL_GROUP_DIM, (g + 1) * POOL_GROUP_DIM)
        csg = cs[..., sl]
        s = jnp.take(csg, hi, axis=1) - jnp.take(csg, lo, axis=1)
        cnt = (hi - lo).astype(jnp.float32)[None, :, None]
        pooled = (s / cnt - pf[..., sl]).astype(p.dtype)
        outs.append(pooled @ w_pool[g])
    return jnp.concatenate(outs, -1) * pool_scale


def token_mixer(h, h0, w_in, conv_w, conv_b, w_rg_a, b_rg_a, w_rg_x, b_rg_x, lru_lambda, w_pool, pool_scale, w_out):
    z = h @ w_in
    xr, gr, xq = jnp.split(z, [D_RNN, 2 * D_RNN], axis=-1)
    xc = centred_depthwise_conv(xr, conv_w, conv_b)
    hf, sf = rglru_scan(xc, w_rg_a[0], b_rg_a[0], w_rg_x[0], b_rg_x[0], lru_lambda[0], h0[:, 0], False)
    hb, sb = rglru_scan(xc, w_rg_a[1], b_rg_a[1], w_rg_x[1], b_rg_x[1], lru_lambda[1], h0[:, 1], True)
    rnn_out = (hf + hb).astype(h.dtype) * jax.nn.gelu(gr)
    pool_out = multiscale_pool(xq, w_pool, pool_scale)
    out = jnp.concatenate([rnn_out, pool_out], axis=-1) @ w_out
    return out, jnp.stack([sf, sb], axis=1).astype(h.dtype)


def peer_ffn(h, wq, keys, u_tab, v_tab):
    B, T, D = h.shape
    n_tok = B * T
    xt = h.reshape(n_tok, D)
    q = (xt @ wq).reshape(n_tok, PEER_HEADS, 2, PEER_HALF)
    s = jnp.einsum('nhpd,hpkd->nhpk', q, keys).astype(jnp.float32)
    s1, i1 = lax.top_k(s[:, :, 0], PEER_TOPK)
    s2, i2 = lax.top_k(s[:, :, 1], PEER_TOPK)
    cand = (s1[..., :, None] + s2[..., None, :]).reshape(n_tok, PEER_HEADS, PEER_TOPK * PEER_TOPK)
    sc, ci = lax.top_k(cand, PEER_TOPK)
    idx = (jnp.take_along_axis(i1, ci // PEER_TOPK, axis=-1) * PEER_NKEYS
           + jnp.take_along_axis(i2, ci % PEER_TOPK, axis=-1))
    gates = jax.nn.softmax(sc, axis=-1).astype(h.dtype)
    blk = math.gcd(n_tok, PEER_BLOCK)
    nb = n_tok // blk
    hk = PEER_HEADS * PEER_TOPK
    xs = xt.reshape(nb, blk, D)
    ids = idx.reshape(nb, blk, hk)
    gs = gates.reshape(nb, blk, hk)
    def block_fn(args):
        xb, ib, gb = args
        u = jnp.take(u_tab, ib, axis=0)
        act = jax.nn.gelu(jnp.einsum('tkd,td->tk', u, xb)) * gb
        return jnp.einsum('tk,tkd->td', act, jnp.take(v_tab, ib, axis=0))
    out = lax.map(block_fn, (xs, ids, gs))
    return out.reshape(B, T, D)


def trunk_layer(x, cvec, h0, w_mod, b_mod, w_in, conv_w, conv_b, w_rg_a, b_rg_a, w_rg_x, b_rg_x,
                lru_lambda, w_pool, pool_scale, w_out, ln1_g, ln1_b, ln2_g, ln2_b,
                peer_wq, peer_keys, peer_u, peer_v):
    sh1, sc1, g1, sh2, sc2, g2 = modulation(cvec, w_mod, b_mod)
    mix, st = token_mixer(x * (1.0 + sc1) + sh1, h0, w_in, conv_w, conv_b, w_rg_a, b_rg_a,
                          w_rg_x, b_rg_x, lru_lambda, w_pool, pool_scale, w_out)
    x = layer_norm(DEEPNORM_ALPHA * x + g1 * mix, ln1_g, ln1_b)
    ffn = peer_ffn(x * (1.0 + sc2) + sh2, peer_wq, peer_keys, peer_u, peer_v)
    x = layer_norm(DEEPNORM_ALPHA * x + g2 * ffn, ln2_g, ln2_b)
    return x, st


def setup_inputs(seed: int = 0) -> dict:
    key = jax.random.key(seed)
    ks = jax.random.split(key, 32)
    f32 = jnp.float32
    def nrm(k, shape, scale):
        return jax.random.normal(k, shape, f32) * scale
    u_l = jax.random.uniform(ks[13], (DEPTH, 2, D_RNN), f32, 0.9, 0.999)
    s_l = u_l ** (1.0 / LRU_C)
    lru_lambda = jnp.log(s_l) - jnp.log1p(-s_l)
    return {
        'x_prompt': nrm(ks[0], (BATCH, SEQ, D_MODEL), 1.0),
        'x_sample': nrm(ks[1], (DEC_BATCH, DEC_SEQ, D_MODEL), 1.0),
        'state_rglru': nrm(ks[2], (DEC_BATCH, DEPTH, 2, D_RNN), 0.5),
        'c': nrm(ks[3], (DEC_BATCH, D_MODEL), 1.0),
        'c_ctx': nrm(ks[4], (D_MODEL,), 1.0),
        'w_mod': nrm(ks[5], (DEPTH, D_MODEL, N_MOD * D_MODEL), D_MODEL ** -0.5),
        'b_mod': nrm(ks[6], (DEPTH, N_MOD * D_MODEL), 0.02),
        'w_in': nrm(ks[7], (DEPTH, D_MODEL, D_IN), D_MODEL ** -0.5),
        'conv_w': nrm(ks[8], (DEPTH, CONV_W, D_RNN), CONV_W ** -0.5),
        'conv_b': nrm(ks[9], (DEPTH, D_RNN), 0.02),
        'w_rg_a': nrm(ks[10], (DEPTH, 2, N_RNN_HEADS, RNN_HEAD_DIM, RNN_HEAD_DIM), RNN_HEAD_DIM ** -0.5),
        'b_rg_a': nrm(ks[11], (DEPTH, 2, D_RNN), 0.1),
        'w_rg_x': nrm(ks[12], (DEPTH, 2, N_RNN_HEADS, RNN_HEAD_DIM, RNN_HEAD_DIM), RNN_HEAD_DIM ** -0.5),
        'b_rg_x': nrm(ks[14], (DEPTH, 2, D_RNN), 0.1),
        'lru_lambda': lru_lambda,
        'w_pool': nrm(ks[15], (DEPTH, N_POOL_GROUPS, POOL_GROUP_DIM, POOL_GROUP_DIM), POOL_GROUP_DIM ** -0.5),
        'pool_scale': 1.0 + nrm(ks[16], (DEPTH, D_POOL), 0.02),
        'w_out': nrm(ks[17], (DEPTH, D_RNN + D_POOL, D_MODEL), (D_RNN + D_POOL) ** -0.5 * DEEPNORM_BETA),
        'ln1_g': 1.0 + nrm(ks[18], (DEPTH, D_MODEL), 0.02),
        'ln1_b': nrm(ks[19], (DEPTH, D_MODEL), 0.02),
        'ln2_g': 1.0 + nrm(ks[20], (DEPTH, D_MODEL), 0.02),
        'ln2_b': nrm(ks[21], (DEPTH, D_MODEL), 0.02),
        'peer_wq': nrm(ks[22], (DEPTH, D_MODEL, PEER_HEADS * PEER_QDIM), D_MODEL ** -0.5),
        'peer_keys': nrm(ks[23], (DEPTH, PEER_HEADS, 2, PEER_NKEYS, PEER_HALF), PEER_HALF ** -0.5),
        'peer_u': nrm(ks[24], (DEPTH, PEER_EXPERTS, D_MODEL), D_MODEL ** -0.5),
        'peer_v': nrm(ks[25], (DEPTH, PEER_EXPERTS, D_MODEL), DEEPNORM_BETA),
    }


def reference(x_prompt, x_sample, state_rglru, c, c_ctx, w_mod, b_mod, w_in, conv_w, conv_b,
              w_rg_a, b_rg_a, w_rg_x, b_rg_x, lru_lambda, w_pool, pool_scale, w_out,
              ln1_g, ln1_b, ln2_g, ln2_b, peer_wq, peer_keys, peer_u, peer_v):
    def run(x, cvec, h0, l):
        return trunk_layer(x, cvec, h0, w_mod[l], b_mod[l], w_in[l], conv_w[l], conv_b[l],
                           w_rg_a[l], b_rg_a[l], w_rg_x[l], b_rg_x[l], lru_lambda[l], w_pool[l],
                           pool_scale[l], w_out[l], ln1_g[l], ln1_b[l], ln2_g[l], ln2_b[l],
                           peer_wq[l], peer_keys[l], peer_u[l], peer_v[l])

    xp = x_prompt
    zero_state = jnp.zeros((x_prompt.shape[0], 2, D_RNN), x_prompt.dtype)
    ctx_states = []
    rows = x_sample.shape[1] // GRID_W
    xs = x_sample + grid_pos_embed(rows, x_sample.dtype)[None]
    for l in range(DEPTH):
        xp, st = run(xp, c_ctx[None, :], zero_state, l)
        ctx_states.append(st)
        xs, _ = run(xs, c, state_rglru[:, l], l)
    new_state_rglru = jnp.stack(ctx_states, axis=1)
    return (xp, xs, new_state_rglru)
```

```python
import math
import jax, jax.numpy as jnp
from jax import lax
from jax.experimental import pallas as pl

D_MODEL = 1024
DEPTH = 2
GRID_W = 64
D_RNN = 512
N_RNN_HEADS = 8
RNN_HEAD_DIM = D_RNN // N_RNN_HEADS
CONV_W = 4
LRU_C = 8.0
D_POOL = 512
POOL_WINDOWS = (2, 4, 8, 16)
N_POOL_GROUPS = len(POOL_WINDOWS)
POOL_GROUP_DIM = D_POOL // N_POOL_GROUPS
PEER_HEADS = 8
PEER_NKEYS = 128
PEER_TOPK = 16
PEER_QDIM = 256
PEER_HALF = PEER_QDIM // 2
PEER_BLOCK = 128
N_MOD = 6
DEEPNORM_ALPHA = (2 * DEPTH) ** 0.25
LN_EPS = 1e-5


def layer_norm(x, g, b):
    xf = x.astype(jnp.float32)
    mu = jnp.mean(xf, -1, keepdims=True)
    var = jnp.mean(jnp.square(xf - mu), -1, keepdims=True)
    return ((xf - mu) * lax.rsqrt(var + LN_EPS) * g.astype(jnp.float32) + b.astype(jnp.float32)).astype(x.dtype)


def modulation(cvec, w_mod, b_mod):
    m = jax.nn.silu(cvec) @ w_mod + b_mod
    return jnp.split(m[:, None, :], N_MOD, axis=-1)


def grid_pos_embed(rows, dtype):
    t = jnp.arange(rows * GRID_W)
    r = (t // GRID_W).astype(jnp.float32)
    col = (t % GRID_W).astype(jnp.float32)
    n_freq = D_MODEL // 4
    omega = 1.0 / (10000.0 ** (jnp.arange(n_freq, dtype=jnp.float32) / n_freq))
    def enc(p):
        ang = p[:, None] * omega[None, :]
        return jnp.concatenate([jnp.sin(ang), jnp.cos(ang)], -1)
    return jnp.concatenate([enc(r), enc(col)], -1).astype(dtype)


def centred_depthwise_conv(x, w, b):
    T = x.shape[1]
    left = CONV_W // 2
    xp = jnp.pad(x, ((0, 0), (left, CONV_W - 1 - left), (0, 0)))
    y = b + xp[:, 0:T] * w[0]
    for k in range(1, CONV_W):
        y = y + xp[:, k:k + T] * w[k]
    return y


def rglru_scan(xc, w_a, b_a, w_x, b_x, lam, h0, reverse):
    B, T, _ = xc.shape
    xh = xc.reshape(B, T, N_RNN_HEADS, RNN_HEAD_DIM)
    r = jax.nn.sigmoid(jnp.einsum('bthi,hij->bthj', xh, w_a).reshape(B, T, D_RNN) + b_a)
    gi = jax.nn.sigmoid(jnp.einsum('bthi,hij->bthj', xh, w_x).reshape(B, T, D_RNN) + b_x)
    log_a = (-LRU_C * jax.nn.softplus(-lam.astype(jnp.float32))) * r.astype(jnp.float32)
    a = jnp.exp(log_a)
    u = jnp.sqrt(-jnp.expm1(2.0 * log_a)) * (gi * xc).astype(jnp.float32)
    if reverse:
        a = jnp.flip(a, 1)
        u = jnp.flip(u, 1)
    u = u.at[:, 0].add(a[:, 0] * h0.astype(jnp.float32))
    def combine(lhs, rhs):
        a1, b1 = lhs
        a2, b2 = rhs
        return a1 * a2, a2 * b1 + b2
    _, h = lax.associative_scan(combine, (a, u), axis=1)
    h_last = h[:, -1]
    if reverse:
        h = jnp.flip(h, 1)
    return h, h_last


def multiscale_pool(p, w_pool, pool_scale):
    B, T, _ = p.shape
    pf = p.astype(jnp.float32)
    cs = jnp.concatenate([jnp.zeros((B, 1, D_POOL), jnp.float32), jnp.cumsum(pf, axis=1)], axis=1)
    t = jnp.arange(T)
    outs = []
    for g, w in enumerate(POOL_WINDOWS):
        lo = jnp.clip(t - w // 2, 0, T)
        hi = jnp.clip(t + w - w // 2, 0, T)
        sl = slice(g * POOL_GROUP_DIM, (g + 1) * POOL_GROUP_DIM)
        csg = cs[..., sl]
        s = jnp.take(csg, hi, axis=1) - jnp.take(csg, lo, axis=1)
        cnt = (hi - lo).astype(jnp.float32)[None, :, None]
        pooled = (s / cnt - pf[..., sl]).astype(p.dtype)
        outs.append(pooled @ w_pool[g])
    return jnp.concatenate(outs, -1) * pool_scale


def token_mixer(h, h0, w_in, conv_w, conv_b, w_rg_a, b_rg_a, w_rg_x, b_rg_x, lru_lambda, w_pool, pool_scale, w_out):
    z = h @ w_in
    xr, gr, xq = jnp.split(z, [D_RNN, 2 * D_RNN], axis=-1)
    xc = centred_depthwise_conv(xr, conv_w, conv_b)
    hf, sf = rglru_scan(xc, w_rg_a[0], b_rg_a[0], w_rg_x[0], b_rg_x[0], lru_lambda[0], h0[:, 0], False)
    hb, sb = rglru_scan(xc, w_rg_a[1], b_rg_a[1], w_rg_x[1], b_rg_x[1], lru_lambda[1], h0[:, 1], True)
    rnn_out = (hf + hb).astype(h.dtype) * jax.nn.gelu(gr)
    pool_out = multiscale_pool(xq, w_pool, pool_scale)
    out = jnp.concatenate([rnn_out, pool_out], axis=-1) @ w_out
    return out, jnp.stack([sf, sb], axis=1).astype(h.dtype)


def peer_ffn(h, wq, keys, u_tab, v_tab):
    B, T, D = h.shape
    n_tok = B * T
    xt = h.reshape(n_tok, D)
    q = (xt @ wq).reshape(n_tok, PEER_HEADS, 2, PEER_HALF)
    s = jnp.einsum('nhpd,hpkd->nhpk', q, keys).astype(jnp.float32)
    s1, i1 = lax.top_k(s[:, :, 0], PEER_TOPK)
    s2, i2 = lax.top_k(s[:, :, 1], PEER_TOPK)
    cand = (s1[..., :, None] + s2[..., None, :]).reshape(n_tok, PEER_HEADS, PEER_TOPK * PEER_TOPK)
    sc, ci = lax.top_k(cand, PEER_TOPK)
    idx = (jnp.take_along_axis(i1, ci // PEER_TOPK, axis=-1) * PEER_NKEYS
           + jnp.take_along_axis(i2, ci % PEER_TOPK, axis=-1))
    gates = jax.nn.softmax(sc, axis=-1).astype(h.dtype)
    blk = math.gcd(n_tok, PEER_BLOCK)
    nb = n_tok // blk
    hk = PEER_HEADS * PEER_TOPK
    xs = xt.reshape(nb, blk, D)
    ids = idx.reshape(nb, blk, hk)
    gs = gates.reshape(nb, blk, hk)
    def block_fn(args):
        xb, ib, gb = args
        u = jnp.take(u_tab, ib, axis=0)
        act = jax.nn.gelu(jnp.einsum('tkd,td->tk', u, xb)) * gb
        return jnp.einsum('tk,tkd->td', act, jnp.take(v_tab, ib, axis=0))
    out = lax.map(block_fn, (xs, ids, gs))
    return out.reshape(B, T, D)


def trunk_layer(x, cvec, h0, w_mod, b_mod, w_in, conv_w, conv_b, w_rg_a, b_rg_a, w_rg_x, b_rg_x,
                lru_lambda, w_pool, pool_scale, w_out, ln1_g, ln1_b, ln2_g, ln2_b,
                peer_wq, peer_keys, peer_u, peer_v):
    sh1, sc1, g1, sh2, sc2, g2 = modulation(cvec, w_mod, b_mod)
    mix, st = token_mixer(x * (1.0 + sc1) + sh1, h0, w_in, conv_w, conv_b, w_rg_a, b_rg_a,
                          w_rg_x, b_rg_x, lru_lambda, w_pool, pool_scale, w_out)
    x = layer_norm(DEEPNORM_ALPHA * x + g1 * mix, ln1_g, ln1_b)
    ffn = peer_ffn(x * (1.0 + sc2) + sh2, peer_wq, peer_keys, peer_u, peer_v)
    x = layer_norm(DEEPNORM_ALPHA * x + g2 * ffn, ln2_g, ln2_b)
    return x, st


def _copy_body(x_ref, o_ref):
    o_ref[...] = x_ref[...]


def _copy(x):
    b, t, d = x.shape
    return pl.pallas_call(
        _copy_body,
        grid=(b,),
        in_specs=[pl.BlockSpec((1, t, d), lambda i: (i, 0, 0))],
        out_specs=pl.BlockSpec((1, t, d), lambda i: (i, 0, 0)),
        out_shape=jax.ShapeDtypeStruct(x.shape, x.dtype),
    )(x)


def kernel(x_prompt, x_sample, state_rglru, c, c_ctx, w_mod, b_mod, w_in, conv_w, conv_b,
           w_rg_a, b_rg_a, w_rg_x, b_rg_x, lru_lambda, w_pool, pool_scale, w_out,
           ln1_g, ln1_b, ln2_g, ln2_b, peer_wq, peer_keys, peer_u, peer_v):
    def run(x, cvec, h0, l):
        return trunk_layer(x, cvec, h0, w_mod[l], b_mod[l], w_in[l], conv_w[l], conv_b[l],
                           w_rg_a[l], b_rg_a[l], w_rg_x[l], b_rg_x[l], lru_lambda[l], w_pool[l],
                           pool_scale[l], w_out[l], ln1_g[l], ln1_b[l], ln2_g[l], ln2_b[l],
                           peer_wq[l], peer_keys[l], peer_u[l], peer_v[l])

    xp = x_prompt
    zero_state = jnp.zeros((x_prompt.shape[0], 2, D_RNN), x_prompt.dtype)
    ctx_states = []
    rows = x_sample.shape[1] // GRID_W
    xs = x_sample + grid_pos_embed(rows, x_sample.dtype)[None]
    for l in range(DEPTH):
        xp, st = run(xp, c_ctx[None, :], zero_state, l)
        ctx_states.append(st)
        xs, _ = run(xs, c, state_rglru[:, l], l)
    new_state_rglru = jnp.stack(ctx_states, axis=1)
    return (_copy(xp), _copy(xs), new_state_rglru)
```

```python
import functools
import math

import jax
import jax.numpy as jnp
from jax import lax
from jax.experimental import pallas as pl
from jax.experimental.pallas import tpu as pltpu
from jax.experimental.pallas import tpu_sc as plsc

F32 = jnp.float32
BF16 = jnp.bfloat16

D_MODEL = 1024
DEPTH = 2
GRID_W = 64
D_RNN = 512
N_RNN_HEADS = 8
RNN_HEAD_DIM = D_RNN // N_RNN_HEADS
CONV_W = 4
LRU_C = 8.0
D_POOL = 512
POOL_WINDOWS = (2, 4, 8, 16)
POOL_GROUP_DIM = D_POOL // len(POOL_WINDOWS)
D_IN = 2 * D_RNN + D_POOL
PEER_HEADS = 8
PEER_NKEYS = 128
PEER_TOPK = 16
PEER_QDIM = 256
PEER_HALF = PEER_QDIM // 2
PEER_PICKS = PEER_HEADS * PEER_TOPK
N_MOD = 6
DEEPNORM_ALPHA = (2 * DEPTH) ** 0.25
LN_EPS = 1e-5

LANES = 128
SUBLANES = 8
VMEM_LIMIT_BYTES = 56 * 1024 * 1024
MOD_ROWS = 16
ROW_CHUNK = 256


def _params(*sem):
    return pltpu.CompilerParams(dimension_semantics=sem, vmem_limit_bytes=VMEM_LIMIT_BYTES)


def _layer_norm(y, g, b):
    mu = jnp.mean(y, -1, keepdims=True)
    var = jnp.mean(jnp.square(y - mu), -1, keepdims=True)
    return (y - mu) * lax.rsqrt(var + LN_EPS) * g + b


def _mod_body(c_ref, w_ref, b_ref, o_ref):
    c = c_ref[...]
    s = (c * jax.nn.sigmoid(c)).astype(BF16)
    o_ref[0] = jnp.dot(s, w_ref[0].astype(BF16), preferred_element_type=F32) + b_ref[0]


def _modulation(cvecs, w_mod, b_mod):
    n_out = N_MOD * D_MODEL
    tn = 1536
    return pl.pallas_call(
        _mod_body,
        grid=(DEPTH, n_out // tn),
        in_specs=[pl.BlockSpec((MOD_ROWS, D_MODEL), lambda l, j: (0, 0)),
                  pl.BlockSpec((1, D_MODEL, tn), lambda l, j: (l, 0, j)),
                  pl.BlockSpec((1, 1, tn), lambda l, j: (l, 0, j))],
        out_specs=pl.BlockSpec((1, MOD_ROWS, tn), lambda l, j: (l, 0, j)),
        out_shape=jax.ShapeDtypeStruct((DEPTH, MOD_ROWS, n_out), F32),
        compiler_params=_params("parallel", "parallel"),
        name="modulation",
    )(cvecs, w_mod, b_mod.reshape(DEPTH, 1, n_out))


def _add_pos_body(x_ref, p_ref, o_ref):
    o_ref[0] = x_ref[0] + p_ref[...]


def _add_pos(x, pos):
    b, t, d = x.shape
    tm = min(t, 512)
    return pl.pallas_call(
        _add_pos_body,
        grid=(b, t // tm),
        in_specs=[pl.BlockSpec((1, tm, d), lambda i, j: (i, j, 0)),
                  pl.BlockSpec((tm, d), lambda i, j: (j, 0))],
        out_specs=pl.BlockSpec((1, tm, d), lambda i, j: (i, j, 0)),
        out_shape=jax.ShapeDtypeStruct(x.shape, x.dtype),
        compiler_params=_params("parallel", "parallel"),
        name="add_pos",
    )(x, pos)


def _mix_in_body(x_ref, m_ref, w_ref, z_ref):
    m = m_ref[0]
    h = x_ref[0] * (1.0 + m[1:2]) + m[0:1]
    z_ref[0] = jnp.dot(h.astype(BF16), w_ref[...], preferred_element_type=F32)


def _mod_spec(mods):
    if mods.shape[0] == 1:
        return pl.BlockSpec((1, SUBLANES, D_MODEL), lambda i, j: (0, 0, 0))
    return pl.BlockSpec((1, SUBLANES, D_MODEL), lambda i, j: (i, 0, 0))


def _mix_in(x, mods, w_in):
    b, t, d = x.shape
    tm = min(t, 512)
    return pl.pallas_call(
        _mix_in_body,
        grid=(b, t // tm),
        in_specs=[pl.BlockSpec((1, tm, d), lambda i, j: (i, j, 0)),
                  _mod_spec(mods),
                  pl.BlockSpec((d, D_IN), lambda i, j: (0, 0))],
        out_specs=pl.BlockSpec((1, tm, D_IN), lambda i, j: (i, j, 0)),
        out_shape=jax.ShapeDtypeStruct((b, t, D_IN), F32),
        compiler_params=_params("parallel", "parallel"),
        name="mix_in",
    )(x, mods, w_in)


PAD = SUBLANES
UNROLL_GROUPS = 8


def _rnn_body(xr_ref, gr_ref, h0_ref, cw_ref, cb_ref, wa_ref, wx_ref, ba_ref, bx_ref, lam_ref,
              out_ref, st_ref, xpad, a_s, u_s):
    t_len = xr_ref.shape[1]
    rc = min(ROW_CHUNK, t_len)
    zeros = jnp.zeros((PAD, LANES), F32)
    xpad[0:PAD] = zeros
    xpad[t_len + PAD:t_len + 2 * PAD] = zeros
    xpad[PAD:t_len + PAD] = xr_ref[0]

    cw = cw_ref[...]
    lam = lam_ref[...]
    softplus_neg = jnp.maximum(-lam, 0.0) + jnp.log1p(jnp.exp(-jnp.abs(lam)))
    coef = -LRU_C * softplus_neg
    row = lax.broadcasted_iota(jnp.int32, (rc, LANES), 0) % SUBLANES
    left = CONV_W // 2

    for c in range(t_len // rc):
        t0 = c * rc
        xc = cb_ref[...] + xpad[t0 + PAD - left:t0 + PAD - left + rc] * cw[0:1]
        for k in range(1, CONV_W):
            xc = xc + xpad[t0 + PAD - left + k:t0 + PAD - left + k + rc] * cw[k:k + 1]
        xcb = xc.astype(BF16)
        for d in range(2):
            r = jax.nn.sigmoid(jnp.dot(xcb, wa_ref[d, 0], preferred_element_type=F32) + ba_ref[d:d + 1])
            gi = jax.nn.sigmoid(jnp.dot(xcb, wx_ref[d, 0], preferred_element_type=F32) + bx_ref[d:d + 1])
            log_a = coef[d:d + 1] * r
            a = jnp.exp(log_a)
            u = jnp.sqrt(-jnp.tanh(log_a) * (a * a + 1.0)) * (gi * xc)
            for s in (1, 2, 4):
                if d == 0:
                    keep = row >= s
                    shift = s
                else:
                    keep = row <= SUBLANES - 1 - s
                    shift = rc - s
                a_sh = jnp.where(keep, pltpu.roll(a, shift, 0), 1.0)
                u_sh = jnp.where(keep, pltpu.roll(u, shift, 0), 0.0)
                u = a * u_sh + u
                a = a * a_sh
            a_s[d, t0:t0 + rc] = a
            u_s[d, t0:t0 + rc] = u

    n_groups = t_len // SUBLANES
    n_iter = n_groups // UNROLL_GROUPS

    def carry_step(i, carry):
        hf, hb = carry
        for j in range(UNROLL_GROUPS):
            gf = pl.multiple_of((i * UNROLL_GROUPS + j) * SUBLANES, SUBLANES)
            gb = pl.multiple_of((n_groups - 1 - i * UNROLL_GROUPS - j) * SUBLANES, SUBLANES)
            h_f = u_s[0, pl.ds(gf, SUBLANES)] + a_s[0, pl.ds(gf, SUBLANES)] * hf
            h_b = u_s[1, pl.ds(gb, SUBLANES)] + a_s[1, pl.ds(gb, SUBLANES)] * hb
            u_s[0, pl.ds(gf, SUBLANES)] = h_f
            u_s[1, pl.ds(gb, SUBLANES)] = h_b
            hf = h_f[SUBLANES - 1:SUBLANES]
            hb = h_b[0:1]
        return hf, hb

    h0 = h0_ref[0]
    hf, hb = lax.fori_loop(0, n_iter, carry_step, (h0[0:1], h0[1:2]))
    st_ref[0, 0:1] = hf
    st_ref[0, 1:2] = hb

    for c in range(t_len // rc):
        t0 = c * rc
        hsum = u_s[0, t0:t0 + rc] + u_s[1, t0:t0 + rc]
        out_ref[0, t0:t0 + rc] = (hsum * jax.nn.gelu(gr_ref[0, t0:t0 + rc])).astype(BF16)


def _rnn(z, h0, conv_w, conv_b, wa, wx, b_a, b_x, lam):
    b, t, _ = z.shape
    nblk = D_RNN // LANES
    return pl.pallas_call(
        _rnn_body,
        grid=(b, nblk),
        in_specs=[pl.BlockSpec((1, t, LANES), lambda i, j: (i, 0, j)),
                  pl.BlockSpec((1, t, LANES), lambda i, j: (i, 0, nblk + j)),
                  pl.BlockSpec((1, 2, LANES), lambda i, j: (i, 0, j)),
                  pl.BlockSpec((CONV_W, LANES), lambda i, j: (0, j)),
                  pl.BlockSpec((1, LANES), lambda i, j: (0, j)),
                  pl.BlockSpec((2, 1, LANES, LANES), lambda i, j: (0, j, 0, 0)),
                  pl.BlockSpec((2, 1, LANES, LANES), lambda i, j: (0, j, 0, 0)),
                  pl.BlockSpec((2, LANES), lambda i, j: (0, j)),
                  pl.BlockSpec((2, LANES), lambda i, j: (0, j)),
                  pl.BlockSpec((2, LANES), lambda i, j: (0, j))],
        out_specs=[pl.BlockSpec((1, t, LANES), lambda i, j: (i, 0, j)),
                   pl.BlockSpec((1, 2, LANES), lambda i, j: (i, 0, j))],
        out_shape=[jax.ShapeDtypeStruct((b, t, D_RNN), BF16),
                   jax.ShapeDtypeStruct((b, 2, D_RNN), F32)],
        scratch_shapes=[pltpu.VMEM((t + 2 * PAD, LANES), F32),
                        pltpu.VMEM((2, t, LANES), F32),
                        pltpu.VMEM((2, t, LANES), F32)],
        compiler_params=_params("parallel", "parallel"),
        name="rglru",
    )(z, z, h0, conv_w, conv_b.reshape(1, D_RNN), wa, wx, b_a, b_x, lam)


def _block_diag_pairs(w):
    per = LANES // RNN_HEAD_DIM
    w = w.reshape(2, D_RNN // LANES, per, RNN_HEAD_DIM, RNN_HEAD_DIM)
    out = jnp.zeros((2, D_RNN // LANES, LANES, LANES), w.dtype)
    for p in range(per):
        sl = slice(p * RNN_HEAD_DIM, (p + 1) * RNN_HEAD_DIM)
        out = out.at[:, :, sl, sl].set(w[:, :, p])
    return out.astype(BF16)


POOL_PAD = 16


def _pool_body(xq_ref, w_ref, sc_ref, out_ref, ppad):
    t_len = xq_ref.shape[1]
    rc = min(ROW_CHUNK, t_len)
    zeros = jnp.zeros((POOL_PAD, D_POOL), F32)
    ppad[0:POOL_PAD] = zeros
    ppad[t_len + POOL_PAD:t_len + 2 * POOL_PAD] = zeros
    ppad[POOL_PAD:t_len + POOL_PAD] = xq_ref[0]
    for c in range(t_len // rc):
        t0 = c * rc
        tpos = t0 + lax.broadcasted_iota(jnp.int32, (rc, POOL_GROUP_DIM), 0)
        for g, w in enumerate(POOL_WINDOWS):
            cols = slice(g * POOL_GROUP_DIM, (g + 1) * POOL_GROUP_DIM)
            half = w // 2
            base = t0 + POOL_PAD - half
            s = ppad[base:base + rc, cols]
            for k in range(1, w):
                s = s + ppad[base + k:base + k + rc, cols]
            cnt = (jnp.minimum(tpos + half, t_len) - jnp.maximum(tpos - half, 0)).astype(F32)
            pooled = s / cnt - ppad[t0 + POOL_PAD:t0 + POOL_PAD + rc, cols]
            y = jnp.dot(pooled.astype(BF16), w_ref[g], preferred_element_type=F32)
            out_ref[0, t0:t0 + rc, cols] = (y * sc_ref[:, cols]).astype(BF16)


def _pool(z, w_pool, pool_scale):
    b, t, _ = z.shape
    return pl.pallas_call(
        _pool_body,
        grid=(b,),
        in_specs=[pl.BlockSpec((1, t, D_POOL), lambda i: (i, 0, 2 * D_RNN // D_POOL)),
                  pl.BlockSpec((len(POOL_WINDOWS), POOL_GROUP_DIM, POOL_GROUP_DIM), lambda i: (0, 0, 0)),
                  pl.BlockSpec((1, D_POOL), lambda i: (0, 0))],
        out_specs=pl.BlockSpec((1, t, D_POOL), lambda i: (i, 0, 0)),
        out_shape=jax.ShapeDtypeStruct((b, t, D_POOL), BF16),
        scratch_shapes=[pltpu.VMEM((t + 2 * POOL_PAD, D_POOL), F32)],
        compiler_params=_params("parallel"),
        name="pool",
    )(z, w_pool, pool_scale.reshape(1, D_POOL))


def _mix_out_body(rnn_ref, pool_ref, x_ref, m_ref, w_ref, g_ref, b_ref, x1_ref, hq_ref):
    m = m_ref[0]
    mix = (jnp.dot(rnn_ref[0], w_ref[0:D_RNN], preferred_element_type=F32)
           + jnp.dot(pool_ref[0], w_ref[D_RNN:D_RNN + D_POOL], preferred_element_type=F32))
    x1 = _layer_norm(DEEPNORM_ALPHA * x_ref[0] + m[2:3] * mix, g_ref[...], b_ref[...])
    x1_ref[0] = x1
    hq_ref[0] = x1 * (1.0 + m[4:5]) + m[3:4]


def _mix_out(rnn, pool, x, mods, w_out, ln_g, ln_b):
    b, t, d = x.shape
    tm = min(t, 512)
    tok = lambda i, j: (i, j, 0)
    return pl.pallas_call(
        _mix_out_body,
        grid=(b, t // tm),
        in_specs=[pl.BlockSpec((1, tm, D_RNN), tok),
                  pl.BlockSpec((1, tm, D_POOL), tok),
                  pl.BlockSpec((1, tm, d), tok),
                  _mod_spec(mods),
                  pl.BlockSpec((D_RNN + D_POOL, d), lambda i, j: (0, 0)),
                  pl.BlockSpec((1, d), lambda i, j: (0, 0)),
                  pl.BlockSpec((1, d), lambda i, j: (0, 0))],
        out_specs=[pl.BlockSpec((1, tm, d), tok), pl.BlockSpec((1, tm, d), tok)],
        out_shape=[jax.ShapeDtypeStruct(x.shape, F32), jax.ShapeDtypeStruct(x.shape, F32)],
        compiler_params=_params("parallel", "parallel"),
        name="mix_out",
    )(rnn, pool, x, mods, w_out, ln_g.reshape(1, d), ln_b.reshape(1, d))


NEG_INF = float("-inf")


def _top_rows(s, ids, k):
    vals, picks = [], []
    big = jnp.int32(2 ** 30)
    for _ in range(k):
        m = jnp.max(s, axis=0, keepdims=True)
        pick = jnp.min(jnp.where(s == m, ids, big), axis=0, keepdims=True)
        s = jnp.where(ids == pick, NEG_INF, s)
        vals.append(m)
        picks.append(pick)
    return jnp.concatenate(vals, axis=0), jnp.concatenate(picks, axis=0)


def _gather_rows(table, ids, picks):
    out = []
    for r in range(picks.shape[0]):
        out.append(jnp.sum(jnp.where(ids == picks[r:r + 1], table, 0), axis=0, keepdims=True))
    return jnp.concatenate(out, axis=0)


def _route_body(hq_ref, wq_ref, keys_ref, idx_ref, gate_ref):
    tm = hq_ref.shape[0]
    q = jnp.dot(hq_ref[...].astype(BF16), wq_ref[...], preferred_element_type=F32)
    key_ids = lax.broadcasted_iota(jnp.int32, (PEER_NKEYS, tm), 0)
    half_rows = SUBLANES
    sub = lax.broadcasted_iota(jnp.int32, (half_rows, tm), 0)
    for h in range(PEER_HEADS):
        tops = []
        for p in range(2):
            c0 = (h * 2 + p) * PEER_HALF
            qs = q[:, c0:c0 + PEER_HALF].astype(BF16)
            s_t = lax.dot_general(keys_ref[h, p], qs, (((1,), (1,)), ((), ())),
                                  preferred_element_type=F32)
            tops.append(_top_rows(s_t, key_ids, PEER_TOPK))
        (s1, i1), (s2, i2) = tops
        cand, flat, expert = [], [], []
        for j in range(half_rows):
            ok = (sub + 1) * (j + 1) <= PEER_TOPK
            cand.append(jnp.where(ok, s1[0:half_rows] + s2[j:j + 1], NEG_INF))
            flat.append(sub * PEER_TOPK + j)
            expert.append(i1[0:half_rows] * PEER_NKEYS + i2[j:j + 1])
        cand.append(s1[0:1] + s2[half_rows:2 * half_rows])
        flat.append(sub + half_rows)
        expert.append(i1[0:1] * PEER_NKEYS + i2[half_rows:2 * half_rows])
        cand.append(s1[half_rows:2 * half_rows] + s2[0:1])
        flat.append((sub + half_rows) * PEER_TOPK)
        expert.append(i1[half_rows:2 * half_rows] * PEER_NKEYS + i2[0:1])
        cand = jnp.concatenate(cand, axis=0)
        flat = jnp.concatenate(flat, axis=0)
        expert = jnp.concatenate(expert, axis=0)
        sc, picks = _top_rows(cand, flat, PEER_TOPK)
        e = jnp.exp(sc - sc[0:1])
        rows = slice(h * PEER_TOPK, (h + 1) * PEER_TOPK)
        gate_ref[rows, :] = e / jnp.sum(e, axis=0, keepdims=True)
        idx_ref[rows, :] = _gather_rows(expert, flat, picks)


def _route(hq, wq, keys):
    n, d = hq.shape
    tm = 256
    return pl.pallas_call(
        _route_body,
        grid=(n // tm,),
        in_specs=[pl.BlockSpec((tm, d), lambda i: (i, 0)),
                  pl.BlockSpec((d, PEER_HEADS * PEER_QDIM), lambda i: (0, 0)),
                  pl.BlockSpec((PEER_HEADS, 2, PEER_NKEYS, PEER_HALF), lambda i: (0, 0, 0, 0))],
        out_specs=[pl.BlockSpec((PEER_PICKS, tm), lambda i: (0, i)),
                   pl.BlockSpec((PEER_PICKS, tm), lambda i: (0, i))],
        out_shape=[jax.ShapeDtypeStruct((PEER_PICKS, n), jnp.int32),
                   jax.ShapeDtypeStruct((PEER_PICKS, n), F32)],
        compiler_params=_params("parallel"),
        name="peer_route",
    )(hq, wq, keys)


def _final_body(x1_ref, f_ref, m_ref, g_ref, b_ref, o_ref):
    m = m_ref[0]
    o_ref[0] = _layer_norm(DEEPNORM_ALPHA * x1_ref[0] + m[5:6] * f_ref[0], g_ref[...], b_ref[...])


def _final(x1, ffn, mods, ln_g, ln_b):
    b, t, d = x1.shape
    tm = min(t, 512)
    tok = lambda i, j: (i, j, 0)
    return pl.pallas_call(
        _final_body,
        grid=(b, t // tm),
        in_specs=[pl.BlockSpec((1, tm, d), tok), pl.BlockSpec((1, tm, d), tok), _mod_spec(mods),
                  pl.BlockSpec((1, d), lambda i, j: (0, 0)), pl.BlockSpec((1, d), lambda i, j: (0, 0))],
        out_specs=pl.BlockSpec((1, tm, d), tok),
        out_shape=jax.ShapeDtypeStruct(x1.shape, F32),
        compiler_params=_params("parallel", "parallel"),
        name="final_ln",
    )(x1, ffn, mods, ln_g.reshape(1, d), ln_b.reshape(1, d))


GATHER_ROWS = 32
EVAL_TOKENS = 8
EVAL_BLOCK = 128
EXPERT_TOKENS = 2048


def _sc_gather_rows(table, idx):
    n = idx.shape[0]
    d = table.shape[1]
    info = plsc.get_sparse_core_info()
    n_workers = info.num_cores * info.num_subcores
    per_w = n // n_workers
    rows = GATHER_ROWS
    assert n % n_workers == 0 and per_w % (2 * rows) == 0
    mesh = plsc.VectorSubcoreMesh(core_axis_name="core", subcore_axis_name="subcore")

    @functools.partial(
        pl.kernel, out_type=jax.ShapeDtypeStruct((n, d), table.dtype), mesh=mesh,
        scratch_types=[pltpu.VMEM((per_w,), jnp.int32),
                       pltpu.VMEM((rows, d), table.dtype), pltpu.VMEM((rows, d), table.dtype),
                       pltpu.SemaphoreType.DMA, pltpu.SemaphoreType.DMA,
                       pltpu.SemaphoreType.DMA, pltpu.SemaphoreType.DMA],
        name="sc_gather_rows")
    def gather(tab_hbm, idx_hbm, out_hbm, idx_v, rows0, rows1, g0, g1, w0, w1):
        wid = lax.axis_index("subcore") * info.num_cores + lax.axis_index("core")
        base = wid * per_w
        pltpu.sync_copy(idx_hbm.at[pl.ds(base, per_w)], idx_v)

        @pl.loop(0, per_w // (2 * rows))
        def _(c):
            off0 = pl.multiple_of(c * (2 * rows), rows)
            off1 = off0 + rows
            ga = pltpu.async_copy(tab_hbm.at[idx_v.at[pl.ds(off0, rows)]], rows0, g0)
            gb = pltpu.async_copy(tab_hbm.at[idx_v.at[pl.ds(off1, rows)]], rows1, g1)
            ga.wait()
            wa = pltpu.async_copy(rows0, out_hbm.at[pl.ds(base + off0, rows)], w0)
            gb.wait()
            wb = pltpu.async_copy(rows1, out_hbm.at[pl.ds(base + off1, rows)], w1)
            wa.wait()
            wb.wait()

    return gather(table, idx)


def _eval_body(hq_ref, gate_ref, gu_ref, gv_ref, out_ref):
    j = pl.program_id(1)
    r0 = pl.multiple_of(j * EVAL_TOKENS, EVAL_TOKENS)
    x = hq_ref[pl.ds(r0, EVAL_TOKENS), :]
    g_t = gate_ref[pl.ds(r0, EVAL_TOKENS), :].T
    for t in range(EVAL_TOKENS):
        act = jnp.sum(gu_ref[t] * x[t:t + 1], axis=1, keepdims=True)
        act = jax.nn.gelu(act) * g_t[:, t:t + 1]
        out_ref[pl.ds(r0 + t, 1), :] = jnp.sum(act * gv_ref[t], axis=0, keepdims=True)


def _expert_eval(hq, gates, gu, gv):
    n, d = hq.shape
    gu = gu.reshape(n, PEER_PICKS, d)
    gv = gv.reshape(n, PEER_PICKS, d)
    steps = EVAL_BLOCK // EVAL_TOKENS
    return pl.pallas_call(
        _eval_body,
        grid=(n // EVAL_BLOCK, steps),
        in_specs=[pl.BlockSpec((EVAL_BLOCK, d), lambda i, j: (i, 0)),
                  pl.BlockSpec((EVAL_BLOCK, PEER_PICKS), lambda i, j: (i, 0)),
                  pl.BlockSpec((EVAL_TOKENS, PEER_PICKS, d), lambda i, j: (i * steps + j, 0, 0)),
                  pl.BlockSpec((EVAL_TOKENS, PEER_PICKS, d), lambda i, j: (i * steps + j, 0, 0))],
        out_specs=pl.BlockSpec((EVAL_BLOCK, d), lambda i, j: (i, 0)),
        out_shape=jax.ShapeDtypeStruct((n, d), F32),
        compiler_params=_params("parallel", "arbitrary"),
        name="expert_eval",
    )(hq, gates, gu, gv)


def _experts(hq, idx, gates, u_tab, v_tab):
    n = hq.shape[0]
    outs = []
    for s in range(0, n, EXPERT_TOKENS):
        flat = idx[s:s + EXPERT_TOKENS].reshape(-1)
        gu = _sc_gather_rows(u_tab, flat)
        gv = _sc_gather_rows(v_tab, flat)
        outs.append(_expert_eval(hq[s:s + EXPERT_TOKENS], gates[s:s + EXPERT_TOKENS], gu, gv))
    return jnp.concatenate(outs, axis=0)


def _grid_pos_embed(rows):
    t = jnp.arange(rows * GRID_W)
    r = (t // GRID_W).astype(F32)
    col = (t % GRID_W).astype(F32)
    n_freq = D_MODEL // 4
    omega = 1.0 / (10000.0 ** (jnp.arange(n_freq, dtype=F32) / n_freq))

    def enc(p):
        ang = p[:, None] * omega[None, :]
        return jnp.concatenate([jnp.sin(ang), jnp.cos(ang)], -1)
    return jnp.concatenate([enc(r), enc(col)], -1)


def _layer(x, mods, h0, p):
    b, t, d = x.shape
    z = _mix_in(x, mods, p["w_in"])
    rnn, st = _rnn(z, h0, p["conv_w"], p["conv_b"], p["wa"], p["wx"], p["b_a"], p["b_x"], p["lam"])
    pool = _pool(z, p["w_pool"], p["pool_scale"])
    x1, hq = _mix_out(rnn, pool, x, mods, p["w_out"], p["ln1_g"], p["ln1_b"])
    hq = hq.reshape(b * t, d)
    idx_t, gate_t = _route(hq, p["wq"], p["keys"])
    ffn = _experts(hq, idx_t.T, gate_t.T, p["peer_u"], p["peer_v"])
    x2 = _final(x1, ffn.reshape(b, t, d), mods, p["ln2_g"], p["ln2_b"])
    return x2, st


def kernel(x_prompt, x_sample, state_rglru, c, c_ctx, w_mod, b_mod, w_in, conv_w, conv_b,
           w_rg_a, b_rg_a, w_rg_x, b_rg_x, lru_lambda, w_pool, pool_scale, w_out,
           ln1_g, ln1_b, ln2_g, ln2_b, peer_wq, peer_keys, peer_u, peer_v):
    n_req = c.shape[0]
    cvecs = jnp.zeros((MOD_ROWS, D_MODEL), F32).at[:n_req].set(c).at[n_req].set(c_ctx)
    mod = _modulation(cvecs, w_mod, b_mod).reshape(DEPTH, MOD_ROWS, N_MOD, D_MODEL)
    mod = jnp.pad(mod, ((0, 0), (0, 0), (0, SUBLANES - N_MOD), (0, 0)))

    rows = x_sample.shape[1] // GRID_W
    xs = _add_pos(x_sample, _grid_pos_embed(rows))
    xp = x_prompt
    zero_state = jnp.zeros((x_prompt.shape[0], 2, D_RNN), F32)
    ctx_states = []
    for l in range(DEPTH):
        p = dict(w_in=w_in[l].astype(BF16), conv_w=conv_w[l], conv_b=conv_b[l],
                 wa=_block_diag_pairs(w_rg_a[l]), wx=_block_diag_pairs(w_rg_x[l]),
                 b_a=b_rg_a[l], b_x=b_rg_x[l], lam=lru_lambda[l],
                 w_pool=w_pool[l].astype(BF16), pool_scale=pool_scale[l], w_out=w_out[l].astype(BF16),
                 ln1_g=ln1_g[l], ln1_b=ln1_b[l], ln2_g=ln2_g[l], ln2_b=ln2_b[l],
                 wq=peer_wq[l].astype(BF16), keys=peer_keys[l].astype(BF16),
                 peer_u=peer_u[l], peer_v=peer_v[l])
        xp, st = _layer(xp, mod[l, n_req:n_req + 1], zero_state, p)
        ctx_states.append(st)
        xs, _ = _layer(xs, mod[l, :n_req], state_rglru[:, l], p)
    return xp, xs, jnp.stack(ctx_states, axis=1)
```

```python
import functools
import math

import jax
import jax.numpy as jnp
from jax import lax
from jax.experimental import pallas as pl
from jax.experimental.pallas import tpu as pltpu
from jax.experimental.pallas import tpu_sc as plsc

F32 = jnp.float32
BF16 = jnp.bfloat16

D_MODEL = 1024
DEPTH = 2
GRID_W = 64
D_RNN = 512
N_RNN_HEADS = 8
RNN_HEAD_DIM = D_RNN // N_RNN_HEADS
CONV_W = 4
LRU_C = 8.0
D_POOL = 512
POOL_WINDOWS = (2, 4, 8, 16)
POOL_GROUP_DIM = D_POOL // len(POOL_WINDOWS)
D_IN = 2 * D_RNN + D_POOL
PEER_HEADS = 8
PEER_NKEYS = 128
PEER_TOPK = 16
PEER_QDIM = 256
PEER_HALF = PEER_QDIM // 2
PEER_PICKS = PEER_HEADS * PEER_TOPK
N_MOD = 6
DEEPNORM_ALPHA = (2 * DEPTH) ** 0.25
LN_EPS = 1e-5

LANES = 128
SUBLANES = 8
VMEM_LIMIT_BYTES = 56 * 1024 * 1024
MOD_ROWS = 16
ROW_CHUNK = 256


def _params(*sem):
    return pltpu.CompilerParams(dimension_semantics=sem, vmem_limit_bytes=VMEM_LIMIT_BYTES)


def _layer_norm(y, g, b):
    mu = jnp.mean(y, -1, keepdims=True)
    var = jnp.mean(jnp.square(y - mu), -1, keepdims=True)
    return (y - mu) * lax.rsqrt(var + LN_EPS) * g + b


def _mod_body(c_ref, w_ref, b_ref, o_ref):
    c = c_ref[...]
    s = (c * jax.nn.sigmoid(c)).astype(BF16)
    o_ref[0] = jnp.dot(s, w_ref[0].astype(BF16), preferred_element_type=F32) + b_ref[0]


def _modulation(cvecs, w_mod, b_mod):
    n_out = N_MOD * D_MODEL
    tn = 1536
    return pl.pallas_call(
        _mod_body,
        grid=(DEPTH, n_out // tn),
        in_specs=[pl.BlockSpec((MOD_ROWS, D_MODEL), lambda l, j: (0, 0)),
                  pl.BlockSpec((1, D_MODEL, tn), lambda l, j: (l, 0, j)),
                  pl.BlockSpec((1, 1, tn), lambda l, j: (l, 0, j))],
        out_specs=pl.BlockSpec((1, MOD_ROWS, tn), lambda l, j: (l, 0, j)),
        out_shape=jax.ShapeDtypeStruct((DEPTH, MOD_ROWS, n_out), F32),
        compiler_params=_params("parallel", "parallel"),
        name="modulation",
    )(cvecs, w_mod, b_mod.reshape(DEPTH, 1, n_out))


def _add_pos_body(x_ref, p_ref, o_ref):
    o_ref[0] = x_ref[0] + p_ref[...]


def _add_pos(x, pos):
    b, t, d = x.shape
    tm = min(t, 512)
    return pl.pallas_call(
        _add_pos_body,
        grid=(b, t // tm),
        in_specs=[pl.BlockSpec((1, tm, d), lambda i, j: (i, j, 0)),
                  pl.BlockSpec((tm, d), lambda i, j: (j, 0))],
        out_specs=pl.BlockSpec((1, tm, d), lambda i, j: (i, j, 0)),
        out_shape=jax.ShapeDtypeStruct(x.shape, x.dtype),
        compiler_params=_params("parallel", "parallel"),
        name="add_pos",
    )(x, pos)


def _mix_in_body(x_ref, m_ref, w_ref, z_ref):
    m = m_ref[0]
    h = x_ref[0] * (1.0 + m[1:2]) + m[0:1]
    z_ref[0] = jnp.dot(h.astype(BF16), w_ref[...], preferred_element_type=F32)


def _mod_spec(mods):
    if mods.shape[0] == 1:
        return pl.BlockSpec((1, SUBLANES, D_MODEL), lambda i, j: (0, 0, 0))
    return pl.BlockSpec((1, SUBLANES, D_MODEL), lambda i, j: (i, 0, 0))


def _mix_in(x, mods, w_in):
    b, t, d = x.shape
    tm = min(t, 512)
    return pl.pallas_call(
        _mix_in_body,
        grid=(b, t // tm),
        in_specs=[pl.BlockSpec((1, tm, d), lambda i, j: (i, j, 0)),
                  _mod_spec(mods),
                  pl.BlockSpec((d, D_IN), lambda i, j: (0, 0))],
        out_specs=pl.BlockSpec((1, tm, D_IN), lambda i, j: (i, j, 0)),
        out_shape=jax.ShapeDtypeStruct((b, t, D_IN), F32),
        compiler_params=_params("parallel", "parallel"),
        name="mix_in",
    )(x, mods, w_in)


PAD = SUBLANES
UNROLL_GROUPS = 8


def _rnn_body(xr_ref, gr_ref, h0_ref, cw_ref, cb_ref, wa_ref, wx_ref, ba_ref, bx_ref, lam_ref,
              out_ref, st_ref, xpad, a_s, u_s):
    t_len = xr_ref.shape[1]
    rc = min(ROW_CHUNK, t_len)
    zeros = jnp.zeros((PAD, LANES), F32)
    xpad[0:PAD] = zeros
    xpad[t_len + PAD:t_len + 2 * PAD] = zeros
    xpad[PAD:t_len + PAD] = xr_ref[0]

    cw = cw_ref[...]
    lam = lam_ref[...]
    softplus_neg = jnp.maximum(-lam, 0.0) + jnp.log1p(jnp.exp(-jnp.abs(lam)))
    coef = -LRU_C * softplus_neg
    row = lax.broadcasted_iota(jnp.int32, (rc, LANES), 0) % SUBLANES
    left = CONV_W // 2

    for c in range(t_len // rc):
        t0 = c * rc
        xc = cb_ref[...] + xpad[t0 + PAD - left:t0 + PAD - left + rc] * cw[0:1]
        for k in range(1, CONV_W):
            xc = xc + xpad[t0 + PAD - left + k:t0 + PAD - left + k + rc] * cw[k:k + 1]
        xcb = xc.astype(BF16)
        for d in range(2):
            r = jax.nn.sigmoid(jnp.dot(xcb, wa_ref[d, 0], preferred_element_type=F32) + ba_ref[d:d + 1])
            gi = jax.nn.sigmoid(jnp.dot(xcb, wx_ref[d, 0], preferred_element_type=F32) + bx_ref[d:d + 1])
            log_a = coef[d:d + 1] * r
            a = jnp.exp(log_a)
            u = jnp.sqrt(-jnp.tanh(log_a) * (a * a + 1.0)) * (gi * xc)
            for s in (1, 2, 4):
                if d == 0:
                    keep = row >= s
                    shift = s
                else:
                    keep = row <= SUBLANES - 1 - s
                    shift = rc - s
                a_sh = jnp.where(keep, pltpu.roll(a, shift, 0), 1.0)
                u_sh = jnp.where(keep, pltpu.roll(u, shift, 0), 0.0)
                u = a * u_sh + u
                a = a * a_sh
            a_s[d, t0:t0 + rc] = a
            u_s[d, t0:t0 + rc] = u

    n_groups = t_len // SUBLANES
    n_iter = n_groups // UNROLL_GROUPS

    def carry_step(i, carry):
        hf, hb = carry
        for j in range(UNROLL_GROUPS):
            gf = pl.multiple_of((i * UNROLL_GROUPS + j) * SUBLANES, SUBLANES)
            gb = pl.multiple_of((n_groups - 1 - i * UNROLL_GROUPS - j) * SUBLANES, SUBLANES)
            h_f = u_s[0, pl.ds(gf, SUBLANES)] + a_s[0, pl.ds(gf, SUBLANES)] * hf
            h_b = u_s[1, pl.ds(gb, SUBLANES)] + a_s[1, pl.ds(gb, SUBLANES)] * hb
            u_s[0, pl.ds(gf, SUBLANES)] = h_f
            u_s[1, pl.ds(gb, SUBLANES)] = h_b
            hf = h_f[SUBLANES - 1:SUBLANES]
            hb = h_b[0:1]
        return hf, hb

    h0 = h0_ref[0]
    hf, hb = lax.fori_loop(0, n_iter, carry_step, (h0[0:1], h0[1:2]))
    st_ref[0, 0:1] = hf
    st_ref[0, 1:2] = hb

    for c in range(t_len // rc):
        t0 = c * rc
        hsum = u_s[0, t0:t0 + rc] + u_s[1, t0:t0 + rc]
        out_ref[0, t0:t0 + rc] = (hsum * jax.nn.gelu(gr_ref[0, t0:t0 + rc])).astype(BF16)


def _rnn(z, h0, conv_w, conv_b, wa, wx, b_a, b_x, lam):
    b, t, _ = z.shape
    nblk = D_RNN // LANES
    return pl.pallas_call(
        _rnn_body,
        grid=(b, nblk),
        in_specs=[pl.BlockSpec((1, t, LANES), lambda i, j: (i, 0, j)),
                  pl.BlockSpec((1, t, LANES), lambda i, j: (i, 0, nblk + j)),
                  pl.BlockSpec((1, 2, LANES), lambda i, j: (i, 0, j)),
                  pl.BlockSpec((CONV_W, LANES), lambda i, j: (0, j)),
                  pl.BlockSpec((1, LANES), lambda i, j: (0, j)),
                  pl.BlockSpec((2, 1, LANES, LANES), lambda i, j: (0, j, 0, 0)),
                  pl.BlockSpec((2, 1, LANES, LANES), lambda i, j: (0, j, 0, 0)),
                  pl.BlockSpec((2, LANES), lambda i, j: (0, j)),
                  pl.BlockSpec((2, LANES), lambda i, j: (0, j)),
                  pl.BlockSpec((2, LANES), lambda i, j: (0, j))],
        out_specs=[pl.BlockSpec((1, t, LANES), lambda i, j: (i, 0, j)),
                   pl.BlockSpec((1, 2, LANES), lambda i, j: (i, 0, j))],
        out_shape=[jax.ShapeDtypeStruct((b, t, D_RNN), BF16),
                   jax.ShapeDtypeStruct((b, 2, D_RNN), F32)],
        scratch_shapes=[pltpu.VMEM((t + 2 * PAD, LANES), F32),
                        pltpu.VMEM((2, t, LANES), F32),
                        pltpu.VMEM((2, t, LANES), F32)],
        compiler_params=_params("parallel", "parallel"),
        name="rglru",
    )(z, z, h0, conv_w, conv_b.reshape(1, D_RNN), wa, wx, b_a, b_x, lam)


def _block_diag_pairs(w):
    per = LANES // RNN_HEAD_DIM
    w = w.reshape(2, D_RNN // LANES, per, RNN_HEAD_DIM, RNN_HEAD_DIM)
    out = jnp.zeros((2, D_RNN // LANES, LANES, LANES), w.dtype)
    for p in range(per):
        sl = slice(p * RNN_HEAD_DIM, (p + 1) * RNN_HEAD_DIM)
        out = out.at[:, :, sl, sl].set(w[:, :, p])
    return out.astype(BF16)


POOL_PAD = 16


def _pool_body(xq_ref, w_ref, sc_ref, out_ref, ppad):
    t_len = xq_ref.shape[1]
    rc = min(ROW_CHUNK, t_len)
    zeros = jnp.zeros((POOL_PAD, D_POOL), F32)
    ppad[0:POOL_PAD] = zeros
    ppad[t_len + POOL_PAD:t_len + 2 * POOL_PAD] = zeros
    ppad[POOL_PAD:t_len + POOL_PAD] = xq_ref[0]
    for c in range(t_len // rc):
        t0 = c * rc
        tpos = t0 + lax.broadcasted_iota(jnp.int32, (rc, POOL_GROUP_DIM), 0)
        for g, w in enumerate(POOL_WINDOWS):
            cols = slice(g * POOL_GROUP_DIM, (g + 1) * POOL_GROUP_DIM)
            half = w // 2
            base = t0 + POOL_PAD - half
            s = ppad[base:base + rc, cols]
            for k in range(1, w):
                s = s + ppad[base + k:base + k + rc, cols]
            cnt = (jnp.minimum(tpos + half, t_len) - jnp.maximum(tpos - half, 0)).astype(F32)
            pooled = s / cnt - ppad[t0 + POOL_PAD:t0 + POOL_PAD + rc, cols]
            y = jnp.dot(pooled.astype(BF16), w_ref[g], preferred_element_type=F32)
            out_ref[0, t0:t0 + rc, cols] = (y * sc_ref[:, cols]).astype(BF16)


def _pool(z, w_pool, pool_scale):
    b, t, _ = z.shape
    return pl.pallas_call(
        _pool_body,
        grid=(b,),
        in_specs=[pl.BlockSpec((1, t, D_POOL), lambda i: (i, 0, 2 * D_RNN // D_POOL)),
                  pl.BlockSpec((len(POOL_WINDOWS), POOL_GROUP_DIM, POOL_GROUP_DIM), lambda i: (0, 0, 0)),
                  pl.BlockSpec((1, D_POOL), lambda i: (0, 0))],
        out_specs=pl.BlockSpec((1, t, D_POOL), lambda i: (i, 0, 0)),
        out_shape=jax.ShapeDtypeStruct((b, t, D_POOL), BF16),
        scratch_shapes=[pltpu.VMEM((t + 2 * POOL_PAD, D_POOL), F32)],
        compiler_params=_params("parallel"),
        name="pool",
    )(z, w_pool, pool_scale.reshape(1, D_POOL))


def _mix_out_body(rnn_ref, pool_ref, x_ref, m_ref, w_ref, g_ref, b_ref, x1_ref, hq_ref):
    m = m_ref[0]
    mix = (jnp.dot(rnn_ref[0], w_ref[0:D_RNN], preferred_element_type=F32)
           + jnp.dot(pool_ref[0], w_ref[D_RNN:D_RNN + D_POOL], preferred_element_type=F32))
    x1 = _layer_norm(DEEPNORM_ALPHA * x_ref[0] + m[2:3] * mix, g_ref[...], b_ref[...])
    x1_ref[0] = x1
    hq_ref[0] = x1 * (1.0 + m[4:5]) + m[3:4]


def _mix_out(rnn, pool, x, mods, w_out, ln_g, ln_b):
    b, t, d = x.shape
    tm = min(t, 512)
    tok = lambda i, j: (i, j, 0)
    return pl.pallas_call(
        _mix_out_body,
        grid=(b, t // tm),
        in_specs=[pl.BlockSpec((1, tm, D_RNN), tok),
                  pl.BlockSpec((1, tm, D_POOL), tok),
                  pl.BlockSpec((1, tm, d), tok),
                  _mod_spec(mods),
                  pl.BlockSpec((D_RNN + D_POOL, d), lambda i, j: (0, 0)),
                  pl.BlockSpec((1, d), lambda i, j: (0, 0)),
                  pl.BlockSpec((1, d), lambda i, j: (0, 0))],
        out_specs=[pl.BlockSpec((1, tm, d), tok), pl.BlockSpec((1, tm, d), tok)],
        out_shape=[jax.ShapeDtypeStruct(x.shape, F32), jax.ShapeDtypeStruct(x.shape, F32)],
        compiler_params=_params("parallel", "parallel"),
        name="mix_out",
    )(rnn, pool, x, mods, w_out, ln_g.reshape(1, d), ln_b.reshape(1, d))


NEG_INF = float("-inf")


def _top_rows(s, ids, k):
    vals, picks = [], []
    big = jnp.int32(2 ** 30)
    for _ in range(k):
        m = jnp.max(s, axis=0, keepdims=True)
        pick = jnp.min(jnp.where(s == m, ids, big), axis=0, keepdims=True)
        s = jnp.where(ids == pick, NEG_INF, s)
        vals.append(m)
        picks.append(pick)
    return jnp.concatenate(vals, axis=0), jnp.concatenate(picks, axis=0)


def _gather_rows(table, ids, picks):
    out = []
    for r in range(picks.shape[0]):
        out.append(jnp.sum(jnp.where(ids == picks[r:r + 1], table, 0), axis=0, keepdims=True))
    return jnp.concatenate(out, axis=0)


def _route_body(hq_ref, wq_ref, keys_ref, idx_ref, gate_ref):
    tm = hq_ref.shape[0]
    q = jnp.dot(hq_ref[...].astype(BF16), wq_ref[...], preferred_element_type=F32)
    key_ids = lax.broadcasted_iota(jnp.int32, (PEER_NKEYS, tm), 0)
    half_rows = SUBLANES
    sub = lax.broadcasted_iota(jnp.int32, (half_rows, tm), 0)
    for h in range(PEER_HEADS):
        tops = []
        for p in range(2):
            c0 = (h * 2 + p) * PEER_HALF
            qs = q[:, c0:c0 + PEER_HALF].astype(BF16)
            s_t = lax.dot_general(keys_ref[h, p], qs, (((1,), (1,)), ((), ())),
                                  preferred_element_type=F32)
            tops.append(_top_rows(s_t, key_ids, PEER_TOPK))
        (s1, i1), (s2, i2) = tops
        cand, flat, expert = [], [], []
        for j in range(half_rows):
            ok = (sub + 1) * (j + 1) <= PEER_TOPK
            cand.append(jnp.where(ok, s1[0:half_rows] + s2[j:j + 1], NEG_INF))
            flat.append(sub * PEER_TOPK + j)
            expert.append(i1[0:half_rows] * PEER_NKEYS + i2[j:j + 1])
        cand.append(s1[0:1] + s2[half_rows:2 * half_rows])
        flat.append(sub + half_rows)
        expert.append(i1[0:1] * PEER_NKEYS + i2[half_rows:2 * half_rows])
        cand.append(s1[half_rows:2 * half_rows] + s2[0:1])
        flat.append((sub + half_rows) * PEER_TOPK)
        expert.append(i1[half_rows:2 * half_rows] * PEER_NKEYS + i2[0:1])
        cand = jnp.concatenate(cand, axis=0)
        flat = jnp.concatenate(flat, axis=0)
        expert = jnp.concatenate(expert, axis=0)
        sc, picks = _top_rows(cand, flat, PEER_TOPK)
        e = jnp.exp(sc - sc[0:1])
        rows = slice(h * PEER_TOPK, (h + 1) * PEER_TOPK)
        gate_ref[rows, :] = e / jnp.sum(e, axis=0, keepdims=True)
        idx_ref[rows, :] = _gather_rows(expert, flat, picks)


def _route(hq, wq, keys):
    n, d = hq.shape
    tm = 256
    return pl.pallas_call(
        _route_body,
        grid=(n // tm,),
        in_specs=[pl.BlockSpec((tm, d), lambda i: (i, 0)),
                  pl.BlockSpec((d, PEER_HEADS * PEER_QDIM), lambda i: (0, 0)),
                  pl.BlockSpec((PEER_HEADS, 2, PEER_NKEYS, PEER_HALF), lambda i: (0, 0, 0, 0))],
        out_specs=[pl.BlockSpec((PEER_PICKS, tm), lambda i: (0, i)),
                   pl.BlockSpec((PEER_PICKS, tm), lambda i: (0, i))],
        out_shape=[jax.ShapeDtypeStruct((PEER_PICKS, n), jnp.int32),
                   jax.ShapeDtypeStruct((PEER_PICKS, n), F32)],
        compiler_params=_params("parallel"),
        name="peer_route",
    )(hq, wq, keys)


def _final_body(x1_ref, f_ref, m_ref, g_ref, b_ref, o_ref):
    m = m_ref[0]
    o_ref[0] = _layer_norm(DEEPNORM_ALPHA * x1_ref[0] + m[5:6] * f_ref[0], g_ref[...], b_ref[...])


def _final(x1, ffn, mods, ln_g, ln_b):
    b, t, d = x1.shape
    tm = min(t, 512)
    tok = lambda i, j: (i, j, 0)
    return pl.pallas_call(
        _final_body,
        grid=(b, t // tm),
        in_specs=[pl.BlockSpec((1, tm, d), tok), pl.BlockSpec((1, tm, d), tok), _mod_spec(mods),
                  pl.BlockSpec((1, d), lambda i, j: (0, 0)), pl.BlockSpec((1, d), lambda i, j: (0, 0))],
        out_specs=pl.BlockSpec((1, tm, d), tok),
        out_shape=jax.ShapeDtypeStruct(x1.shape, F32),
        compiler_params=_params("parallel", "parallel"),
        name="final_ln",
    )(x1, ffn, mods, ln_g.reshape(1, d), ln_b.reshape(1, d))


SC_TOKENS = 8
SC_PICKS = 16
GELU_C = math.sqrt(2.0 / math.pi)


def _gelu_tanh(x):
    y = GELU_C * (x + 0.044715 * (x * x * x))
    return 0.5 * x * (2.0 - 2.0 / (1.0 + jnp.exp(2.0 * y)))


def _experts(hq, idx, gates, u_tab, v_tab):
    n, d = hq.shape
    info = plsc.get_sparse_core_info()
    n_workers = info.num_cores * info.num_subcores
    lanes = info.num_lanes
    assert SC_PICKS == lanes and n % (n_workers * SC_TOKENS) == 0 and d % lanes == 0
    per_w = n // n_workers
    n_blocks = per_w // SC_TOKENS
    steps_per_token = PEER_PICKS // SC_PICKS
    steps = SC_TOKENS * steps_per_token
    n_cols = d // lanes
    mesh = plsc.VectorSubcoreMesh(core_axis_name="core", subcore_axis_name="subcore")
    row_buf = pltpu.VMEM((SC_PICKS, d), F32)

    @functools.partial(
        pl.kernel, out_type=jax.ShapeDtypeStruct((n * d,), F32), mesh=mesh,
        scratch_types=[pltpu.VMEM((SC_TOKENS * d,), F32), pltpu.VMEM((SC_TOKENS * d,), F32),
                       pltpu.VMEM((SC_TOKENS * PEER_PICKS,), jnp.int32),
                       pltpu.VMEM((SC_TOKENS * PEER_PICKS,), F32),
                       row_buf, row_buf, row_buf, row_buf,
                       pltpu.VMEM((lanes, 2 * lanes), F32), pltpu.VMEM((2 * lanes,), F32),
                       pltpu.SemaphoreType.DMA, pltpu.SemaphoreType.DMA,
                       pltpu.SemaphoreType.DMA, pltpu.SemaphoreType.DMA],
        compiler_params=pltpu.CompilerParams(needs_layout_passes=False), name="sc_experts")
    def run(hq_hbm, idx_hbm, gate_hbm, u_hbm, v_hbm, out_hbm,
            x_v, o_v, idx_v, g_v, u0, u1, v0, v1, red, act_s, su0, su1, sv0, sv1):
        wid = lax.axis_index("subcore") * info.num_cores + lax.axis_index("core")
        tok_base = wid * per_w
        ubufs, vbufs, usems, vsems = (u0, u1), (v0, v1), (su0, su1), (sv0, sv1)
        lane = lax.iota(jnp.int32, lanes)

        def copies(s, b):
            rows = idx_v.at[pl.ds(pl.multiple_of(s * SC_PICKS, SC_PICKS), SC_PICKS)]
            return (pltpu.make_async_copy(u_hbm.at[rows], ubufs[b], usems[b]),
                    pltpu.make_async_copy(v_hbm.at[rows], vbufs[b], vsems[b]))

        def issue(s, b):
            for cp in copies(s, b):
                cp.start()

        def wait(s, b):
            for cp in copies(s, b):
                cp.wait()

        def compute(s, b):
            xoff = pl.multiple_of((s // steps_per_token) * d, d)
            gate = g_v[pl.ds(pl.multiple_of(s * SC_PICKS, SC_PICKS), SC_PICKS)]
            ub, vb = ubufs[b], vbufs[b]

            def dot_body(c, accs):
                c0 = pl.multiple_of(c * lanes, lanes)
                xv = x_v[pl.ds(xoff + c0, lanes)]
                return tuple(accs[p] + ub[p, pl.ds(c0, lanes)] * xv for p in range(SC_PICKS))

            accs = lax.fori_loop(0, n_cols, dot_body,
                                 tuple(jnp.zeros((lanes,), F32) for _ in range(SC_PICKS)))
            for p in range(SC_PICKS):
                red[p, pl.ds(lanes, lanes)] = accs[p]
            dots = plsc.load_gather(red, [lane, jnp.full((lanes,), lanes, jnp.int32)])
            for j in range(1, lanes):
                dots = dots + plsc.load_gather(red, [lane, jnp.full((lanes,), lanes + j, jnp.int32)])
            act_s[pl.ds(lanes, lanes)] = _gelu_tanh(dots) * gate
            act = [plsc.load_gather(act_s, [jnp.full((lanes,), lanes + p, jnp.int32)])
                   for p in range(SC_PICKS)]

            def out_body(c, carry):
                c0 = pl.multiple_of(c * lanes, lanes)
                o = act[0] * vb[0, pl.ds(c0, lanes)]
                for p in range(1, SC_PICKS):
                    o = o + act[p] * vb[p, pl.ds(c0, lanes)]
                o_v[pl.ds(xoff + c0, lanes)] = o_v[pl.ds(xoff + c0, lanes)] + o
                return carry

            lax.fori_loop(0, n_cols, out_body, 0)

        @pl.loop(0, n_blocks)
        def _(blk):
            tok0 = tok_base + blk * SC_TOKENS
            x_at = pl.ds(pl.multiple_of(tok0 * d, SC_TOKENS * d), SC_TOKENS * d)
            k_at = pl.ds(pl.multiple_of(tok0 * PEER_PICKS, SC_TOKENS * PEER_PICKS), SC_TOKENS * PEER_PICKS)
            pltpu.sync_copy(hq_hbm.at[x_at], x_v)
            pltpu.sync_copy(idx_hbm.at[k_at], idx_v)
            pltpu.sync_copy(gate_hbm.at[k_at], g_v)

            @pl.loop(0, SC_TOKENS * n_cols)
            def _(i):
                o_v[pl.ds(pl.multiple_of(i * lanes, lanes), lanes)] = jnp.zeros((lanes,), F32)

            issue(0, 0)

            @pl.loop(0, steps // 2)
            def _(h):
                s0 = 2 * h
                issue(s0 + 1, 1)
                wait(s0, 0)
                compute(s0, 0)

                @pl.when(h < steps // 2 - 1)
                def _():
                    issue(s0 + 2, 0)

                wait(s0 + 1, 1)
                compute(s0 + 1, 1)

            pltpu.sync_copy(o_v, out_hbm.at[x_at])

    out = run(hq.reshape(-1), idx.reshape(-1), gates.reshape(-1), u_tab, v_tab)
    return out.reshape(n, d)


def _grid_pos_embed(rows):
    t = jnp.arange(rows * GRID_W)
    r = (t // GRID_W).astype(F32)
    col = (t % GRID_W).astype(F32)
    n_freq = D_MODEL // 4
    omega = 1.0 / (10000.0 ** (jnp.arange(n_freq, dtype=F32) / n_freq))

    def enc(p):
        ang = p[:, None] * omega[None, :]
        return jnp.concatenate([jnp.sin(ang), jnp.cos(ang)], -1)
    return jnp.concatenate([enc(r), enc(col)], -1)


def _layer(x, mods, h0, p):
    b, t, d = x.shape
    z = _mix_in(x, mods, p["w_in"])
    rnn, st = _rnn(z, h0, p["conv_w"], p["conv_b"], p["wa"], p["wx"], p["b_a"], p["b_x"], p["lam"])
    pool = _pool(z, p["w_pool"], p["pool_scale"])
    x1, hq = _mix_out(rnn, pool, x, mods, p["w_out"], p["ln1_g"], p["ln1_b"])
    hq = hq.reshape(b * t, d)
    idx_t, gate_t = _route(hq, p["wq"], p["keys"])
    ffn = _experts(hq, idx_t.T, gate_t.T, p["peer_u"], p["peer_v"])
    x2 = _final(x1, ffn.reshape(b, t, d), mods, p["ln2_g"], p["ln2_b"])
    return x2, st


def kernel(x_prompt, x_sample, state_rglru, c, c_ctx, w_mod, b_mod, w_in, conv_w, conv_b,
           w_rg_a, b_rg_a, w_rg_x, b_rg_x, lru_lambda, w_pool, pool_scale, w_out,
           ln1_g, ln1_b, ln2_g, ln2_b, peer_wq, peer_keys, peer_u, peer_v):
    n_req = c.shape[0]
    cvecs = jnp.zeros((MOD_ROWS, D_MODEL), F32).at[:n_req].set(c).at[n_req].set(c_ctx)
    mod = _modulation(cvecs, w_mod, b_mod).reshape(DEPTH, MOD_ROWS, N_MOD, D_MODEL)
    mod = jnp.pad(mod, ((0, 0), (0, 0), (0, SUBLANES - N_MOD), (0, 0)))

    rows = x_sample.shape[1] // GRID_W
    xs = _add_pos(x_sample, _grid_pos_embed(rows))
    xp = x_prompt
    zero_state = jnp.zeros((x_prompt.shape[0], 2, D_RNN), F32)
    ctx_states = []
    for l in range(DEPTH):
        p = dict(w_in=w_in[l].astype(BF16), conv_w=conv_w[l], conv_b=conv_b[l],
                 wa=_block_diag_pairs(w_rg_a[l]), wx=_block_diag_pairs(w_rg_x[l]),
                 b_a=b_rg_a[l], b_x=b_rg_x[l], lam=lru_lambda[l],
                 w_pool=w_pool[l].astype(BF16), pool_scale=pool_scale[l], w_out=w_out[l].astype(BF16),
                 ln1_g=ln1_g[l], ln1_b=ln1_b[l], ln2_g=ln2_g[l], ln2_b=ln2_b[l],
                 wq=peer_wq[l].astype(BF16), keys=peer_keys[l].astype(BF16),
                 peer_u=peer_u[l], peer_v=peer_v[l])
        xp, st = _layer(xp, mod[l, n_req:n_req + 1], zero_state, p)
        ctx_states.append(st)
        xs, _ = _layer(xs, mod[l, :n_req], state_rglru[:, l], p)
    return xp, xs, jnp.stack(ctx_states, axis=1)
```

```python
import functools
import math

import jax
import jax.numpy as jnp
from jax import lax
from jax.experimental import pallas as pl
from jax.experimental.pallas import tpu as pltpu
from jax.experimental.pallas import tpu_sc as plsc

F32 = jnp.float32
BF16 = jnp.bfloat16

D_MODEL = 1024
DEPTH = 2
GRID_W = 64
D_RNN = 512
N_RNN_HEADS = 8
RNN_HEAD_DIM = D_RNN // N_RNN_HEADS
CONV_W = 4
LRU_C = 8.0
D_POOL = 512
POOL_WINDOWS = (2, 4, 8, 16)
POOL_GROUP_DIM = D_POOL // len(POOL_WINDOWS)
D_IN = 2 * D_RNN + D_POOL
PEER_HEADS = 8
PEER_NKEYS = 128
PEER_TOPK = 16
PEER_QDIM = 256
PEER_HALF = PEER_QDIM // 2
PEER_PICKS = PEER_HEADS * PEER_TOPK
N_MOD = 6
DEEPNORM_ALPHA = (2 * DEPTH) ** 0.25
LN_EPS = 1e-5

LANES = 128
SUBLANES = 8
VMEM_LIMIT_BYTES = 56 * 1024 * 1024
MOD_ROWS = 16
ROW_CHUNK = 256


def _params(*sem):
    return pltpu.CompilerParams(dimension_semantics=sem, vmem_limit_bytes=VMEM_LIMIT_BYTES)


def _layer_norm(y, g, b):
    mu = jnp.mean(y, -1, keepdims=True)
    var = jnp.mean(jnp.square(y - mu), -1, keepdims=True)
    return (y - mu) * lax.rsqrt(var + LN_EPS) * g + b


def _mod_body(c_ref, w_ref, b_ref, o_ref):
    c = c_ref[...]
    s = (c * jax.nn.sigmoid(c)).astype(BF16)
    o_ref[0] = jnp.dot(s, w_ref[0].astype(BF16), preferred_element_type=F32) + b_ref[0]


def _modulation(cvecs, w_mod, b_mod):
    n_out = N_MOD * D_MODEL
    tn = 1536
    return pl.pallas_call(
        _mod_body,
        grid=(DEPTH, n_out // tn),
        in_specs=[pl.BlockSpec((MOD_ROWS, D_MODEL), lambda l, j: (0, 0)),
                  pl.BlockSpec((1, D_MODEL, tn), lambda l, j: (l, 0, j)),
                  pl.BlockSpec((1, 1, tn), lambda l, j: (l, 0, j))],
        out_specs=pl.BlockSpec((1, MOD_ROWS, tn), lambda l, j: (l, 0, j)),
        out_shape=jax.ShapeDtypeStruct((DEPTH, MOD_ROWS, n_out), F32),
        compiler_params=_params("parallel", "parallel"),
        name="modulation",
    )(cvecs, w_mod, b_mod.reshape(DEPTH, 1, n_out))


def _add_pos_body(x_ref, p_ref, o_ref):
    o_ref[0] = x_ref[0] + p_ref[...]


def _add_pos(x, pos):
    b, t, d = x.shape
    tm = min(t, 512)
    return pl.pallas_call(
        _add_pos_body,
        grid=(b, t // tm),
        in_specs=[pl.BlockSpec((1, tm, d), lambda i, j: (i, j, 0)),
                  pl.BlockSpec((tm, d), lambda i, j: (j, 0))],
        out_specs=pl.BlockSpec((1, tm, d), lambda i, j: (i, j, 0)),
        out_shape=jax.ShapeDtypeStruct(x.shape, x.dtype),
        compiler_params=_params("parallel", "parallel"),
        name="add_pos",
    )(x, pos)


def _mix_in_body(x_ref, m_ref, w_ref, z_ref):
    m = m_ref[0]
    h = x_ref[0] * (1.0 + m[1:2]) + m[0:1]
    z_ref[0] = jnp.dot(h.astype(BF16), w_ref[...], preferred_element_type=F32)


def _mod_spec(mods):
    if mods.shape[0] == 1:
        return pl.BlockSpec((1, SUBLANES, D_MODEL), lambda i, j: (0, 0, 0))
    return pl.BlockSpec((1, SUBLANES, D_MODEL), lambda i, j: (i, 0, 0))


def _mix_in(x, mods, w_in):
    b, t, d = x.shape
    tm = min(t, 512)
    return pl.pallas_call(
        _mix_in_body,
        grid=(b, t // tm),
        in_specs=[pl.BlockSpec((1, tm, d), lambda i, j: (i, j, 0)),
                  _mod_spec(mods),
                  pl.BlockSpec((d, D_IN), lambda i, j: (0, 0))],
        out_specs=pl.BlockSpec((1, tm, D_IN), lambda i, j: (i, j, 0)),
        out_shape=jax.ShapeDtypeStruct((b, t, D_IN), F32),
        compiler_params=_params("parallel", "parallel"),
        name="mix_in",
    )(x, mods, w_in)


PAD = SUBLANES
UNROLL_GROUPS = 8


def _rnn_body(xr_ref, gr_ref, h0_ref, cw_ref, cb_ref, wa_ref, wx_ref, ba_ref, bx_ref, lam_ref,
              out_ref, st_ref, xpad, a_s, u_s):
    t_len = xr_ref.shape[1]
    rc = min(ROW_CHUNK, t_len)
    zeros = jnp.zeros((PAD, LANES), F32)
    xpad[0:PAD] = zeros
    xpad[t_len + PAD:t_len + 2 * PAD] = zeros
    xpad[PAD:t_len + PAD] = xr_ref[0]

    cw = cw_ref[...]
    lam = lam_ref[...]
    softplus_neg = jnp.maximum(-lam, 0.0) + jnp.log1p(jnp.exp(-jnp.abs(lam)))
    coef = -LRU_C * softplus_neg
    row = lax.broadcasted_iota(jnp.int32, (rc, LANES), 0) % SUBLANES
    left = CONV_W // 2

    for c in range(t_len // rc):
        t0 = c * rc
        xc = cb_ref[...] + xpad[t0 + PAD - left:t0 + PAD - left + rc] * cw[0:1]
        for k in range(1, CONV_W):
            xc = xc + xpad[t0 + PAD - left + k:t0 + PAD - left + k + rc] * cw[k:k + 1]
        xcb = xc.astype(BF16)
        for d in range(2):
            r = jax.nn.sigmoid(jnp.dot(xcb, wa_ref[d, 0], preferred_element_type=F32) + ba_ref[d:d + 1])
            gi = jax.nn.sigmoid(jnp.dot(xcb, wx_ref[d, 0], preferred_element_type=F32) + bx_ref[d:d + 1])
            log_a = coef[d:d + 1] * r
            a = jnp.exp(log_a)
            u = jnp.sqrt(-jnp.tanh(log_a) * (a * a + 1.0)) * (gi * xc)
            for s in (1, 2, 4):
                if d == 0:
                    keep = row >= s
                    shift = s
                else:
                    keep = row <= SUBLANES - 1 - s
                    shift = rc - s
                a_sh = jnp.where(keep, pltpu.roll(a, shift, 0), 1.0)
                u_sh = jnp.where(keep, pltpu.roll(u, shift, 0), 0.0)
                u = a * u_sh + u
                a = a * a_sh
            a_s[d, t0:t0 + rc] = a
            u_s[d, t0:t0 + rc] = u

    n_groups = t_len // SUBLANES
    n_iter = n_groups // UNROLL_GROUPS

    def carry_step(i, carry):
        hf, hb = carry
        for j in range(UNROLL_GROUPS):
            gf = pl.multiple_of((i * UNROLL_GROUPS + j) * SUBLANES, SUBLANES)
            gb = pl.multiple_of((n_groups - 1 - i * UNROLL_GROUPS - j) * SUBLANES, SUBLANES)
            h_f = u_s[0, pl.ds(gf, SUBLANES)] + a_s[0, pl.ds(gf, SUBLANES)] * hf
            h_b = u_s[1, pl.ds(gb, SUBLANES)] + a_s[1, pl.ds(gb, SUBLANES)] * hb
            u_s[0, pl.ds(gf, SUBLANES)] = h_f
            u_s[1, pl.ds(gb, SUBLANES)] = h_b
            hf = h_f[SUBLANES - 1:SUBLANES]
            hb = h_b[0:1]
        return hf, hb

    h0 = h0_ref[0]
    hf, hb = lax.fori_loop(0, n_iter, carry_step, (h0[0:1], h0[1:2]))
    st_ref[0, 0:1] = hf
    st_ref[0, 1:2] = hb

    for c in range(t_len // rc):
        t0 = c * rc
        hsum = u_s[0, t0:t0 + rc] + u_s[1, t0:t0 + rc]
        out_ref[0, t0:t0 + rc] = (hsum * jax.nn.gelu(gr_ref[0, t0:t0 + rc])).astype(BF16)


def _rnn(z, h0, conv_w, conv_b, wa, wx, b_a, b_x, lam):
    b, t, _ = z.shape
    nblk = D_RNN // LANES
    return pl.pallas_call(
        _rnn_body,
        grid=(b, nblk),
        in_specs=[pl.BlockSpec((1, t, LANES), lambda i, j: (i, 0, j)),
                  pl.BlockSpec((1, t, LANES), lambda i, j: (i, 0, nblk + j)),
                  pl.BlockSpec((1, 2, LANES), lambda i, j: (i, 0, j)),
                  pl.BlockSpec((CONV_W, LANES), lambda i, j: (0, j)),
                  pl.BlockSpec((1, LANES), lambda i, j: (0, j)),
                  pl.BlockSpec((2, 1, LANES, LANES), lambda i, j: (0, j, 0, 0)),
                  pl.BlockSpec((2, 1, LANES, LANES), lambda i, j: (0, j, 0, 0)),
                  pl.BlockSpec((2, LANES), lambda i, j: (0, j)),
                  pl.BlockSpec((2, LANES), lambda i, j: (0, j)),
                  pl.BlockSpec((2, LANES), lambda i, j: (0, j))],
        out_specs=[pl.BlockSpec((1, t, LANES), lambda i, j: (i, 0, j)),
                   pl.BlockSpec((1, 2, LANES), lambda i, j: (i, 0, j))],
        out_shape=[jax.ShapeDtypeStruct((b, t, D_RNN), BF16),
                   jax.ShapeDtypeStruct((b, 2, D_RNN), F32)],
        scratch_shapes=[pltpu.VMEM((t + 2 * PAD, LANES), F32),
                        pltpu.VMEM((2, t, LANES), F32),
                        pltpu.VMEM((2, t, LANES), F32)],
        compiler_params=_params("parallel", "parallel"),
        name="rglru",
    )(z, z, h0, conv_w, conv_b.reshape(1, D_RNN), wa, wx, b_a, b_x, lam)


def _block_diag_pairs(w):
    per = LANES // RNN_HEAD_DIM
    w = w.reshape(2, D_RNN // LANES, per, RNN_HEAD_DIM, RNN_HEAD_DIM)
    out = jnp.zeros((2, D_RNN // LANES, LANES, LANES), w.dtype)
    for p in range(per):
        sl = slice(p * RNN_HEAD_DIM, (p + 1) * RNN_HEAD_DIM)
        out = out.at[:, :, sl, sl].set(w[:, :, p])
    return out.astype(BF16)


POOL_PAD = 16


def _pool_body(xq_ref, w_ref, sc_ref, out_ref, ppad):
    t_len = xq_ref.shape[1]
    rc = min(ROW_CHUNK, t_len)
    zeros = jnp.zeros((POOL_PAD, D_POOL), F32)
    ppad[0:POOL_PAD] = zeros
    ppad[t_len + POOL_PAD:t_len + 2 * POOL_PAD] = zeros
    ppad[POOL_PAD:t_len + POOL_PAD] = xq_ref[0]
    for c in range(t_len // rc):
        t0 = c * rc
        tpos = t0 + lax.broadcasted_iota(jnp.int32, (rc, POOL_GROUP_DIM), 0)
        for g, w in enumerate(POOL_WINDOWS):
            cols = slice(g * POOL_GROUP_DIM, (g + 1) * POOL_GROUP_DIM)
            half = w // 2
            base = t0 + POOL_PAD - half
            s = ppad[base:base + rc, cols]
            for k in range(1, w):
                s = s + ppad[base + k:base + k + rc, cols]
            cnt = (jnp.minimum(tpos + half, t_len) - jnp.maximum(tpos - half, 0)).astype(F32)
            pooled = s / cnt - ppad[t0 + POOL_PAD:t0 + POOL_PAD + rc, cols]
            y = jnp.dot(pooled.astype(BF16), w_ref[g], preferred_element_type=F32)
            out_ref[0, t0:t0 + rc, cols] = (y * sc_ref[:, cols]).astype(BF16)


def _pool(z, w_pool, pool_scale):
    b, t, _ = z.shape
    return pl.pallas_call(
        _pool_body,
        grid=(b,),
        in_specs=[pl.BlockSpec((1, t, D_POOL), lambda i: (i, 0, 2 * D_RNN // D_POOL)),
                  pl.BlockSpec((len(POOL_WINDOWS), POOL_GROUP_DIM, POOL_GROUP_DIM), lambda i: (0, 0, 0)),
                  pl.BlockSpec((1, D_POOL), lambda i: (0, 0))],
        out_specs=pl.BlockSpec((1, t, D_POOL), lambda i: (i, 0, 0)),
        out_shape=jax.ShapeDtypeStruct((b, t, D_POOL), BF16),
        scratch_shapes=[pltpu.VMEM((t + 2 * POOL_PAD, D_POOL), F32)],
        compiler_params=_params("parallel"),
        name="pool",
    )(z, w_pool, pool_scale.reshape(1, D_POOL))


def _mix_out_body(rnn_ref, pool_ref, x_ref, m_ref, w_ref, g_ref, b_ref, x1_ref, hq_ref):
    m = m_ref[0]
    mix = (jnp.dot(rnn_ref[0], w_ref[0:D_RNN], preferred_element_type=F32)
           + jnp.dot(pool_ref[0], w_ref[D_RNN:D_RNN + D_POOL], preferred_element_type=F32))
    x1 = _layer_norm(DEEPNORM_ALPHA * x_ref[0] + m[2:3] * mix, g_ref[...], b_ref[...])
    x1_ref[0] = x1
    hq_ref[0] = x1 * (1.0 + m[4:5]) + m[3:4]


def _mix_out(rnn, pool, x, mods, w_out, ln_g, ln_b):
    b, t, d = x.shape
    tm = min(t, 512)
    tok = lambda i, j: (i, j, 0)
    return pl.pallas_call(
        _mix_out_body,
        grid=(b, t // tm),
        in_specs=[pl.BlockSpec((1, tm, D_RNN), tok),
                  pl.BlockSpec((1, tm, D_POOL), tok),
                  pl.BlockSpec((1, tm, d), tok),
                  _mod_spec(mods),
                  pl.BlockSpec((D_RNN + D_POOL, d), lambda i, j: (0, 0)),
                  pl.BlockSpec((1, d), lambda i, j: (0, 0)),
                  pl.BlockSpec((1, d), lambda i, j: (0, 0))],
        out_specs=[pl.BlockSpec((1, tm, d), tok), pl.BlockSpec((1, tm, d), tok)],
        out_shape=[jax.ShapeDtypeStruct(x.shape, F32), jax.ShapeDtypeStruct(x.shape, F32)],
        compiler_params=_params("parallel", "parallel"),
        name="mix_out",
    )(rnn, pool, x, mods, w_out, ln_g.reshape(1, d), ln_b.reshape(1, d))


NEG_INF = float("-inf")


def _top_rows(s, ids, k):
    vals, picks = [], []
    big = jnp.int32(2 ** 30)
    for _ in range(k):
        m = jnp.max(s, axis=0, keepdims=True)
        pick = jnp.min(jnp.where(s == m, ids, big), axis=0, keepdims=True)
        s = jnp.where(ids == pick, NEG_INF, s)
        vals.append(m)
        picks.append(pick)
    return jnp.concatenate(vals, axis=0), jnp.concatenate(picks, axis=0)


def _gather_rows(table, ids, picks):
    out = []
    for r in range(picks.shape[0]):
        out.append(jnp.sum(jnp.where(ids == picks[r:r + 1], table, 0), axis=0, keepdims=True))
    return jnp.concatenate(out, axis=0)


def _route_body(hq_ref, wq_ref, keys_ref, idx_ref, gate_ref):
    tm = hq_ref.shape[0]
    q = jnp.dot(hq_ref[...].astype(BF16), wq_ref[...], preferred_element_type=F32)
    key_ids = lax.broadcasted_iota(jnp.int32, (PEER_NKEYS, tm), 0)
    half_rows = SUBLANES
    sub = lax.broadcasted_iota(jnp.int32, (half_rows, tm), 0)
    for h in range(PEER_HEADS):
        tops = []
        for p in range(2):
            c0 = (h * 2 + p) * PEER_HALF
            qs = q[:, c0:c0 + PEER_HALF].astype(BF16)
            s_t = lax.dot_general(keys_ref[h, p], qs, (((1,), (1,)), ((), ())),
                                  preferred_element_type=F32)
            tops.append(_top_rows(s_t, key_ids, PEER_TOPK))
        (s1, i1), (s2, i2) = tops
        cand, flat, expert = [], [], []
        for j in range(half_rows):
            ok = (sub + 1) * (j + 1) <= PEER_TOPK
            cand.append(jnp.where(ok, s1[0:half_rows] + s2[j:j + 1], NEG_INF))
            flat.append(sub * PEER_TOPK + j)
            expert.append(i1[0:half_rows] * PEER_NKEYS + i2[j:j + 1])
        cand.append(s1[0:1] + s2[half_rows:2 * half_rows])
        flat.append(sub + half_rows)
        expert.append(i1[0:1] * PEER_NKEYS + i2[half_rows:2 * half_rows])
        cand.append(s1[half_rows:2 * half_rows] + s2[0:1])
        flat.append((sub + half_rows) * PEER_TOPK)
        expert.append(i1[half_rows:2 * half_rows] * PEER_NKEYS + i2[0:1])
        cand = jnp.concatenate(cand, axis=0)
        flat = jnp.concatenate(flat, axis=0)
        expert = jnp.concatenate(expert, axis=0)
        sc, picks = _top_rows(cand, flat, PEER_TOPK)
        e = jnp.exp(sc - sc[0:1])
        rows = slice(h * PEER_TOPK, (h + 1) * PEER_TOPK)
        gate_ref[rows, :] = e / jnp.sum(e, axis=0, keepdims=True)
        idx_ref[rows, :] = _gather_rows(expert, flat, picks)


def _route(hq, wq, keys):
    n, d = hq.shape
    tm = 256
    return pl.pallas_call(
        _route_body,
        grid=(n // tm,),
        in_specs=[pl.BlockSpec((tm, d), lambda i: (i, 0)),
                  pl.BlockSpec((d, PEER_HEADS * PEER_QDIM), lambda i: (0, 0)),
                  pl.BlockSpec((PEER_HEADS, 2, PEER_NKEYS, PEER_HALF), lambda i: (0, 0, 0, 0))],
        out_specs=[pl.BlockSpec((PEER_PICKS, tm), lambda i: (0, i)),
                   pl.BlockSpec((PEER_PICKS, tm), lambda i: (0, i))],
        out_shape=[jax.ShapeDtypeStruct((PEER_PICKS, n), jnp.int32),
                   jax.ShapeDtypeStruct((PEER_PICKS, n), F32)],
        compiler_params=_params("parallel"),
        name="peer_route",
    )(hq, wq, keys)


def _final_body(x1_ref, f_ref, m_ref, g_ref, b_ref, o_ref):
    m = m_ref[0]
    o_ref[0] = _layer_norm(DEEPNORM_ALPHA * x1_ref[0] + m[5:6] * f_ref[0], g_ref[...], b_ref[...])


def _final(x1, ffn, mods, ln_g, ln_b):
    b, t, d = x1.shape
    tm = min(t, 512)
    tok = lambda i, j: (i, j, 0)
    return pl.pallas_call(
        _final_body,
        grid=(b, t // tm),
        in_specs=[pl.BlockSpec((1, tm, d), tok), pl.BlockSpec((1, tm, d), tok), _mod_spec(mods),
                  pl.BlockSpec((1, d), lambda i, j: (0, 0)), pl.BlockSpec((1, d), lambda i, j: (0, 0))],
        out_specs=pl.BlockSpec((1, tm, d), tok),
        out_shape=jax.ShapeDtypeStruct(x1.shape, F32),
        compiler_params=_params("parallel", "parallel"),
        name="final_ln",
    )(x1, ffn, mods, ln_g.reshape(1, d), ln_b.reshape(1, d))


SC_TOKENS = 8
SC_PICKS = 16
GELU_C = math.sqrt(2.0 / math.pi)


def _gelu_tanh(x):
    y = GELU_C * (x + 0.044715 * (x * x * x))
    return 0.5 * x * (2.0 - 2.0 / (1.0 + jnp.exp(2.0 * y)))


def _pack_table(tab, lanes=16):
    e, d = tab.shape
    t = tab.astype(BF16).reshape(e, d // (2 * lanes), 2, lanes).transpose(0, 1, 3, 2)
    return lax.bitcast_convert_type(t, jnp.int32).reshape(e, d // 2)


def _experts(hq, idx, gates, u_tab, v_tab):
    n, d = hq.shape
    info = plsc.get_sparse_core_info()
    n_workers = info.num_cores * info.num_subcores
    lanes = info.num_lanes
    assert SC_PICKS == lanes and n % (n_workers * SC_TOKENS) == 0 and d % lanes == 0
    per_w = n // n_workers
    n_blocks = per_w // SC_TOKENS
    steps_per_token = PEER_PICKS // SC_PICKS
    steps = SC_TOKENS * steps_per_token
    n_cols = d // lanes
    n_words = u_tab.shape[1] // lanes
    assert u_tab.shape[1] * 2 == d and u_tab.dtype == jnp.int32
    mesh = plsc.VectorSubcoreMesh(core_axis_name="core", subcore_axis_name="subcore")
    row_buf = pltpu.VMEM((SC_PICKS, u_tab.shape[1]), jnp.int32)

    @functools.partial(
        pl.kernel, out_type=jax.ShapeDtypeStruct((n * d,), F32), mesh=mesh,
        scratch_types=[pltpu.VMEM((SC_TOKENS * d,), F32), pltpu.VMEM((SC_TOKENS * d,), F32),
                       pltpu.VMEM((SC_TOKENS * PEER_PICKS,), jnp.int32),
                       pltpu.VMEM((SC_TOKENS * PEER_PICKS,), F32),
                       row_buf, row_buf, row_buf, row_buf,
                       pltpu.VMEM((lanes, 2 * lanes), F32), pltpu.VMEM((2 * lanes,), F32),
                       pltpu.SemaphoreType.DMA, pltpu.SemaphoreType.DMA,
                       pltpu.SemaphoreType.DMA, pltpu.SemaphoreType.DMA],
        compiler_params=pltpu.CompilerParams(needs_layout_passes=False), name="sc_experts")
    def run(hq_hbm, idx_hbm, gate_hbm, u_hbm, v_hbm, out_hbm,
            x_v, o_v, idx_v, g_v, u0, u1, v0, v1, red, act_s, su0, su1, sv0, sv1):
        wid = lax.axis_index("subcore") * info.num_cores + lax.axis_index("core")
        tok_base = wid * per_w
        ubufs, vbufs, usems, vsems = (u0, u1), (v0, v1), (su0, su1), (sv0, sv1)
        lane = lax.iota(jnp.int32, lanes)

        def copies(s, b):
            rows = idx_v.at[pl.ds(pl.multiple_of(s * SC_PICKS, SC_PICKS), SC_PICKS)]
            return (pltpu.make_async_copy(u_hbm.at[rows], ubufs[b], usems[b]),
                    pltpu.make_async_copy(v_hbm.at[rows], vbufs[b], vsems[b]))

        def issue(s, b):
            for cp in copies(s, b):
                cp.start()

        def wait(s, b):
            for cp in copies(s, b):
                cp.wait()

        def compute(s, b):
            xoff = pl.multiple_of((s // steps_per_token) * d, d)
            gate = g_v[pl.ds(pl.multiple_of(s * SC_PICKS, SC_PICKS), SC_PICKS)]
            ub, vb = ubufs[b], vbufs[b]

            def halves(w):
                return (lax.bitcast_convert_type(w << 16, F32),
                        lax.bitcast_convert_type(w & jnp.int32(-65536), F32))

            def dot_body(g, accs):
                w0 = pl.multiple_of(g * lanes, lanes)
                xa = x_v[pl.ds(xoff + 2 * w0, lanes)]
                xb = x_v[pl.ds(xoff + 2 * w0 + lanes, lanes)]
                out = []
                for p in range(SC_PICKS):
                    lo, hi = halves(ub[p, pl.ds(w0, lanes)])
                    out.append(accs[p] + (lo * xa + hi * xb))
                return tuple(out)

            accs = lax.fori_loop(0, n_words, dot_body,
                                 tuple(jnp.zeros((lanes,), F32) for _ in range(SC_PICKS)))
            for p in range(SC_PICKS):
                red[p, pl.ds(lanes, lanes)] = accs[p]
            dots = plsc.load_gather(red, [lane, jnp.full((lanes,), lanes, jnp.int32)])
            for j in range(1, lanes):
                dots = dots + plsc.load_gather(red, [lane, jnp.full((lanes,), lanes + j, jnp.int32)])
            act_s[pl.ds(lanes, lanes)] = _gelu_tanh(dots) * gate
            act = [plsc.load_gather(act_s, [jnp.full((lanes,), lanes + p, jnp.int32)])
                   for p in range(SC_PICKS)]

            @plsc.parallel_loop(0, n_words)
            def _(g):
                w0 = pl.multiple_of(g * lanes, lanes)
                ta, tb = [], []
                for p in range(SC_PICKS):
                    lo, hi = halves(vb[p, pl.ds(w0, lanes)])
                    ta.append(act[p] * lo)
                    tb.append(act[p] * hi)
                for terms, off in ((ta, 0), (tb, lanes)):
                    while len(terms) > 1:
                        terms = [terms[i] + terms[i + 1] for i in range(0, len(terms), 2)]
                    at = pl.ds(xoff + 2 * w0 + off, lanes)
                    o_v[at] = o_v[at] + terms[0]

        @pl.loop(0, n_blocks)
        def _(blk):
            tok0 = tok_base + blk * SC_TOKENS
            x_at = pl.ds(pl.multiple_of(tok0 * d, SC_TOKENS * d), SC_TOKENS * d)
            k_at = pl.ds(pl.multiple_of(tok0 * PEER_PICKS, SC_TOKENS * PEER_PICKS), SC_TOKENS * PEER_PICKS)
            pltpu.sync_copy(hq_hbm.at[x_at], x_v)
            pltpu.sync_copy(idx_hbm.at[k_at], idx_v)
            pltpu.sync_copy(gate_hbm.at[k_at], g_v)

            @pl.loop(0, SC_TOKENS * n_cols)
            def _(i):
                o_v[pl.ds(pl.multiple_of(i * lanes, lanes), lanes)] = jnp.zeros((lanes,), F32)

            issue(0, 0)

            @pl.loop(0, steps // 2)
            def _(h):
                s0 = 2 * h
                issue(s0 + 1, 1)
                wait(s0, 0)
                compute(s0, 0)

                @pl.when(h < steps // 2 - 1)
                def _():
                    issue(s0 + 2, 0)

                wait(s0 + 1, 1)
                compute(s0 + 1, 1)

            pltpu.sync_copy(o_v, out_hbm.at[x_at])

    out = run(hq.reshape(-1), idx.reshape(-1), gates.reshape(-1), u_tab, v_tab)
    return out.reshape(n, d)


def _grid_pos_embed(rows):
    t = jnp.arange(rows * GRID_W)
    r = (t // GRID_W).astype(F32)
    col = (t % GRID_W).astype(F32)
    n_freq = D_MODEL // 4
    omega = 1.0 / (10000.0 ** (jnp.arange(n_freq, dtype=F32) / n_freq))

    def enc(p):
        ang = p[:, None] * omega[None, :]
        return jnp.concatenate([jnp.sin(ang), jnp.cos(ang)], -1)
    return jnp.concatenate([enc(r), enc(col)], -1)


def _layer(x, mods, h0, p):
    b, t, d = x.shape
    z = _mix_in(x, mods, p["w_in"])
    rnn, st = _rnn(z, h0, p["conv_w"], p["conv_b"], p["wa"], p["wx"], p["b_a"], p["b_x"], p["lam"])
    pool = _pool(z, p["w_pool"], p["pool_scale"])
    x1, hq = _mix_out(rnn, pool, x, mods, p["w_out"], p["ln1_g"], p["ln1_b"])
    hq = hq.reshape(b * t, d)
    idx_t, gate_t = _route(hq, p["wq"], p["keys"])
    ffn = _experts(hq, idx_t.T, gate_t.T, p["peer_u"], p["peer_v"])
    x2 = _final(x1, ffn.reshape(b, t, d), mods, p["ln2_g"], p["ln2_b"])
    return x2, st


def kernel(x_prompt, x_sample, state_rglru, c, c_ctx, w_mod, b_mod, w_in, conv_w, conv_b,
           w_rg_a, b_rg_a, w_rg_x, b_rg_x, lru_lambda, w_pool, pool_scale, w_out,
           ln1_g, ln1_b, ln2_g, ln2_b, peer_wq, peer_keys, peer_u, peer_v):
    n_req = c.shape[0]
    cvecs = jnp.zeros((MOD_ROWS, D_MODEL), F32).at[:n_req].set(c).at[n_req].set(c_ctx)
    mod = _modulation(cvecs, w_mod, b_mod).reshape(DEPTH, MOD_ROWS, N_MOD, D_MODEL)
    mod = jnp.pad(mod, ((0, 0), (0, 0), (0, SUBLANES - N_MOD), (0, 0)))

    rows = x_sample.shape[1] // GRID_W
    xs = _add_pos(x_sample, _grid_pos_embed(rows))
    xp = x_prompt
    zero_state = jnp.zeros((x_prompt.shape[0], 2, D_RNN), F32)
    ctx_states = []
    for l in range(DEPTH):
        p = dict(w_in=w_in[l].astype(BF16), conv_w=conv_w[l], conv_b=conv_b[l],
                 wa=_block_diag_pairs(w_rg_a[l]), wx=_block_diag_pairs(w_rg_x[l]),
                 b_a=b_rg_a[l], b_x=b_rg_x[l], lam=lru_lambda[l],
                 w_pool=w_pool[l].astype(BF16), pool_scale=pool_scale[l], w_out=w_out[l].astype(BF16),
                 ln1_g=ln1_g[l], ln1_b=ln1_b[l], ln2_g=ln2_g[l], ln2_b=ln2_b[l],
                 wq=peer_wq[l].astype(BF16), keys=peer_keys[l].astype(BF16),
                 peer_u=_pack_table(peer_u[l]), peer_v=_pack_table(peer_v[l]))
        xp, st = _layer(xp, mod[l, n_req:n_req + 1], zero_state, p)
        ctx_states.append(st)
        xs, _ = _layer(xs, mod[l, :n_req], state_rglru[:, l], p)
    return xp, xs, jnp.stack(ctx_states, axis=1)
```

```python
import functools
import math

import jax
import jax.numpy as jnp
from jax import lax
from jax.experimental import pallas as pl
from jax.experimental.pallas import tpu as pltpu
from jax.experimental.pallas import tpu_sc as plsc

F32 = jnp.float32
BF16 = jnp.bfloat16

D_MODEL = 1024
DEPTH = 2
GRID_W = 64
D_RNN = 512
N_RNN_HEADS = 8
RNN_HEAD_DIM = D_RNN // N_RNN_HEADS
CONV_W = 4
LRU_C = 8.0
D_POOL = 512
POOL_WINDOWS = (2, 4, 8, 16)
POOL_GROUP_DIM = D_POOL // len(POOL_WINDOWS)
D_IN = 2 * D_RNN + D_POOL
PEER_HEADS = 8
PEER_NKEYS = 128
PEER_TOPK = 16
PEER_QDIM = 256
PEER_HALF = PEER_QDIM // 2
PEER_PICKS = PEER_HEADS * PEER_TOPK
N_MOD = 6
DEEPNORM_ALPHA = (2 * DEPTH) ** 0.25
LN_EPS = 1e-5

LANES = 128
SUBLANES = 8
VMEM_LIMIT_BYTES = 56 * 1024 * 1024
MOD_ROWS = 16
ROW_CHUNK = 256


def _params(*sem):
    return pltpu.CompilerParams(dimension_semantics=sem, vmem_limit_bytes=VMEM_LIMIT_BYTES)


def _layer_norm(y, g, b):
    mu = jnp.mean(y, -1, keepdims=True)
    var = jnp.mean(jnp.square(y - mu), -1, keepdims=True)
    return (y - mu) * lax.rsqrt(var + LN_EPS) * g + b


def _mod_body(c_ref, w_ref, b_ref, o_ref):
    c = c_ref[...]
    s = (c * jax.nn.sigmoid(c)).astype(BF16)
    o_ref[0] = jnp.dot(s, w_ref[0].astype(BF16), preferred_element_type=F32) + b_ref[0]


def _modulation(cvecs, w_mod, b_mod):
    n_out = N_MOD * D_MODEL
    tn = 1536
    return pl.pallas_call(
        _mod_body,
        grid=(DEPTH, n_out // tn),
        in_specs=[pl.BlockSpec((MOD_ROWS, D_MODEL), lambda l, j: (0, 0)),
                  pl.BlockSpec((1, D_MODEL, tn), lambda l, j: (l, 0, j)),
                  pl.BlockSpec((1, 1, tn), lambda l, j: (l, 0, j))],
        out_specs=pl.BlockSpec((1, MOD_ROWS, tn), lambda l, j: (l, 0, j)),
        out_shape=jax.ShapeDtypeStruct((DEPTH, MOD_ROWS, n_out), F32),
        compiler_params=_params("parallel", "parallel"),
        name="modulation",
    )(cvecs, w_mod, b_mod.reshape(DEPTH, 1, n_out))


def _add_pos_body(x_ref, p_ref, o_ref):
    o_ref[0] = x_ref[0] + p_ref[...]


def _add_pos(x, pos):
    b, t, d = x.shape
    tm = min(t, 512)
    return pl.pallas_call(
        _add_pos_body,
        grid=(b, t // tm),
        in_specs=[pl.BlockSpec((1, tm, d), lambda i, j: (i, j, 0)),
                  pl.BlockSpec((tm, d), lambda i, j: (j, 0))],
        out_specs=pl.BlockSpec((1, tm, d), lambda i, j: (i, j, 0)),
        out_shape=jax.ShapeDtypeStruct(x.shape, x.dtype),
        compiler_params=_params("parallel", "parallel"),
        name="add_pos",
    )(x, pos)


def _mix_in_body(x_ref, m_ref, w_ref, z_ref):
    m = m_ref[0]
    h = x_ref[0] * (1.0 + m[1:2]) + m[0:1]
    z_ref[0] = jnp.dot(h.astype(BF16), w_ref[...], preferred_element_type=F32)


def _mod_spec(mods):
    if mods.shape[0] == 1:
        return pl.BlockSpec((1, SUBLANES, D_MODEL), lambda i, j: (0, 0, 0))
    return pl.BlockSpec((1, SUBLANES, D_MODEL), lambda i, j: (i, 0, 0))


def _mix_in(x, mods, w_in):
    b, t, d = x.shape
    tm = min(t, 512)
    return pl.pallas_call(
        _mix_in_body,
        grid=(b, t // tm),
        in_specs=[pl.BlockSpec((1, tm, d), lambda i, j: (i, j, 0)),
                  _mod_spec(mods),
                  pl.BlockSpec((d, D_IN), lambda i, j: (0, 0))],
        out_specs=pl.BlockSpec((1, tm, D_IN), lambda i, j: (i, j, 0)),
        out_shape=jax.ShapeDtypeStruct((b, t, D_IN), F32),
        compiler_params=_params("parallel", "parallel"),
        name="mix_in",
    )(x, mods, w_in)


PAD = SUBLANES
UNROLL_GROUPS = 8


def _rnn_body(xr_ref, gr_ref, h0_ref, cw_ref, cb_ref, wa_ref, wx_ref, ba_ref, bx_ref, lam_ref,
              out_ref, st_ref, xpad, a_s, u_s):
    t_len = xr_ref.shape[1]
    rc = min(ROW_CHUNK, t_len)
    zeros = jnp.zeros((PAD, LANES), F32)
    xpad[0:PAD] = zeros
    xpad[t_len + PAD:t_len + 2 * PAD] = zeros
    xpad[PAD:t_len + PAD] = xr_ref[0]

    cw = cw_ref[...]
    lam = lam_ref[...]
    softplus_neg = jnp.maximum(-lam, 0.0) + jnp.log1p(jnp.exp(-jnp.abs(lam)))
    coef = -LRU_C * softplus_neg
    row = lax.broadcasted_iota(jnp.int32, (rc, LANES), 0) % SUBLANES
    left = CONV_W // 2

    for c in range(t_len // rc):
        t0 = c * rc
        xc = cb_ref[...] + xpad[t0 + PAD - left:t0 + PAD - left + rc] * cw[0:1]
        for k in range(1, CONV_W):
            xc = xc + xpad[t0 + PAD - left + k:t0 + PAD - left + k + rc] * cw[k:k + 1]
        xcb = xc.astype(BF16)
        for d in range(2):
            r = jax.nn.sigmoid(jnp.dot(xcb, wa_ref[d, 0], preferred_element_type=F32) + ba_ref[d:d + 1])
            gi = jax.nn.sigmoid(jnp.dot(xcb, wx_ref[d, 0], preferred_element_type=F32) + bx_ref[d:d + 1])
            log_a = coef[d:d + 1] * r
            a = jnp.exp(log_a)
            u = jnp.sqrt(-jnp.tanh(log_a) * (a * a + 1.0)) * (gi * xc)
            for s in (1, 2, 4):
                if d == 0:
                    keep = row >= s
                    shift = s
                else:
                    keep = row <= SUBLANES - 1 - s
                    shift = rc - s
                a_sh = jnp.where(keep, pltpu.roll(a, shift, 0), 1.0)
                u_sh = jnp.where(keep, pltpu.roll(u, shift, 0), 0.0)
                u = a * u_sh + u
                a = a * a_sh
            a_s[d, t0:t0 + rc] = a
            u_s[d, t0:t0 + rc] = u

    n_groups = t_len // SUBLANES
    n_iter = n_groups // UNROLL_GROUPS

    def carry_step(i, carry):
        hf, hb = carry
        for j in range(UNROLL_GROUPS):
            gf = pl.multiple_of((i * UNROLL_GROUPS + j) * SUBLANES, SUBLANES)
            gb = pl.multiple_of((n_groups - 1 - i * UNROLL_GROUPS - j) * SUBLANES, SUBLANES)
            h_f = u_s[0, pl.ds(gf, SUBLANES)] + a_s[0, pl.ds(gf, SUBLANES)] * hf
            h_b = u_s[1, pl.ds(gb, SUBLANES)] + a_s[1, pl.ds(gb, SUBLANES)] * hb
            u_s[0, pl.ds(gf, SUBLANES)] = h_f
            u_s[1, pl.ds(gb, SUBLANES)] = h_b
            hf = h_f[SUBLANES - 1:SUBLANES]
            hb = h_b[0:1]
        return hf, hb

    h0 = h0_ref[0]
    hf, hb = lax.fori_loop(0, n_iter, carry_step, (h0[0:1], h0[1:2]))
    st_ref[0, 0:1] = hf
    st_ref[0, 1:2] = hb

    for c in range(t_len // rc):
        t0 = c * rc
        hsum = u_s[0, t0:t0 + rc] + u_s[1, t0:t0 + rc]
        out_ref[0, t0:t0 + rc] = (hsum * jax.nn.gelu(gr_ref[0, t0:t0 + rc])).astype(BF16)


def _rnn(z, h0, conv_w, conv_b, wa, wx, b_a, b_x, lam):
    b, t, _ = z.shape
    nblk = D_RNN // LANES
    return pl.pallas_call(
        _rnn_body,
        grid=(b, nblk),
        in_specs=[pl.BlockSpec((1, t, LANES), lambda i, j: (i, 0, j)),
                  pl.BlockSpec((1, t, LANES), lambda i, j: (i, 0, nblk + j)),
                  pl.BlockSpec((1, 2, LANES), lambda i, j: (i, 0, j)),
                  pl.BlockSpec((CONV_W, LANES), lambda i, j: (0, j)),
                  pl.BlockSpec((1, LANES), lambda i, j: (0, j)),
                  pl.BlockSpec((2, 1, LANES, LANES), lambda i, j: (0, j, 0, 0)),
                  pl.BlockSpec((2, 1, LANES, LANES), lambda i, j: (0, j, 0, 0)),
                  pl.BlockSpec((2, LANES), lambda i, j: (0, j)),
                  pl.BlockSpec((2, LANES), lambda i, j: (0, j)),
                  pl.BlockSpec((2, LANES), lambda i, j: (0, j))],
        out_specs=[pl.BlockSpec((1, t, LANES), lambda i, j: (i, 0, j)),
                   pl.BlockSpec((1, 2, LANES), lambda i, j: (i, 0, j))],
        out_shape=[jax.ShapeDtypeStruct((b, t, D_RNN), BF16),
                   jax.ShapeDtypeStruct((b, 2, D_RNN), F32)],
        scratch_shapes=[pltpu.VMEM((t + 2 * PAD, LANES), F32),
                        pltpu.VMEM((2, t, LANES), F32),
                        pltpu.VMEM((2, t, LANES), F32)],
        compiler_params=_params("parallel", "parallel"),
        name="rglru",
    )(z, z, h0, conv_w, conv_b.reshape(1, D_RNN), wa, wx, b_a, b_x, lam)


def _block_diag_pairs(w):
    per = LANES // RNN_HEAD_DIM
    w = w.reshape(2, D_RNN // LANES, per, RNN_HEAD_DIM, RNN_HEAD_DIM)
    out = jnp.zeros((2, D_RNN // LANES, LANES, LANES), w.dtype)
    for p in range(per):
        sl = slice(p * RNN_HEAD_DIM, (p + 1) * RNN_HEAD_DIM)
        out = out.at[:, :, sl, sl].set(w[:, :, p])
    return out.astype(BF16)


POOL_PAD = 16


def _pool_body(xq_ref, w_ref, sc_ref, out_ref, ppad):
    t_len = xq_ref.shape[1]
    rc = min(ROW_CHUNK, t_len)
    zeros = jnp.zeros((POOL_PAD, D_POOL), F32)
    ppad[0:POOL_PAD] = zeros
    ppad[t_len + POOL_PAD:t_len + 2 * POOL_PAD] = zeros
    ppad[POOL_PAD:t_len + POOL_PAD] = xq_ref[0]
    for c in range(t_len // rc):
        t0 = c * rc
        tpos = t0 + lax.broadcasted_iota(jnp.int32, (rc, POOL_GROUP_DIM), 0)
        for g, w in enumerate(POOL_WINDOWS):
            cols = slice(g * POOL_GROUP_DIM, (g + 1) * POOL_GROUP_DIM)
            half = w // 2
            base = t0 + POOL_PAD - half
            s = ppad[base:base + rc, cols]
            for k in range(1, w):
                s = s + ppad[base + k:base + k + rc, cols]
            cnt = (jnp.minimum(tpos + half, t_len) - jnp.maximum(tpos - half, 0)).astype(F32)
            pooled = s / cnt - ppad[t0 + POOL_PAD:t0 + POOL_PAD + rc, cols]
            y = jnp.dot(pooled.astype(BF16), w_ref[g], preferred_element_type=F32)
            out_ref[0, t0:t0 + rc, cols] = (y * sc_ref[:, cols]).astype(BF16)


def _pool(z, w_pool, pool_scale):
    b, t, _ = z.shape
    return pl.pallas_call(
        _pool_body,
        grid=(b,),
        in_specs=[pl.BlockSpec((1, t, D_POOL), lambda i: (i, 0, 2 * D_RNN // D_POOL)),
                  pl.BlockSpec((len(POOL_WINDOWS), POOL_GROUP_DIM, POOL_GROUP_DIM), lambda i: (0, 0, 0)),
                  pl.BlockSpec((1, D_POOL), lambda i: (0, 0))],
        out_specs=pl.BlockSpec((1, t, D_POOL), lambda i: (i, 0, 0)),
        out_shape=jax.ShapeDtypeStruct((b, t, D_POOL), BF16),
        scratch_shapes=[pltpu.VMEM((t + 2 * POOL_PAD, D_POOL), F32)],
        compiler_params=_params("parallel"),
        name="pool",
    )(z, w_pool, pool_scale.reshape(1, D_POOL))


def _mix_out_body(rnn_ref, pool_ref, x_ref, m_ref, w_ref, g_ref, b_ref, x1_ref, hq_ref):
    m = m_ref[0]
    mix = (jnp.dot(rnn_ref[0], w_ref[0:D_RNN], preferred_element_type=F32)
           + jnp.dot(pool_ref[0], w_ref[D_RNN:D_RNN + D_POOL], preferred_element_type=F32))
    x1 = _layer_norm(DEEPNORM_ALPHA * x_ref[0] + m[2:3] * mix, g_ref[...], b_ref[...])
    x1_ref[0] = x1
    hq_ref[0] = (x1 * (1.0 + m[4:5]) + m[3:4]).astype(BF16)


def _mix_out(rnn, pool, x, mods, w_out, ln_g, ln_b):
    b, t, d = x.shape
    tm = min(t, 512)
    tok = lambda i, j: (i, j, 0)
    return pl.pallas_call(
        _mix_out_body,
        grid=(b, t // tm),
        in_specs=[pl.BlockSpec((1, tm, D_RNN), tok),
                  pl.BlockSpec((1, tm, D_POOL), tok),
                  pl.BlockSpec((1, tm, d), tok),
                  _mod_spec(mods),
                  pl.BlockSpec((D_RNN + D_POOL, d), lambda i, j: (0, 0)),
                  pl.BlockSpec((1, d), lambda i, j: (0, 0)),
                  pl.BlockSpec((1, d), lambda i, j: (0, 0))],
        out_specs=[pl.BlockSpec((1, tm, d), tok), pl.BlockSpec((1, tm, d), tok)],
        out_shape=[jax.ShapeDtypeStruct(x.shape, F32), jax.ShapeDtypeStruct(x.shape, BF16)],
        compiler_params=_params("parallel", "parallel"),
        name="mix_out",
    )(rnn, pool, x, mods, w_out, ln_g.reshape(1, d), ln_b.reshape(1, d))


NEG_INF = float("-inf")


def _top_rows(s, ids, k):
    vals, picks = [], []
    big = jnp.int32(2 ** 30)
    for _ in range(k):
        m = jnp.max(s, axis=0, keepdims=True)
        pick = jnp.min(jnp.where(s == m, ids, big), axis=0, keepdims=True)
        s = jnp.where(ids == pick, NEG_INF, s)
        vals.append(m)
        picks.append(pick)
    return jnp.concatenate(vals, axis=0), jnp.concatenate(picks, axis=0)


def _gather_rows(table, ids, picks):
    out = []
    for r in range(picks.shape[0]):
        out.append(jnp.sum(jnp.where(ids == picks[r:r + 1], table, 0), axis=0, keepdims=True))
    return jnp.concatenate(out, axis=0)


def _route_body(hq_ref, wq_ref, keys_ref, idx_ref, gate_ref):
    tm = hq_ref.shape[0]
    q = jnp.dot(hq_ref[...], wq_ref[...], preferred_element_type=F32)
    key_ids = lax.broadcasted_iota(jnp.int32, (PEER_NKEYS, tm), 0)
    half_rows = SUBLANES
    sub = lax.broadcasted_iota(jnp.int32, (half_rows, tm), 0)
    for h in range(PEER_HEADS):
        tops = []
        for p in range(2):
            c0 = (h * 2 + p) * PEER_HALF
            qs = q[:, c0:c0 + PEER_HALF].astype(BF16)
            s_t = lax.dot_general(keys_ref[h, p], qs, (((1,), (1,)), ((), ())),
                                  preferred_element_type=F32)
            tops.append(_top_rows(s_t, key_ids, PEER_TOPK))
        (s1, i1), (s2, i2) = tops
        cand, flat, expert = [], [], []
        for j in range(half_rows):
            ok = (sub + 1) * (j + 1) <= PEER_TOPK
            cand.append(jnp.where(ok, s1[0:half_rows] + s2[j:j + 1], NEG_INF))
            flat.append(sub * PEER_TOPK + j)
            expert.append(i1[0:half_rows] * PEER_NKEYS + i2[j:j + 1])
        cand.append(s1[0:1] + s2[half_rows:2 * half_rows])
        flat.append(sub + half_rows)
        expert.append(i1[0:1] * PEER_NKEYS + i2[half_rows:2 * half_rows])
        cand.append(s1[half_rows:2 * half_rows] + s2[0:1])
        flat.append((sub + half_rows) * PEER_TOPK)
        expert.append(i1[half_rows:2 * half_rows] * PEER_NKEYS + i2[0:1])
        cand = jnp.concatenate(cand, axis=0)
        flat = jnp.concatenate(flat, axis=0)
        expert = jnp.concatenate(expert, axis=0)
        sc, picks = _top_rows(cand, flat, PEER_TOPK)
        e = jnp.exp(sc - sc[0:1])
        rows = slice(h * PEER_TOPK, (h + 1) * PEER_TOPK)
        gate_ref[rows, :] = e / jnp.sum(e, axis=0, keepdims=True)
        idx_ref[rows, :] = _gather_rows(expert, flat, picks)


def _route(hq, wq, keys):
    n, d = hq.shape
    tm = 256
    return pl.pallas_call(
        _route_body,
        grid=(n // tm,),
        in_specs=[pl.BlockSpec((tm, d), lambda i: (i, 0)),
                  pl.BlockSpec((d, PEER_HEADS * PEER_QDIM), lambda i: (0, 0)),
                  pl.BlockSpec((PEER_HEADS, 2, PEER_NKEYS, PEER_HALF), lambda i: (0, 0, 0, 0))],
        out_specs=[pl.BlockSpec((PEER_PICKS, tm), lambda i: (0, i)),
                   pl.BlockSpec((PEER_PICKS, tm), lambda i: (0, i))],
        out_shape=[jax.ShapeDtypeStruct((PEER_PICKS, n), jnp.int32),
                   jax.ShapeDtypeStruct((PEER_PICKS, n), F32)],
        compiler_params=_params("parallel"),
        name="peer_route",
    )(hq, wq, keys)


def _final_body(x1_ref, f_ref, m_ref, g_ref, b_ref, o_ref):
    m = m_ref[0]
    o_ref[0] = _layer_norm(DEEPNORM_ALPHA * x1_ref[0] + m[5:6] * f_ref[0], g_ref[...], b_ref[...])


def _final(x1, ffn, mods, ln_g, ln_b):
    b, t, d = x1.shape
    tm = min(t, 512)
    tok = lambda i, j: (i, j, 0)
    return pl.pallas_call(
        _final_body,
        grid=(b, t // tm),
        in_specs=[pl.BlockSpec((1, tm, d), tok), pl.BlockSpec((1, tm, d), tok), _mod_spec(mods),
                  pl.BlockSpec((1, d), lambda i, j: (0, 0)), pl.BlockSpec((1, d), lambda i, j: (0, 0))],
        out_specs=pl.BlockSpec((1, tm, d), tok),
        out_shape=jax.ShapeDtypeStruct(x1.shape, F32),
        compiler_params=_params("parallel", "parallel"),
        name="final_ln",
    )(x1, ffn, mods, ln_g.reshape(1, d), ln_b.reshape(1, d))


N_EXPERTS = PEER_NKEYS * PEER_NKEYS
GATE_ROWS = 4
GATE_TOKENS = 32
PEER_TB = 512
PEER_EC = 1024


def _gate_matrix(idx, gates):
    n = idx.shape[0]
    info = plsc.get_sparse_core_info()
    n_workers = info.num_cores * info.num_subcores
    lanes = info.num_lanes
    per_w = n // n_workers
    assert n % (n_workers * GATE_TOKENS) == 0 and GATE_TOKENS % GATE_ROWS == 0 and PEER_PICKS % lanes == 0
    vecs = PEER_PICKS // lanes
    mesh = plsc.VectorSubcoreMesh(core_axis_name="core", subcore_axis_name="subcore")

    @functools.partial(
        pl.kernel, out_type=jax.ShapeDtypeStruct((n, N_EXPERTS), F32), mesh=mesh,
        scratch_types=[pltpu.VMEM((GATE_TOKENS * PEER_PICKS,), jnp.int32),
                       pltpu.VMEM((GATE_TOKENS * PEER_PICKS,), F32)]
                      + [pltpu.VMEM((N_EXPERTS,), F32)] * GATE_ROWS
                      + [pltpu.SemaphoreType.DMA] * GATE_ROWS,
        compiler_params=pltpu.CompilerParams(needs_layout_passes=False), name="sc_gate_matrix")
    def run(idx_hbm, gate_hbm, out_hbm, idx_v, g_v, *bufs):
        rows, sems = bufs[:GATE_ROWS], bufs[GATE_ROWS:]
        wid = lax.axis_index("subcore") * info.num_cores + lax.axis_index("core")
        tok_base = wid * per_w
        zeros = jnp.zeros((lanes,), F32)

        @pl.loop(0, N_EXPERTS // lanes)
        def _(i):
            for r in rows:
                r[pl.ds(pl.multiple_of(i * lanes, lanes), lanes)] = zeros

        @pl.loop(0, per_w // GATE_TOKENS)
        def _(blk):
            tok0 = tok_base + blk * GATE_TOKENS
            k_at = pl.ds(pl.multiple_of(tok0 * PEER_PICKS, GATE_TOKENS * PEER_PICKS), GATE_TOKENS * PEER_PICKS)
            pltpu.sync_copy(idx_hbm.at[k_at], idx_v)
            pltpu.sync_copy(gate_hbm.at[k_at], g_v)

            @pl.loop(0, GATE_TOKENS // GATE_ROWS)
            def _(q):
                def picks(b, c):
                    return pl.ds(pl.multiple_of(((q * GATE_ROWS + b) * vecs + c) * lanes, lanes), lanes)

                for b in range(GATE_ROWS):
                    for c in range(vecs):
                        plsc.addupdate_scatter(rows[b], [idx_v[picks(b, c)]], g_v[picks(b, c)])
                    pltpu.make_async_copy(rows[b], out_hbm.at[tok0 + q * GATE_ROWS + b], sems[b]).start()
                for b in range(GATE_ROWS):
                    pltpu.make_async_copy(rows[b], out_hbm.at[tok0 + q * GATE_ROWS + b], sems[b]).wait()
                    for c in range(vecs):
                        plsc.store_scatter(rows[b], [idx_v[picks(b, c)]], zeros)

    return run(idx.reshape(-1), gates.reshape(-1))


def _dense_body(x_ref, u_ref, v_ref, g_ref, o_ref, acc):
    j = pl.program_id(1)

    @pl.when(j == 0)
    def _():
        acc[...] = jnp.zeros_like(acc)

    s = lax.dot_general(x_ref[...], u_ref[...], (((1,), (1,)), ((), ())), preferred_element_type=F32)
    g = g_ref[...]
    a = jnp.where(g != 0.0, jax.nn.gelu(s) * g, 0.0).astype(BF16)
    acc[...] += jnp.dot(a, v_ref[...], preferred_element_type=F32)

    @pl.when(j == pl.num_programs(1) - 1)
    def _():
        o_ref[...] = acc[...]


def _experts(hq, gate_mat, u_tab, v_tab):
    n, d = hq.shape
    tb = min(PEER_TB, n)
    return pl.pallas_call(
        _dense_body,
        grid=(n // tb, N_EXPERTS // PEER_EC),
        in_specs=[pl.BlockSpec((tb, d), lambda i, j: (i, 0)),
                  pl.BlockSpec((PEER_EC, d), lambda i, j: (j, 0)),
                  pl.BlockSpec((PEER_EC, d), lambda i, j: (j, 0)),
                  pl.BlockSpec((tb, PEER_EC), lambda i, j: (i, j))],
        out_specs=pl.BlockSpec((tb, d), lambda i, j: (i, 0)),
        out_shape=jax.ShapeDtypeStruct((n, d), F32),
        scratch_shapes=[pltpu.VMEM((tb, d), F32)],
        compiler_params=_params("parallel", "arbitrary"),
        name="peer_dense",
    )(hq, u_tab, v_tab, gate_mat)


def _grid_pos_embed(rows):
    t = jnp.arange(rows * GRID_W)
    r = (t // GRID_W).astype(F32)
    col = (t % GRID_W).astype(F32)
    n_freq = D_MODEL // 4
    omega = 1.0 / (10000.0 ** (jnp.arange(n_freq, dtype=F32) / n_freq))

    def enc(p):
        ang = p[:, None] * omega[None, :]
        return jnp.concatenate([jnp.sin(ang), jnp.cos(ang)], -1)
    return jnp.concatenate([enc(r), enc(col)], -1)


def _layer(x, mods, h0, p):
    b, t, d = x.shape
    z = _mix_in(x, mods, p["w_in"])
    rnn, st = _rnn(z, h0, p["conv_w"], p["conv_b"], p["wa"], p["wx"], p["b_a"], p["b_x"], p["lam"])
    pool = _pool(z, p["w_pool"], p["pool_scale"])
    x1, hq = _mix_out(rnn, pool, x, mods, p["w_out"], p["ln1_g"], p["ln1_b"])
    hq = hq.reshape(b * t, d)
    idx_t, gate_t = _route(hq, p["wq"], p["keys"])
    gate_mat = _gate_matrix(idx_t.T, gate_t.T)
    ffn = _experts(hq, gate_mat, p["peer_u"], p["peer_v"])
    x2 = _final(x1, ffn.reshape(b, t, d), mods, p["ln2_g"], p["ln2_b"])
    return x2, st


def kernel(x_prompt, x_sample, state_rglru, c, c_ctx, w_mod, b_mod, w_in, conv_w, conv_b,
           w_rg_a, b_rg_a, w_rg_x, b_rg_x, lru_lambda, w_pool, pool_scale, w_out,
           ln1_g, ln1_b, ln2_g, ln2_b, peer_wq, peer_keys, peer_u, peer_v):
    n_req = c.shape[0]
    cvecs = jnp.zeros((MOD_ROWS, D_MODEL), F32).at[:n_req].set(c).at[n_req].set(c_ctx)
    mod = _modulation(cvecs, w_mod, b_mod).reshape(DEPTH, MOD_ROWS, N_MOD, D_MODEL)
    mod = jnp.pad(mod, ((0, 0), (0, 0), (0, SUBLANES - N_MOD), (0, 0)))

    rows = x_sample.shape[1] // GRID_W
    xs = _add_pos(x_sample, _grid_pos_embed(rows))
    xp = x_prompt
    zero_state = jnp.zeros((x_prompt.shape[0], 2, D_RNN), F32)
    ctx_states = []
    for l in range(DEPTH):
        p = dict(w_in=w_in[l].astype(BF16), conv_w=conv_w[l], conv_b=conv_b[l],
                 wa=_block_diag_pairs(w_rg_a[l]), wx=_block_diag_pairs(w_rg_x[l]),
                 b_a=b_rg_a[l], b_x=b_rg_x[l], lam=lru_lambda[l],
                 w_pool=w_pool[l].astype(BF16), pool_scale=pool_scale[l], w_out=w_out[l].astype(BF16),
                 ln1_g=ln1_g[l], ln1_b=ln1_b[l], ln2_g=ln2_g[l], ln2_b=ln2_b[l],
                 wq=peer_wq[l].astype(BF16), keys=peer_keys[l].astype(BF16),
                 peer_u=peer_u[l].astype(BF16), peer_v=peer_v[l].astype(BF16))
        xp, st = _layer(xp, mod[l, n_req:n_req + 1], zero_state, p)
        ctx_states.append(st)
        xs, _ = _layer(xs, mod[l, :n_req], state_rglru[:, l], p)
    return xp, xs, jnp.stack(ctx_states, axis=1)
```

```python
import functools
import math

import jax
import jax.numpy as jnp
from jax import lax
from jax.experimental import pallas as pl
from jax.experimental.pallas import tpu as pltpu
from jax.experimental.pallas import tpu_sc as plsc

F32 = jnp.float32
BF16 = jnp.bfloat16

D_MODEL = 1024
DEPTH = 2
GRID_W = 64
D_RNN = 512
N_RNN_HEADS = 8
RNN_HEAD_DIM = D_RNN // N_RNN_HEADS
CONV_W = 4
LRU_C = 8.0
D_POOL = 512
POOL_WINDOWS = (2, 4, 8, 16)
POOL_GROUP_DIM = D_POOL // len(POOL_WINDOWS)
D_IN = 2 * D_RNN + D_POOL
PEER_HEADS = 8
PEER_NKEYS = 128
PEER_TOPK = 16
PEER_QDIM = 256
PEER_HALF = PEER_QDIM // 2
PEER_PICKS = PEER_HEADS * PEER_TOPK
N_MOD = 6
DEEPNORM_ALPHA = (2 * DEPTH) ** 0.25
LN_EPS = 1e-5

LANES = 128
SUBLANES = 8
VMEM_LIMIT_BYTES = 56 * 1024 * 1024
MOD_ROWS = 16
ROW_CHUNK = 256


def _params(*sem):
    return pltpu.CompilerParams(dimension_semantics=sem, vmem_limit_bytes=VMEM_LIMIT_BYTES)


def _layer_norm(y, g, b):
    mu = jnp.mean(y, -1, keepdims=True)
    var = jnp.mean(jnp.square(y - mu), -1, keepdims=True)
    return (y - mu) * lax.rsqrt(var + LN_EPS) * g + b


def _mod_body(c_ref, w_ref, b_ref, o_ref):
    c = c_ref[...]
    s = (c * jax.nn.sigmoid(c)).astype(BF16)
    o_ref[0] = jnp.dot(s, w_ref[0].astype(BF16), preferred_element_type=F32) + b_ref[0]


def _modulation(cvecs, w_mod, b_mod):
    n_out = N_MOD * D_MODEL
    tn = 1536
    return pl.pallas_call(
        _mod_body,
        grid=(DEPTH, n_out // tn),
        in_specs=[pl.BlockSpec((MOD_ROWS, D_MODEL), lambda l, j: (0, 0)),
                  pl.BlockSpec((1, D_MODEL, tn), lambda l, j: (l, 0, j)),
                  pl.BlockSpec((1, 1, tn), lambda l, j: (l, 0, j))],
        out_specs=pl.BlockSpec((1, MOD_ROWS, tn), lambda l, j: (l, 0, j)),
        out_shape=jax.ShapeDtypeStruct((DEPTH, MOD_ROWS, n_out), F32),
        compiler_params=_params("parallel", "parallel"),
        name="modulation",
    )(cvecs, w_mod, b_mod.reshape(DEPTH, 1, n_out))


def _add_pos_body(x_ref, p_ref, o_ref):
    o_ref[0] = x_ref[0] + p_ref[...]


def _add_pos(x, pos):
    b, t, d = x.shape
    tm = min(t, 512)
    return pl.pallas_call(
        _add_pos_body,
        grid=(b, t // tm),
        in_specs=[pl.BlockSpec((1, tm, d), lambda i, j: (i, j, 0)),
                  pl.BlockSpec((tm, d), lambda i, j: (j, 0))],
        out_specs=pl.BlockSpec((1, tm, d), lambda i, j: (i, j, 0)),
        out_shape=jax.ShapeDtypeStruct(x.shape, x.dtype),
        compiler_params=_params("parallel", "parallel"),
        name="add_pos",
    )(x, pos)


def _mix_in_body(x_ref, m_ref, w_ref, z_ref):
    m = m_ref[0]
    h = x_ref[0] * (1.0 + m[1:2]) + m[0:1]
    z_ref[0] = jnp.dot(h.astype(BF16), w_ref[...], preferred_element_type=F32)


def _mod_spec(mods):
    if mods.shape[0] == 1:
        return pl.BlockSpec((1, SUBLANES, D_MODEL), lambda i, j: (0, 0, 0))
    return pl.BlockSpec((1, SUBLANES, D_MODEL), lambda i, j: (i, 0, 0))


def _mix_in(x, mods, w_in):
    b, t, d = x.shape
    tm = min(t, 512)
    return pl.pallas_call(
        _mix_in_body,
        grid=(b, t // tm),
        in_specs=[pl.BlockSpec((1, tm, d), lambda i, j: (i, j, 0)),
                  _mod_spec(mods),
                  pl.BlockSpec((d, D_IN), lambda i, j: (0, 0))],
        out_specs=pl.BlockSpec((1, tm, D_IN), lambda i, j: (i, j, 0)),
        out_shape=jax.ShapeDtypeStruct((b, t, D_IN), F32),
        compiler_params=_params("parallel", "parallel"),
        name="mix_in",
    )(x, mods, w_in)


PAD = SUBLANES
UNROLL_GROUPS = 8


def _rnn_body(xr_ref, gr_ref, h0_ref, cw_ref, cb_ref, wa_ref, wx_ref, ba_ref, bx_ref, lam_ref,
              out_ref, st_ref, xpad, a_s, u_s):
    t_len = xr_ref.shape[1]
    rc = min(ROW_CHUNK, t_len)
    zeros = jnp.zeros((PAD, LANES), F32)
    xpad[0:PAD] = zeros
    xpad[t_len + PAD:t_len + 2 * PAD] = zeros
    xpad[PAD:t_len + PAD] = xr_ref[0]

    cw = cw_ref[...]
    lam = lam_ref[...]
    softplus_neg = jnp.maximum(-lam, 0.0) + jnp.log1p(jnp.exp(-jnp.abs(lam)))
    coef = -LRU_C * softplus_neg
    row = lax.broadcasted_iota(jnp.int32, (rc, LANES), 0) % SUBLANES
    left = CONV_W // 2

    for c in range(t_len // rc):
        t0 = c * rc
        xc = cb_ref[...] + xpad[t0 + PAD - left:t0 + PAD - left + rc] * cw[0:1]
        for k in range(1, CONV_W):
            xc = xc + xpad[t0 + PAD - left + k:t0 + PAD - left + k + rc] * cw[k:k + 1]
        xcb = xc.astype(BF16)
        for d in range(2):
            r = jax.nn.sigmoid(jnp.dot(xcb, wa_ref[d, 0], preferred_element_type=F32) + ba_ref[d:d + 1])
            gi = jax.nn.sigmoid(jnp.dot(xcb, wx_ref[d, 0], preferred_element_type=F32) + bx_ref[d:d + 1])
            log_a = coef[d:d + 1] * r
            a = jnp.exp(log_a)
            u = jnp.sqrt(-jnp.tanh(log_a) * (a * a + 1.0)) * (gi * xc)
            for s in (1, 2, 4):
                if d == 0:
                    keep = row >= s
                    shift = s
                else:
                    keep = row <= SUBLANES - 1 - s
                    shift = rc - s
                a_sh = jnp.where(keep, pltpu.roll(a, shift, 0), 1.0)
                u_sh = jnp.where(keep, pltpu.roll(u, shift, 0), 0.0)
                u = a * u_sh + u
                a = a * a_sh
            a_s[d, t0:t0 + rc] = a
            u_s[d, t0:t0 + rc] = u

    n_groups = t_len // SUBLANES
    n_iter = n_groups // UNROLL_GROUPS

    def carry_step(i, carry):
        hf, hb = carry
        for j in range(UNROLL_GROUPS):
            gf = pl.multiple_of((i * UNROLL_GROUPS + j) * SUBLANES, SUBLANES)
            gb = pl.multiple_of((n_groups - 1 - i * UNROLL_GROUPS - j) * SUBLANES, SUBLANES)
            h_f = u_s[0, pl.ds(gf, SUBLANES)] + a_s[0, pl.ds(gf, SUBLANES)] * hf
            h_b = u_s[1, pl.ds(gb, SUBLANES)] + a_s[1, pl.ds(gb, SUBLANES)] * hb
            u_s[0, pl.ds(gf, SUBLANES)] = h_f
            u_s[1, pl.ds(gb, SUBLANES)] = h_b
            hf = h_f[SUBLANES - 1:SUBLANES]
            hb = h_b[0:1]
        return hf, hb

    h0 = h0_ref[0]
    hf, hb = lax.fori_loop(0, n_iter, carry_step, (h0[0:1], h0[1:2]))
    st_ref[0, 0:1] = hf
    st_ref[0, 1:2] = hb

    for c in range(t_len // rc):
        t0 = c * rc
        hsum = u_s[0, t0:t0 + rc] + u_s[1, t0:t0 + rc]
        out_ref[0, t0:t0 + rc] = (hsum * jax.nn.gelu(gr_ref[0, t0:t0 + rc])).astype(BF16)


def _rnn(z, h0, conv_w, conv_b, wa, wx, b_a, b_x, lam):
    b, t, _ = z.shape
    nblk = D_RNN // LANES
    return pl.pallas_call(
        _rnn_body,
        grid=(b, nblk),
        in_specs=[pl.BlockSpec((1, t, LANES), lambda i, j: (i, 0, j)),
                  pl.BlockSpec((1, t, LANES), lambda i, j: (i, 0, nblk + j)),
                  pl.BlockSpec((1, 2, LANES), lambda i, j: (i, 0, j)),
                  pl.BlockSpec((CONV_W, LANES), lambda i, j: (0, j)),
                  pl.BlockSpec((1, LANES), lambda i, j: (0, j)),
                  pl.BlockSpec((2, 1, LANES, LANES), lambda i, j: (0, j, 0, 0)),
                  pl.BlockSpec((2, 1, LANES, LANES), lambda i, j: (0, j, 0, 0)),
                  pl.BlockSpec((2, LANES), lambda i, j: (0, j)),
                  pl.BlockSpec((2, LANES), lambda i, j: (0, j)),
                  pl.BlockSpec((2, LANES), lambda i, j: (0, j))],
        out_specs=[pl.BlockSpec((1, t, LANES), lambda i, j: (i, 0, j)),
                   pl.BlockSpec((1, 2, LANES), lambda i, j: (i, 0, j))],
        out_shape=[jax.ShapeDtypeStruct((b, t, D_RNN), BF16),
                   jax.ShapeDtypeStruct((b, 2, D_RNN), F32)],
        scratch_shapes=[pltpu.VMEM((t + 2 * PAD, LANES), F32),
                        pltpu.VMEM((2, t, LANES), F32),
                        pltpu.VMEM((2, t, LANES), F32)],
        compiler_params=_params("parallel", "parallel"),
        name="rglru",
    )(z, z, h0, conv_w, conv_b.reshape(1, D_RNN), wa, wx, b_a, b_x, lam)


def _block_diag_pairs(w):
    per = LANES // RNN_HEAD_DIM
    w = w.reshape(2, D_RNN // LANES, per, RNN_HEAD_DIM, RNN_HEAD_DIM)
    out = jnp.zeros((2, D_RNN // LANES, LANES, LANES), w.dtype)
    for p in range(per):
        sl = slice(p * RNN_HEAD_DIM, (p + 1) * RNN_HEAD_DIM)
        out = out.at[:, :, sl, sl].set(w[:, :, p])
    return out.astype(BF16)


POOL_PAD = 16


def _pool_body(xq_ref, w_ref, sc_ref, out_ref, ppad):
    t_len = xq_ref.shape[1]
    rc = min(ROW_CHUNK, t_len)
    zeros = jnp.zeros((POOL_PAD, D_POOL), F32)
    ppad[0:POOL_PAD] = zeros
    ppad[t_len + POOL_PAD:t_len + 2 * POOL_PAD] = zeros
    ppad[POOL_PAD:t_len + POOL_PAD] = xq_ref[0]
    for c in range(t_len // rc):
        t0 = c * rc
        tpos = t0 + lax.broadcasted_iota(jnp.int32, (rc, POOL_GROUP_DIM), 0)
        for g, w in enumerate(POOL_WINDOWS):
            cols = slice(g * POOL_GROUP_DIM, (g + 1) * POOL_GROUP_DIM)
            half = w // 2
            base = t0 + POOL_PAD - half
            s = ppad[base:base + rc, cols]
            for k in range(1, w):
                s = s + ppad[base + k:base + k + rc, cols]
            cnt = (jnp.minimum(tpos + half, t_len) - jnp.maximum(tpos - half, 0)).astype(F32)
            pooled = s / cnt - ppad[t0 + POOL_PAD:t0 + POOL_PAD + rc, cols]
            y = jnp.dot(pooled.astype(BF16), w_ref[g], preferred_element_type=F32)
            out_ref[0, t0:t0 + rc, cols] = (y * sc_ref[:, cols]).astype(BF16)


def _pool(z, w_pool, pool_scale):
    b, t, _ = z.shape
    return pl.pallas_call(
        _pool_body,
        grid=(b,),
        in_specs=[pl.BlockSpec((1, t, D_POOL), lambda i: (i, 0, 2 * D_RNN // D_POOL)),
                  pl.BlockSpec((len(POOL_WINDOWS), POOL_GROUP_DIM, POOL_GROUP_DIM), lambda i: (0, 0, 0)),
                  pl.BlockSpec((1, D_POOL), lambda i: (0, 0))],
        out_specs=pl.BlockSpec((1, t, D_POOL), lambda i: (i, 0, 0)),
        out_shape=jax.ShapeDtypeStruct((b, t, D_POOL), BF16),
        scratch_shapes=[pltpu.VMEM((t + 2 * POOL_PAD, D_POOL), F32)],
        compiler_params=_params("parallel"),
        name="pool",
    )(z, w_pool, pool_scale.reshape(1, D_POOL))


def _mix_out_body(rnn_ref, pool_ref, x_ref, m_ref, w_ref, g_ref, b_ref, x1_ref, hq_ref):
    m = m_ref[0]
    mix = (jnp.dot(rnn_ref[0], w_ref[0:D_RNN], preferred_element_type=F32)
           + jnp.dot(pool_ref[0], w_ref[D_RNN:D_RNN + D_POOL], preferred_element_type=F32))
    x1 = _layer_norm(DEEPNORM_ALPHA * x_ref[0] + m[2:3] * mix, g_ref[...], b_ref[...])
    x1_ref[0] = x1
    hq_ref[0] = (x1 * (1.0 + m[4:5]) + m[3:4]).astype(BF16)


def _mix_out(rnn, pool, x, mods, w_out, ln_g, ln_b):
    b, t, d = x.shape
    tm = min(t, 512)
    tok = lambda i, j: (i, j, 0)
    return pl.pallas_call(
        _mix_out_body,
        grid=(b, t // tm),
        in_specs=[pl.BlockSpec((1, tm, D_RNN), tok),
                  pl.BlockSpec((1, tm, D_POOL), tok),
                  pl.BlockSpec((1, tm, d), tok),
                  _mod_spec(mods),
                  pl.BlockSpec((D_RNN + D_POOL, d), lambda i, j: (0, 0)),
                  pl.BlockSpec((1, d), lambda i, j: (0, 0)),
                  pl.BlockSpec((1, d), lambda i, j: (0, 0))],
        out_specs=[pl.BlockSpec((1, tm, d), tok), pl.BlockSpec((1, tm, d), tok)],
        out_shape=[jax.ShapeDtypeStruct(x.shape, F32), jax.ShapeDtypeStruct(x.shape, BF16)],
        compiler_params=_params("parallel", "parallel"),
        name="mix_out",
    )(rnn, pool, x, mods, w_out, ln_g.reshape(1, d), ln_b.reshape(1, d))


NEG_INF = float("-inf")


BIG_ID = 2 ** 30


def _sort_network(n):
    def merge(lo, hi, r):
        step = r * 2
        if step < hi - lo:
            yield from merge(lo, hi, step)
            yield from merge(lo + r, hi, step)
            for i in range(lo + r, hi - r, step):
                yield (i, i + r)
        else:
            yield (lo, lo + r)

    def sort(lo, hi):
        if hi - lo >= 1:
            mid = lo + (hi - lo) // 2
            yield from sort(lo, mid)
            yield from sort(mid + 1, hi)
            yield from merge(lo, hi, 1)

    return list(sort(0, n - 1))


def _sorted_levels(s_t):
    n_lvl = s_t.shape[0] // SUBLANES
    sub = lax.broadcasted_iota(jnp.int32, (SUBLANES, s_t.shape[1]), 0)
    vals = [s_t[l * SUBLANES:(l + 1) * SUBLANES] for l in range(n_lvl)]
    ids = [sub + l * SUBLANES for l in range(n_lvl)]
    untouched = [True] * n_lvl
    for i, j in _sort_network(n_lvl):
        a, b, ia, ib = vals[i], vals[j], ids[i], ids[j]
        swap = b > a
        if not (untouched[i] and untouched[j]):
            swap = swap | ((b == a) & (ib < ia))
        vals[i], vals[j] = jnp.maximum(a, b), jnp.minimum(a, b)
        ids[i], ids[j] = jnp.where(swap, ib, ia), jnp.where(swap, ia, ib)
        untouched[i] = untouched[j] = False
    return vals, ids


def _pop_top(vals, ids, k):
    vals, ids = list(vals), list(ids)
    out_v, out_i = [], []
    for it in range(k):
        m = jnp.max(vals[0], axis=0, keepdims=True)
        pick = jnp.min(jnp.where(vals[0] == m, ids[0], BIG_ID), axis=0, keepdims=True)
        sel = ids[0] == pick
        out_v.append(m)
        out_i.append(pick)
        for l in range(min(len(vals) - 1, k - 1 - it)):
            vals[l] = jnp.where(sel, vals[l + 1], vals[l])
            ids[l] = jnp.where(sel, ids[l + 1], ids[l])
    return jnp.concatenate(out_v, axis=0), jnp.concatenate(out_i, axis=0)


def _pair_top(s1, i1, s2, i2):
    k = PEER_TOPK
    tm = s1.shape[1]
    sub = lax.broadcasted_iota(jnp.int32, (SUBLANES, tm), 0)
    s2lo, i2lo = s2[0:SUBLANES], i2[0:SUBLANES]
    cand, expert = [], []
    for l in range(k):
        ok = (sub + 1) * (l + 1) <= k
        cand.append(jnp.where(ok, s1[l:l + 1] + s2lo, NEG_INF))
        expert.append(i1[l:l + 1] * PEER_NKEYS + i2lo)
    single = s1[0:1] + s2[SUBLANES:k]
    single_expert = i1[0:1] * PEER_NKEYS + i2[SUBLANES:k]
    single_flat = sub + SUBLANES
    pops = jnp.zeros((SUBLANES, tm), jnp.int32)
    out_s, out_e = [], []
    for it in range(k):
        flat = pops * k + sub
        m = jnp.max(jnp.maximum(cand[0], single), axis=0, keepdims=True)
        pick = jnp.min(jnp.minimum(jnp.where(cand[0] == m, flat, BIG_ID),
                                   jnp.where(single == m, single_flat, BIG_ID)), axis=0, keepdims=True)
        sel = flat == pick
        sel_single = single_flat == pick
        out_s.append(m)
        out_e.append(jnp.sum(jnp.where(sel, expert[0], 0) + jnp.where(sel_single, single_expert, 0),
                             axis=0, keepdims=True))
        single = jnp.where(sel_single, NEG_INF, single)
        pops = jnp.where(sel, pops + 1, pops)
        for l in range(k - 1 - it):
            cand[l] = jnp.where(sel, cand[l + 1], cand[l])
            expert[l] = jnp.where(sel, expert[l + 1], expert[l])
    return jnp.concatenate(out_s, axis=0), jnp.concatenate(out_e, axis=0)


def _route_body(hq_ref, wq_ref, keys_ref, idx_ref, gate_ref):
    q = jnp.dot(hq_ref[...], wq_ref[...], preferred_element_type=F32)
    for h in range(PEER_HEADS):
        tops = []
        for p in range(2):
            c0 = (h * 2 + p) * PEER_HALF
            qs = q[:, c0:c0 + PEER_HALF].astype(BF16)
            s_t = lax.dot_general(keys_ref[h, p], qs, (((1,), (1,)), ((), ())),
                                  preferred_element_type=F32)
            tops.append(_pop_top(*_sorted_levels(s_t), PEER_TOPK))
        (s1, i1), (s2, i2) = tops
        sc, experts = _pair_top(s1, i1, s2, i2)
        e = jnp.exp(sc - sc[0:1])
        rows = slice(h * PEER_TOPK, (h + 1) * PEER_TOPK)
        gate_ref[rows, :] = e / jnp.sum(e, axis=0, keepdims=True)
        idx_ref[rows, :] = experts


def _route(hq, wq, keys):
    n, d = hq.shape
    tm = 256
    return pl.pallas_call(
        _route_body,
        grid=(n // tm,),
        in_specs=[pl.BlockSpec((tm, d), lambda i: (i, 0)),
                  pl.BlockSpec((d, PEER_HEADS * PEER_QDIM), lambda i: (0, 0)),
                  pl.BlockSpec((PEER_HEADS, 2, PEER_NKEYS, PEER_HALF), lambda i: (0, 0, 0, 0))],
        out_specs=[pl.BlockSpec((PEER_PICKS, tm), lambda i: (0, i)),
                   pl.BlockSpec((PEER_PICKS, tm), lambda i: (0, i))],
        out_shape=[jax.ShapeDtypeStruct((PEER_PICKS, n), jnp.int32),
                   jax.ShapeDtypeStruct((PEER_PICKS, n), F32)],
        compiler_params=_params("parallel"),
        name="peer_route",
    )(hq, wq, keys)


def _final_body(x1_ref, f_ref, m_ref, g_ref, b_ref, o_ref):
    m = m_ref[0]
    o_ref[0] = _layer_norm(DEEPNORM_ALPHA * x1_ref[0] + m[5:6] * f_ref[0], g_ref[...], b_ref[...])


def _final(x1, ffn, mods, ln_g, ln_b):
    b, t, d = x1.shape
    tm = min(t, 512)
    tok = lambda i, j: (i, j, 0)
    return pl.pallas_call(
        _final_body,
        grid=(b, t // tm),
        in_specs=[pl.BlockSpec((1, tm, d), tok), pl.BlockSpec((1, tm, d), tok), _mod_spec(mods),
                  pl.BlockSpec((1, d), lambda i, j: (0, 0)), pl.BlockSpec((1, d), lambda i, j: (0, 0))],
        out_specs=pl.BlockSpec((1, tm, d), tok),
        out_shape=jax.ShapeDtypeStruct(x1.shape, F32),
        compiler_params=_params("parallel", "parallel"),
        name="final_ln",
    )(x1, ffn, mods, ln_g.reshape(1, d), ln_b.reshape(1, d))


N_EXPERTS = PEER_NKEYS * PEER_NKEYS
GATE_ROWS = 4
GATE_TOKENS = 32
PEER_TB = 1024
PEER_EC = 2048


def _gate_matrix(idx, gates):
    n = idx.shape[0]
    info = plsc.get_sparse_core_info()
    n_workers = info.num_cores * info.num_subcores
    lanes = info.num_lanes
    per_w = n // n_workers
    assert n % (n_workers * GATE_TOKENS) == 0 and GATE_TOKENS % GATE_ROWS == 0 and PEER_PICKS % lanes == 0
    vecs = PEER_PICKS // lanes
    mesh = plsc.VectorSubcoreMesh(core_axis_name="core", subcore_axis_name="subcore")

    @functools.partial(
        pl.kernel, out_type=jax.ShapeDtypeStruct((n, N_EXPERTS), F32), mesh=mesh,
        scratch_types=[pltpu.VMEM((GATE_TOKENS * PEER_PICKS,), jnp.int32),
                       pltpu.VMEM((GATE_TOKENS * PEER_PICKS,), F32)]
                      + [pltpu.VMEM((N_EXPERTS,), F32)] * GATE_ROWS
                      + [pltpu.SemaphoreType.DMA] * GATE_ROWS,
        compiler_params=pltpu.CompilerParams(needs_layout_passes=False), name="sc_gate_matrix")
    def run(idx_hbm, gate_hbm, out_hbm, idx_v, g_v, *bufs):
        rows, sems = bufs[:GATE_ROWS], bufs[GATE_ROWS:]
        wid = lax.axis_index("subcore") * info.num_cores + lax.axis_index("core")
        tok_base = wid * per_w
        zeros = jnp.zeros((lanes,), F32)

        @pl.loop(0, N_EXPERTS // lanes)
        def _(i):
            for r in rows:
                r[pl.ds(pl.multiple_of(i * lanes, lanes), lanes)] = zeros

        @pl.loop(0, per_w // GATE_TOKENS)
        def _(blk):
            tok0 = tok_base + blk * GATE_TOKENS
            k_at = pl.ds(pl.multiple_of(tok0 * PEER_PICKS, GATE_TOKENS * PEER_PICKS), GATE_TOKENS * PEER_PICKS)
            pltpu.sync_copy(idx_hbm.at[k_at], idx_v)
            pltpu.sync_copy(gate_hbm.at[k_at], g_v)

            @pl.loop(0, GATE_TOKENS // GATE_ROWS)
            def _(q):
                def picks(b, c):
                    return pl.ds(pl.multiple_of(((q * GATE_ROWS + b) * vecs + c) * lanes, lanes), lanes)

                for b in range(GATE_ROWS):
                    for c in range(vecs):
                        plsc.addupdate_scatter(rows[b], [idx_v[picks(b, c)]], g_v[picks(b, c)])
                    pltpu.make_async_copy(rows[b], out_hbm.at[tok0 + q * GATE_ROWS + b], sems[b]).start()
                for b in range(GATE_ROWS):
                    pltpu.make_async_copy(rows[b], out_hbm.at[tok0 + q * GATE_ROWS + b], sems[b]).wait()
                    for c in range(vecs):
                        plsc.store_scatter(rows[b], [idx_v[picks(b, c)]], zeros)

    return run(idx.reshape(-1), gates.reshape(-1))


def _dense_body(x_ref, u_ref, v_ref, g_ref, o_ref, acc):
    j = pl.program_id(1)

    @pl.when(j == 0)
    def _():
        acc[...] = jnp.zeros_like(acc)

    s = lax.dot_general(x_ref[...], u_ref[...], (((1,), (1,)), ((), ())), preferred_element_type=F32)
    g = g_ref[...]
    a = jnp.where(g != 0.0, jax.nn.gelu(s) * g, 0.0).astype(BF16)
    acc[...] += jnp.dot(a, v_ref[...], preferred_element_type=F32)

    @pl.when(j == pl.num_programs(1) - 1)
    def _():
        o_ref[...] = acc[...]


def _experts(hq, gate_mat, u_tab, v_tab):
    n, d = hq.shape
    tb = min(PEER_TB, n)
    return pl.pallas_call(
        _dense_body,
        grid=(n // tb, N_EXPERTS // PEER_EC),
        in_specs=[pl.BlockSpec((tb, d), lambda i, j: (i, 0)),
                  pl.BlockSpec((PEER_EC, d), lambda i, j: (j, 0)),
                  pl.BlockSpec((PEER_EC, d), lambda i, j: (j, 0)),
                  pl.BlockSpec((tb, PEER_EC), lambda i, j: (i, j))],
        out_specs=pl.BlockSpec((tb, d), lambda i, j: (i, 0)),
        out_shape=jax.ShapeDtypeStruct((n, d), F32),
        scratch_shapes=[pltpu.VMEM((tb, d), F32)],
        compiler_params=_params("parallel", "arbitrary"),
        name="peer_dense",
    )(hq, u_tab, v_tab, gate_mat)


def _grid_pos_embed(rows):
    t = jnp.arange(rows * GRID_W)
    r = (t // GRID_W).astype(F32)
    col = (t % GRID_W).astype(F32)
    n_freq = D_MODEL // 4
    omega = 1.0 / (10000.0 ** (jnp.arange(n_freq, dtype=F32) / n_freq))

    def enc(p):
        ang = p[:, None] * omega[None, :]
        return jnp.concatenate([jnp.sin(ang), jnp.cos(ang)], -1)
    return jnp.concatenate([enc(r), enc(col)], -1)


def _layer(x, mods, h0, p):
    b, t, d = x.shape
    z = _mix_in(x, mods, p["w_in"])
    rnn, st = _rnn(z, h0, p["conv_w"], p["conv_b"], p["wa"], p["wx"], p["b_a"], p["b_x"], p["lam"])
    pool = _pool(z, p["w_pool"], p["pool_scale"])
    x1, hq = _mix_out(rnn, pool, x, mods, p["w_out"], p["ln1_g"], p["ln1_b"])
    hq = hq.reshape(b * t, d)
    idx_t, gate_t = _route(hq, p["wq"], p["keys"])
    gate_mat = _gate_matrix(idx_t.T, gate_t.T)
    ffn = _experts(hq, gate_mat, p["peer_u"], p["peer_v"])
    x2 = _final(x1, ffn.reshape(b, t, d), mods, p["ln2_g"], p["ln2_b"])
    return x2, st


def kernel(x_prompt, x_sample, state_rglru, c, c_ctx, w_mod, b_mod, w_in, conv_w, conv_b,
           w_rg_a, b_rg_a, w_rg_x, b_rg_x, lru_lambda, w_pool, pool_scale, w_out,
           ln1_g, ln1_b, ln2_g, ln2_b, peer_wq, peer_keys, peer_u, peer_v):
    n_req = c.shape[0]
    cvecs = jnp.zeros((MOD_ROWS, D_MODEL), F32).at[:n_req].set(c).at[n_req].set(c_ctx)
    mod = _modulation(cvecs, w_mod, b_mod).reshape(DEPTH, MOD_ROWS, N_MOD, D_MODEL)
    mod = jnp.pad(mod, ((0, 0), (0, 0), (0, SUBLANES - N_MOD), (0, 0)))

    rows = x_sample.shape[1] // GRID_W
    xs = _add_pos(x_sample, _grid_pos_embed(rows))
    xp = x_prompt
    zero_state = jnp.zeros((x_prompt.shape[0], 2, D_RNN), F32)
    ctx_states = []
    for l in range(DEPTH):
        p = dict(w_in=w_in[l].astype(BF16), conv_w=conv_w[l], conv_b=conv_b[l],
                 wa=_block_diag_pairs(w_rg_a[l]), wx=_block_diag_pairs(w_rg_x[l]),
                 b_a=b_rg_a[l], b_x=b_rg_x[l], lam=lru_lambda[l],
                 w_pool=w_pool[l].astype(BF16), pool_scale=pool_scale[l], w_out=w_out[l].astype(BF16),
                 ln1_g=ln1_g[l], ln1_b=ln1_b[l], ln2_g=ln2_g[l], ln2_b=ln2_b[l],
                 wq=peer_wq[l].astype(BF16), keys=peer_keys[l].astype(BF16),
                 peer_u=peer_u[l].astype(BF16), peer_v=peer_v[l].astype(BF16))
        xp, st = _layer(xp, mod[l, n_req:n_req + 1], zero_state, p)
        ctx_states.append(st)
        xs, _ = _layer(xs, mod[l, :n_req], state_rglru[:, l], p)
    return xp, xs, jnp.stack(ctx_states, axis=1)
```

```python
import functools
import math

import jax
import jax.numpy as jnp
from jax import lax
from jax.experimental import pallas as pl
from jax.experimental.pallas import tpu as pltpu
from jax.experimental.pallas import tpu_sc as plsc

F32 = jnp.float32
BF16 = jnp.bfloat16

D_MODEL = 1024
DEPTH = 2
GRID_W = 64
D_RNN = 512
N_RNN_HEADS = 8
RNN_HEAD_DIM = D_RNN // N_RNN_HEADS
CONV_W = 4
LRU_C = 8.0
D_POOL = 512
POOL_WINDOWS = (2, 4, 8, 16)
POOL_GROUP_DIM = D_POOL // len(POOL_WINDOWS)
D_IN = 2 * D_RNN + D_POOL
PEER_HEADS = 8
PEER_NKEYS = 128
PEER_TOPK = 16
PEER_QDIM = 256
PEER_HALF = PEER_QDIM // 2
PEER_PICKS = PEER_HEADS * PEER_TOPK
N_MOD = 6
DEEPNORM_ALPHA = (2 * DEPTH) ** 0.25
LN_EPS = 1e-5

LANES = 128
SUBLANES = 8
VMEM_LIMIT_BYTES = 56 * 1024 * 1024
MOD_ROWS = 16
ROW_CHUNK = 256


def _params(*sem):
    return pltpu.CompilerParams(dimension_semantics=sem, vmem_limit_bytes=VMEM_LIMIT_BYTES)


def _layer_norm(y, g, b):
    mu = jnp.mean(y, -1, keepdims=True)
    var = jnp.mean(jnp.square(y - mu), -1, keepdims=True)
    return (y - mu) * lax.rsqrt(var + LN_EPS) * g + b


def _mod_body(c_ref, w_ref, b_ref, o_ref):
    c = c_ref[...]
    s = (c * jax.nn.sigmoid(c)).astype(BF16)
    o_ref[0] = jnp.dot(s, w_ref[0].astype(BF16), preferred_element_type=F32) + b_ref[0]


def _modulation(cvecs, w_mod, b_mod):
    n_out = N_MOD * D_MODEL
    tn = 1536
    return pl.pallas_call(
        _mod_body,
        grid=(DEPTH, n_out // tn),
        in_specs=[pl.BlockSpec((MOD_ROWS, D_MODEL), lambda l, j: (0, 0)),
                  pl.BlockSpec((1, D_MODEL, tn), lambda l, j: (l, 0, j)),
                  pl.BlockSpec((1, 1, tn), lambda l, j: (l, 0, j))],
        out_specs=pl.BlockSpec((1, MOD_ROWS, tn), lambda l, j: (l, 0, j)),
        out_shape=jax.ShapeDtypeStruct((DEPTH, MOD_ROWS, n_out), F32),
        compiler_params=_params("parallel", "parallel"),
        name="modulation",
    )(cvecs, w_mod, b_mod.reshape(DEPTH, 1, n_out))


def _add_pos_body(x_ref, p_ref, o_ref):
    o_ref[0] = x_ref[0] + p_ref[...]


def _add_pos(x, pos):
    b, t, d = x.shape
    tm = min(t, 512)
    return pl.pallas_call(
        _add_pos_body,
        grid=(b, t // tm),
        in_specs=[pl.BlockSpec((1, tm, d), lambda i, j: (i, j, 0)),
                  pl.BlockSpec((tm, d), lambda i, j: (j, 0))],
        out_specs=pl.BlockSpec((1, tm, d), lambda i, j: (i, j, 0)),
        out_shape=jax.ShapeDtypeStruct(x.shape, x.dtype),
        compiler_params=_params("parallel", "parallel"),
        name="add_pos",
    )(x, pos)


def _mix_in_body(x_ref, m_ref, w_ref, z_ref):
    m = m_ref[0]
    h = x_ref[0] * (1.0 + m[1:2]) + m[0:1]
    z_ref[0] = jnp.dot(h.astype(BF16), w_ref[...], preferred_element_type=F32)


def _mod_spec(mods):
    if mods.shape[0] == 1:
        return pl.BlockSpec((1, SUBLANES, D_MODEL), lambda i, j: (0, 0, 0))
    return pl.BlockSpec((1, SUBLANES, D_MODEL), lambda i, j: (i, 0, 0))


def _mix_in(x, mods, w_in):
    b, t, d = x.shape
    tm = min(t, 512)
    return pl.pallas_call(
        _mix_in_body,
        grid=(b, t // tm),
        in_specs=[pl.BlockSpec((1, tm, d), lambda i, j: (i, j, 0)),
                  _mod_spec(mods),
                  pl.BlockSpec((d, D_IN), lambda i, j: (0, 0))],
        out_specs=pl.BlockSpec((1, tm, D_IN), lambda i, j: (i, j, 0)),
        out_shape=jax.ShapeDtypeStruct((b, t, D_IN), F32),
        compiler_params=_params("parallel", "parallel"),
        name="mix_in",
    )(x, mods, w_in)


PAD = SUBLANES
UNROLL_GROUPS = 8


def _rnn_body(xr_ref, gr_ref, h0_ref, cw_ref, cb_ref, wa_ref, wx_ref, ba_ref, bx_ref, lam_ref,
              out_ref, st_ref, xpad, a_s, u_s):
    t_len = xr_ref.shape[1]
    rc = min(ROW_CHUNK, t_len)
    zeros = jnp.zeros((PAD, LANES), F32)
    xpad[0:PAD] = zeros
    xpad[t_len + PAD:t_len + 2 * PAD] = zeros
    xpad[PAD:t_len + PAD] = xr_ref[0]

    cw = cw_ref[...]
    lam = lam_ref[...]
    softplus_neg = jnp.maximum(-lam, 0.0) + jnp.log1p(jnp.exp(-jnp.abs(lam)))
    coef = -LRU_C * softplus_neg
    row = lax.broadcasted_iota(jnp.int32, (rc, LANES), 0) % SUBLANES
    left = CONV_W // 2

    for c in range(t_len // rc):
        t0 = c * rc
        xc = cb_ref[...] + xpad[t0 + PAD - left:t0 + PAD - left + rc] * cw[0:1]
        for k in range(1, CONV_W):
            xc = xc + xpad[t0 + PAD - left + k:t0 + PAD - left + k + rc] * cw[k:k + 1]
        xcb = xc.astype(BF16)
        for d in range(2):
            r = jax.nn.sigmoid(jnp.dot(xcb, wa_ref[d, 0], preferred_element_type=F32) + ba_ref[d:d + 1])
            gi = jax.nn.sigmoid(jnp.dot(xcb, wx_ref[d, 0], preferred_element_type=F32) + bx_ref[d:d + 1])
            log_a = coef[d:d + 1] * r
            a = jnp.exp(log_a)
            u = jnp.sqrt(-jnp.tanh(log_a) * (a * a + 1.0)) * (gi * xc)
            for s in (1, 2, 4):
                if d == 0:
                    keep = row >= s
                    shift = s
                else:
                    keep = row <= SUBLANES - 1 - s
                    shift = rc - s
                a_sh = jnp.where(keep, pltpu.roll(a, shift, 0), 1.0)
                u_sh = jnp.where(keep, pltpu.roll(u, shift, 0), 0.0)
                u = a * u_sh + u
                a = a * a_sh
            a_s[d, t0:t0 + rc] = a
            u_s[d, t0:t0 + rc] = u

    n_groups = t_len // SUBLANES
    n_iter = n_groups // UNROLL_GROUPS

    def carry_step(i, carry):
        hf, hb = carry
        for j in range(UNROLL_GROUPS):
            gf = pl.multiple_of((i * UNROLL_GROUPS + j) * SUBLANES, SUBLANES)
            gb = pl.multiple_of((n_groups - 1 - i * UNROLL_GROUPS - j) * SUBLANES, SUBLANES)
            h_f = u_s[0, pl.ds(gf, SUBLANES)] + a_s[0, pl.ds(gf, SUBLANES)] * hf
            h_b = u_s[1, pl.ds(gb, SUBLANES)] + a_s[1, pl.ds(gb, SUBLANES)] * hb
            u_s[0, pl.ds(gf, SUBLANES)] = h_f
            u_s[1, pl.ds(gb, SUBLANES)] = h_b
            hf = h_f[SUBLANES - 1:SUBLANES]
            hb = h_b[0:1]
        return hf, hb

    h0 = h0_ref[0]
    hf, hb = lax.fori_loop(0, n_iter, carry_step, (h0[0:1], h0[1:2]))
    st_ref[0, 0:1] = hf
    st_ref[0, 1:2] = hb

    for c in range(t_len // rc):
        t0 = c * rc
        hsum = u_s[0, t0:t0 + rc] + u_s[1, t0:t0 + rc]
        out_ref[0, t0:t0 + rc] = (hsum * jax.nn.gelu(gr_ref[0, t0:t0 + rc])).astype(BF16)


def _rnn(z, h0, conv_w, conv_b, wa, wx, b_a, b_x, lam):
    b, t, _ = z.shape
    nblk = D_RNN // LANES
    return pl.pallas_call(
        _rnn_body,
        grid=(b, nblk),
        in_specs=[pl.BlockSpec((1, t, LANES), lambda i, j: (i, 0, j)),
                  pl.BlockSpec((1, t, LANES), lambda i, j: (i, 0, nblk + j)),
                  pl.BlockSpec((1, 2, LANES), lambda i, j: (i, 0, j)),
                  pl.BlockSpec((CONV_W, LANES), lambda i, j: (0, j)),
                  pl.BlockSpec((1, LANES), lambda i, j: (0, j)),
                  pl.BlockSpec((2, 1, LANES, LANES), lambda i, j: (0, j, 0, 0)),
                  pl.BlockSpec((2, 1, LANES, LANES), lambda i, j: (0, j, 0, 0)),
                  pl.BlockSpec((2, LANES), lambda i, j: (0, j)),
                  pl.BlockSpec((2, LANES), lambda i, j: (0, j)),
                  pl.BlockSpec((2, LANES), lambda i, j: (0, j))],
        out_specs=[pl.BlockSpec((1, t, LANES), lambda i, j: (i, 0, j)),
                   pl.BlockSpec((1, 2, LANES), lambda i, j: (i, 0, j))],
        out_shape=[jax.ShapeDtypeStruct((b, t, D_RNN), BF16),
                   jax.ShapeDtypeStruct((b, 2, D_RNN), F32)],
        scratch_shapes=[pltpu.VMEM((t + 2 * PAD, LANES), F32),
                        pltpu.VMEM((2, t, LANES), F32),
                        pltpu.VMEM((2, t, LANES), F32)],
        compiler_params=_params("parallel", "parallel"),
        name="rglru",
    )(z, z, h0, conv_w, conv_b.reshape(1, D_RNN), wa, wx, b_a, b_x, lam)


def _block_diag_pairs(w):
    per = LANES // RNN_HEAD_DIM
    w = w.reshape(2, D_RNN // LANES, per, RNN_HEAD_DIM, RNN_HEAD_DIM)
    out = jnp.zeros((2, D_RNN // LANES, LANES, LANES), w.dtype)
    for p in range(per):
        sl = slice(p * RNN_HEAD_DIM, (p + 1) * RNN_HEAD_DIM)
        out = out.at[:, :, sl, sl].set(w[:, :, p])
    return out.astype(BF16)


POOL_PAD = 16


def _pool_body(xq_ref, w_ref, sc_ref, out_ref, ppad):
    t_len = xq_ref.shape[1]
    rc = min(ROW_CHUNK, t_len)
    zeros = jnp.zeros((POOL_PAD, D_POOL), F32)
    ppad[0:POOL_PAD] = zeros
    ppad[t_len + POOL_PAD:t_len + 2 * POOL_PAD] = zeros
    ppad[POOL_PAD:t_len + POOL_PAD] = xq_ref[0]
    for c in range(t_len // rc):
        t0 = c * rc
        tpos = t0 + lax.broadcasted_iota(jnp.int32, (rc, POOL_GROUP_DIM), 0)
        for g, w in enumerate(POOL_WINDOWS):
            cols = slice(g * POOL_GROUP_DIM, (g + 1) * POOL_GROUP_DIM)
            half = w // 2
            base = t0 + POOL_PAD - half
            s = ppad[base:base + rc, cols]
            for k in range(1, w):
                s = s + ppad[base + k:base + k + rc, cols]
            cnt = (jnp.minimum(tpos + half, t_len) - jnp.maximum(tpos - half, 0)).astype(F32)
            pooled = s / cnt - ppad[t0 + POOL_PAD:t0 + POOL_PAD + rc, cols]
            y = jnp.dot(pooled.astype(BF16), w_ref[g], preferred_element_type=F32)
            out_ref[0, t0:t0 + rc, cols] = (y * sc_ref[:, cols]).astype(BF16)


def _pool(z, w_pool, pool_scale):
    b, t, _ = z.shape
    return pl.pallas_call(
        _pool_body,
        grid=(b,),
        in_specs=[pl.BlockSpec((1, t, D_POOL), lambda i: (i, 0, 2 * D_RNN // D_POOL)),
                  pl.BlockSpec((len(POOL_WINDOWS), POOL_GROUP_DIM, POOL_GROUP_DIM), lambda i: (0, 0, 0)),
                  pl.BlockSpec((1, D_POOL), lambda i: (0, 0))],
        out_specs=pl.BlockSpec((1, t, D_POOL), lambda i: (i, 0, 0)),
        out_shape=jax.ShapeDtypeStruct((b, t, D_POOL), BF16),
        scratch_shapes=[pltpu.VMEM((t + 2 * POOL_PAD, D_POOL), F32)],
        compiler_params=_params("parallel"),
        name="pool",
    )(z, w_pool, pool_scale.reshape(1, D_POOL))


def _mix_out_body(rnn_ref, pool_ref, x_ref, m_ref, w_ref, g_ref, b_ref, x1_ref, hq_ref):
    m = m_ref[0]
    mix = (jnp.dot(rnn_ref[0], w_ref[0:D_RNN], preferred_element_type=F32)
           + jnp.dot(pool_ref[0], w_ref[D_RNN:D_RNN + D_POOL], preferred_element_type=F32))
    x1 = _layer_norm(DEEPNORM_ALPHA * x_ref[0] + m[2:3] * mix, g_ref[...], b_ref[...])
    x1_ref[0] = x1
    hq_ref[0] = (x1 * (1.0 + m[4:5]) + m[3:4]).astype(BF16)


def _mix_out(rnn, pool, x, mods, w_out, ln_g, ln_b):
    b, t, d = x.shape
    tm = min(t, 512)
    tok = lambda i, j: (i, j, 0)
    return pl.pallas_call(
        _mix_out_body,
        grid=(b, t // tm),
        in_specs=[pl.BlockSpec((1, tm, D_RNN), tok),
                  pl.BlockSpec((1, tm, D_POOL), tok),
                  pl.BlockSpec((1, tm, d), tok),
                  _mod_spec(mods),
                  pl.BlockSpec((D_RNN + D_POOL, d), lambda i, j: (0, 0)),
                  pl.BlockSpec((1, d), lambda i, j: (0, 0)),
                  pl.BlockSpec((1, d), lambda i, j: (0, 0))],
        out_specs=[pl.BlockSpec((1, tm, d), tok), pl.BlockSpec((1, tm, d), tok)],
        out_shape=[jax.ShapeDtypeStruct(x.shape, F32), jax.ShapeDtypeStruct(x.shape, BF16)],
        compiler_params=_params("parallel", "parallel"),
        name="mix_out",
    )(rnn, pool, x, mods, w_out, ln_g.reshape(1, d), ln_b.reshape(1, d))


NEG_INF = float("-inf")


BIG_ID = 2 ** 30


def _sort_network(n):
    def merge(lo, hi, r):
        step = r * 2
        if step < hi - lo:
            yield from merge(lo, hi, step)
            yield from merge(lo + r, hi, step)
            for i in range(lo + r, hi - r, step):
                yield (i, i + r)
        else:
            yield (lo, lo + r)

    def sort(lo, hi):
        if hi - lo >= 1:
            mid = lo + (hi - lo) // 2
            yield from sort(lo, mid)
            yield from sort(mid + 1, hi)
            yield from merge(lo, hi, 1)

    return list(sort(0, n - 1))


def _sorted_levels(s_t):
    n_lvl = s_t.shape[0] // SUBLANES
    sub = lax.broadcasted_iota(jnp.int32, (SUBLANES, s_t.shape[1]), 0)
    vals = [s_t[l * SUBLANES:(l + 1) * SUBLANES] for l in range(n_lvl)]
    ids = [sub + l * SUBLANES for l in range(n_lvl)]
    untouched = [True] * n_lvl
    for i, j in _sort_network(n_lvl):
        a, b, ia, ib = vals[i], vals[j], ids[i], ids[j]
        swap = b > a
        if not (untouched[i] and untouched[j]):
            swap = swap | ((b == a) & (ib < ia))
        vals[i], vals[j] = jnp.maximum(a, b), jnp.minimum(a, b)
        ids[i], ids[j] = jnp.where(swap, ib, ia), jnp.where(swap, ia, ib)
        untouched[i] = untouched[j] = False
    return vals, ids


def _pop_top(vals, ids, k):
    vals, ids = list(vals), list(ids)
    out_v, out_i = [], []
    for it in range(k):
        m = jnp.max(vals[0], axis=0, keepdims=True)
        pick = jnp.min(jnp.where(vals[0] == m, ids[0], BIG_ID), axis=0, keepdims=True)
        sel = ids[0] == pick
        out_v.append(m)
        out_i.append(pick)
        for l in range(min(len(vals) - 1, k - 1 - it)):
            vals[l] = jnp.where(sel, vals[l + 1], vals[l])
            ids[l] = jnp.where(sel, ids[l + 1], ids[l])
    return jnp.concatenate(out_v, axis=0), jnp.concatenate(out_i, axis=0)


def _pair_top(s1, i1, s2, i2):
    k = PEER_TOPK
    tm = s1.shape[1]
    sub = lax.broadcasted_iota(jnp.int32, (SUBLANES, tm), 0)
    s2lo, i2lo = s2[0:SUBLANES], i2[0:SUBLANES]
    cand, expert = [], []
    for l in range(k):
        ok = (sub + 1) * (l + 1) <= k
        cand.append(jnp.where(ok, s1[l:l + 1] + s2lo, NEG_INF))
        expert.append(i1[l:l + 1] * PEER_NKEYS + i2lo)
    single = s1[0:1] + s2[SUBLANES:k]
    single_expert = i1[0:1] * PEER_NKEYS + i2[SUBLANES:k]
    single_flat = sub + SUBLANES
    pops = jnp.zeros((SUBLANES, tm), jnp.int32)
    out_s, out_e = [], []
    for it in range(k):
        flat = pops * k + sub
        m = jnp.max(jnp.maximum(cand[0], single), axis=0, keepdims=True)
        pick = jnp.min(jnp.minimum(jnp.where(cand[0] == m, flat, BIG_ID),
                                   jnp.where(single == m, single_flat, BIG_ID)), axis=0, keepdims=True)
        sel = flat == pick
        sel_single = single_flat == pick
        out_s.append(m)
        out_e.append(jnp.sum(jnp.where(sel, expert[0], 0) + jnp.where(sel_single, single_expert, 0),
                             axis=0, keepdims=True))
        single = jnp.where(sel_single, NEG_INF, single)
        pops = jnp.where(sel, pops + 1, pops)
        for l in range(k - 1 - it):
            cand[l] = jnp.where(sel, cand[l + 1], cand[l])
            expert[l] = jnp.where(sel, expert[l + 1], expert[l])
    return jnp.concatenate(out_s, axis=0), jnp.concatenate(out_e, axis=0)


def _route_body(hq_ref, wq_ref, keys_ref, idx_ref, gate_ref):
    q = jnp.dot(hq_ref[...], wq_ref[...], preferred_element_type=F32)
    gates, experts = [], []
    for h in range(PEER_HEADS):
        tops = []
        for p in range(2):
            c0 = (h * 2 + p) * PEER_HALF
            qs = q[:, c0:c0 + PEER_HALF].astype(BF16)
            s_t = lax.dot_general(keys_ref[h, p], qs, (((1,), (1,)), ((), ())),
                                  preferred_element_type=F32)
            tops.append(_pop_top(*_sorted_levels(s_t), PEER_TOPK))
        (s1, i1), (s2, i2) = tops
        sc, picked = _pair_top(s1, i1, s2, i2)
        e = jnp.exp(sc - sc[0:1])
        gates.append(e / jnp.sum(e, axis=0, keepdims=True))
        experts.append(picked)
    gate_ref[...] = jnp.concatenate(gates, axis=0).T
    idx_ref[...] = jnp.concatenate(experts, axis=0).T


def _route(hq, wq, keys):
    n, d = hq.shape
    tm = 256
    return pl.pallas_call(
        _route_body,
        grid=(n // tm,),
        in_specs=[pl.BlockSpec((tm, d), lambda i: (i, 0)),
                  pl.BlockSpec((d, PEER_HEADS * PEER_QDIM), lambda i: (0, 0)),
                  pl.BlockSpec((PEER_HEADS, 2, PEER_NKEYS, PEER_HALF), lambda i: (0, 0, 0, 0))],
        out_specs=[pl.BlockSpec((tm, PEER_PICKS), lambda i: (i, 0)),
                   pl.BlockSpec((tm, PEER_PICKS), lambda i: (i, 0))],
        out_shape=[jax.ShapeDtypeStruct((n, PEER_PICKS), jnp.int32),
                   jax.ShapeDtypeStruct((n, PEER_PICKS), F32)],
        compiler_params=_params("parallel"),
        name="peer_route",
    )(hq, wq, keys)


def _final_body(x1_ref, f_ref, m_ref, g_ref, b_ref, o_ref):
    m = m_ref[0]
    o_ref[0] = _layer_norm(DEEPNORM_ALPHA * x1_ref[0] + m[5:6] * f_ref[0], g_ref[...], b_ref[...])


def _final(x1, ffn, mods, ln_g, ln_b):
    b, t, d = x1.shape
    tm = min(t, 512)
    tok = lambda i, j: (i, j, 0)
    return pl.pallas_call(
        _final_body,
        grid=(b, t // tm),
        in_specs=[pl.BlockSpec((1, tm, d), tok), pl.BlockSpec((1, tm, d), tok), _mod_spec(mods),
                  pl.BlockSpec((1, d), lambda i, j: (0, 0)), pl.BlockSpec((1, d), lambda i, j: (0, 0))],
        out_specs=pl.BlockSpec((1, tm, d), tok),
        out_shape=jax.ShapeDtypeStruct(x1.shape, F32),
        compiler_params=_params("parallel", "parallel"),
        name="final_ln",
    )(x1, ffn, mods, ln_g.reshape(1, d), ln_b.reshape(1, d))


N_EXPERTS = PEER_NKEYS * PEER_NKEYS
GATE_ROWS = 4
GATE_TOKENS = 32
PEER_TB = 1024
PEER_EC = 2048


def _gate_matrix(idx, gates):
    n = idx.shape[0]
    info = plsc.get_sparse_core_info()
    n_workers = info.num_cores * info.num_subcores
    lanes = info.num_lanes
    per_w = n // n_workers
    assert n % (n_workers * GATE_TOKENS) == 0 and GATE_TOKENS % GATE_ROWS == 0 and PEER_PICKS % lanes == 0
    vecs = PEER_PICKS // lanes
    mesh = plsc.VectorSubcoreMesh(core_axis_name="core", subcore_axis_name="subcore")

    @functools.partial(
        pl.kernel, out_type=jax.ShapeDtypeStruct((n, N_EXPERTS), F32), mesh=mesh,
        scratch_types=[pltpu.VMEM((GATE_TOKENS * PEER_PICKS,), jnp.int32),
                       pltpu.VMEM((GATE_TOKENS * PEER_PICKS,), F32)]
                      + [pltpu.VMEM((N_EXPERTS,), F32)] * GATE_ROWS
                      + [pltpu.SemaphoreType.DMA] * GATE_ROWS,
        compiler_params=pltpu.CompilerParams(needs_layout_passes=False), name="sc_gate_matrix")
    def run(idx_hbm, gate_hbm, out_hbm, idx_v, g_v, *bufs):
        rows, sems = bufs[:GATE_ROWS], bufs[GATE_ROWS:]
        wid = lax.axis_index("subcore") * info.num_cores + lax.axis_index("core")
        tok_base = wid * per_w
        zeros = jnp.zeros((lanes,), F32)

        @pl.loop(0, N_EXPERTS // lanes)
        def _(i):
            for r in rows:
                r[pl.ds(pl.multiple_of(i * lanes, lanes), lanes)] = zeros

        @pl.loop(0, per_w // GATE_TOKENS)
        def _(blk):
            tok0 = tok_base + blk * GATE_TOKENS
            k_at = pl.ds(pl.multiple_of(tok0 * PEER_PICKS, GATE_TOKENS * PEER_PICKS), GATE_TOKENS * PEER_PICKS)
            pltpu.sync_copy(idx_hbm.at[k_at], idx_v)
            pltpu.sync_copy(gate_hbm.at[k_at], g_v)

            @pl.loop(0, GATE_TOKENS // GATE_ROWS)
            def _(q):
                def picks(b, c):
                    return pl.ds(pl.multiple_of(((q * GATE_ROWS + b) * vecs + c) * lanes, lanes), lanes)

                for b in range(GATE_ROWS):
                    for c in range(vecs):
                        plsc.addupdate_scatter(rows[b], [idx_v[picks(b, c)]], g_v[picks(b, c)])
                    pltpu.make_async_copy(rows[b], out_hbm.at[tok0 + q * GATE_ROWS + b], sems[b]).start()
                for b in range(GATE_ROWS):
                    pltpu.make_async_copy(rows[b], out_hbm.at[tok0 + q * GATE_ROWS + b], sems[b]).wait()
                    for c in range(vecs):
                        plsc.store_scatter(rows[b], [idx_v[picks(b, c)]], zeros)

    return run(idx.reshape(-1), gates.reshape(-1))


def _dense_body(x_ref, u_ref, v_ref, g_ref, o_ref, acc):
    j = pl.program_id(1)

    @pl.when(j == 0)
    def _():
        acc[...] = jnp.zeros_like(acc)

    s = lax.dot_general(x_ref[...], u_ref[...], (((1,), (1,)), ((), ())), preferred_element_type=F32)
    g = g_ref[...]
    a = jnp.where(g != 0.0, jax.nn.gelu(s) * g, 0.0).astype(BF16)
    acc[...] += jnp.dot(a, v_ref[...], preferred_element_type=F32)

    @pl.when(j == pl.num_programs(1) - 1)
    def _():
        o_ref[...] = acc[...]


def _experts(hq, gate_mat, u_tab, v_tab):
    n, d = hq.shape
    tb = min(PEER_TB, n)
    return pl.pallas_call(
        _dense_body,
        grid=(n // tb, N_EXPERTS // PEER_EC),
        in_specs=[pl.BlockSpec((tb, d), lambda i, j: (i, 0)),
                  pl.BlockSpec((PEER_EC, d), lambda i, j: (j, 0)),
                  pl.BlockSpec((PEER_EC, d), lambda i, j: (j, 0)),
                  pl.BlockSpec((tb, PEER_EC), lambda i, j: (i, j))],
        out_specs=pl.BlockSpec((tb, d), lambda i, j: (i, 0)),
        out_shape=jax.ShapeDtypeStruct((n, d), F32),
        scratch_shapes=[pltpu.VMEM((tb, d), F32)],
        compiler_params=_params("parallel", "arbitrary"),
        name="peer_dense",
    )(hq, u_tab, v_tab, gate_mat)


def _grid_pos_embed(rows):
    t = jnp.arange(rows * GRID_W)
    r = (t // GRID_W).astype(F32)
    col = (t % GRID_W).astype(F32)
    n_freq = D_MODEL // 4
    omega = 1.0 / (10000.0 ** (jnp.arange(n_freq, dtype=F32) / n_freq))

    def enc(p):
        ang = p[:, None] * omega[None, :]
        return jnp.concatenate([jnp.sin(ang), jnp.cos(ang)], -1)
    return jnp.concatenate([enc(r), enc(col)], -1)


def _layer(x, mods, h0, p):
    b, t, d = x.shape
    z = _mix_in(x, mods, p["w_in"])
    rnn, st = _rnn(z, h0, p["conv_w"], p["conv_b"], p["wa"], p["wx"], p["b_a"], p["b_x"], p["lam"])
    pool = _pool(z, p["w_pool"], p["pool_scale"])
    x1, hq = _mix_out(rnn, pool, x, mods, p["w_out"], p["ln1_g"], p["ln1_b"])
    hq = hq.reshape(b * t, d)
    idx, gates = _route(hq, p["wq"], p["keys"])
    gate_mat = _gate_matrix(idx, gates)
    ffn = _experts(hq, gate_mat, p["peer_u"], p["peer_v"])
    x2 = _final(x1, ffn.reshape(b, t, d), mods, p["ln2_g"], p["ln2_b"])
    return x2, st


def kernel(x_prompt, x_sample, state_rglru, c, c_ctx, w_mod, b_mod, w_in, conv_w, conv_b,
           w_rg_a, b_rg_a, w_rg_x, b_rg_x, lru_lambda, w_pool, pool_scale, w_out,
           ln1_g, ln1_b, ln2_g, ln2_b, peer_wq, peer_keys, peer_u, peer_v):
    n_req = c.shape[0]
    cvecs = jnp.zeros((MOD_ROWS, D_MODEL), F32).at[:n_req].set(c).at[n_req].set(c_ctx)
    mod = _modulation(cvecs, w_mod, b_mod).reshape(DEPTH, MOD_ROWS, N_MOD, D_MODEL)
    mod = jnp.pad(mod, ((0, 0), (0, 0), (0, SUBLANES - N_MOD), (0, 0)))

    rows = x_sample.shape[1] // GRID_W
    xs = _add_pos(x_sample, _grid_pos_embed(rows))
    xp = x_prompt
    zero_state = jnp.zeros((x_prompt.shape[0], 2, D_RNN), F32)
    ctx_states = []
    for l in range(DEPTH):
        p = dict(w_in=w_in[l].astype(BF16), conv_w=conv_w[l], conv_b=conv_b[l],
                 wa=_block_diag_pairs(w_rg_a[l]), wx=_block_diag_pairs(w_rg_x[l]),
                 b_a=b_rg_a[l], b_x=b_rg_x[l], lam=lru_lambda[l],
                 w_pool=w_pool[l].astype(BF16), pool_scale=pool_scale[l], w_out=w_out[l].astype(BF16),
                 ln1_g=ln1_g[l], ln1_b=ln1_b[l], ln2_g=ln2_g[l], ln2_b=ln2_b[l],
                 wq=peer_wq[l].astype(BF16), keys=peer_keys[l].astype(BF16),
                 peer_u=peer_u[l].astype(BF16), peer_v=peer_v[l].astype(BF16))
        for stream in ("ctx", "latent") if l % 2 == 0 else ("latent", "ctx"):
            if stream == "ctx":
                xp, st = _layer(xp, mod[l, n_req:n_req + 1], zero_state, p)
                ctx_states.append(st)
            else:
                xs, _ = _layer(xs, mod[l, :n_req], state_rglru[:, l], p)
    return xp, xs, jnp.stack(ctx_states, axis=1)
```

```python
import functools
import math

import jax
import jax.numpy as jnp
from jax import lax
from jax.experimental import pallas as pl
from jax.experimental.pallas import tpu as pltpu
from jax.experimental.pallas import tpu_sc as plsc

F32 = jnp.float32
BF16 = jnp.bfloat16

D_MODEL = 1024
DEPTH = 2
GRID_W = 64
D_RNN = 512
N_RNN_HEADS = 8
RNN_HEAD_DIM = D_RNN // N_RNN_HEADS
CONV_W = 4
LRU_C = 8.0
D_POOL = 512
POOL_WINDOWS = (2, 4, 8, 16)
POOL_GROUP_DIM = D_POOL // len(POOL_WINDOWS)
D_IN = 2 * D_RNN + D_POOL
PEER_HEADS = 8
PEER_NKEYS = 128
PEER_TOPK = 16
PEER_QDIM = 256
PEER_HALF = PEER_QDIM // 2
PEER_PICKS = PEER_HEADS * PEER_TOPK
N_MOD = 6
DEEPNORM_ALPHA = (2 * DEPTH) ** 0.25
LN_EPS = 1e-5

LANES = 128
SUBLANES = 8
VMEM_LIMIT_BYTES = 56 * 1024 * 1024
MOD_ROWS = 16
ROW_CHUNK = 256


def _params(*sem):
    return pltpu.CompilerParams(dimension_semantics=sem, vmem_limit_bytes=VMEM_LIMIT_BYTES)


def _layer_norm(y, g, b):
    mu = jnp.mean(y, -1, keepdims=True)
    var = jnp.mean(jnp.square(y - mu), -1, keepdims=True)
    return (y - mu) * lax.rsqrt(var + LN_EPS) * g + b


def _mod_body(c_ref, w_ref, b_ref, o_ref):
    c = c_ref[...]
    s = (c * jax.nn.sigmoid(c)).astype(BF16)
    o_ref[0] = jnp.dot(s, w_ref[0].astype(BF16), preferred_element_type=F32) + b_ref[0]


def _modulation(cvecs, w_mod, b_mod):
    n_out = N_MOD * D_MODEL
    tn = 1536
    return pl.pallas_call(
        _mod_body,
        grid=(DEPTH, n_out // tn),
        in_specs=[pl.BlockSpec((MOD_ROWS, D_MODEL), lambda l, j: (0, 0)),
                  pl.BlockSpec((1, D_MODEL, tn), lambda l, j: (l, 0, j)),
                  pl.BlockSpec((1, 1, tn), lambda l, j: (l, 0, j))],
        out_specs=pl.BlockSpec((1, MOD_ROWS, tn), lambda l, j: (l, 0, j)),
        out_shape=jax.ShapeDtypeStruct((DEPTH, MOD_ROWS, n_out), F32),
        compiler_params=_params("parallel", "parallel"),
        name="modulation",
    )(cvecs, w_mod, b_mod.reshape(DEPTH, 1, n_out))


def _add_pos_body(x_ref, p_ref, o_ref):
    o_ref[0] = x_ref[0] + p_ref[...]


def _add_pos(x, pos):
    b, t, d = x.shape
    tm = min(t, 512)
    return pl.pallas_call(
        _add_pos_body,
        grid=(b, t // tm),
        in_specs=[pl.BlockSpec((1, tm, d), lambda i, j: (i, j, 0)),
                  pl.BlockSpec((tm, d), lambda i, j: (j, 0))],
        out_specs=pl.BlockSpec((1, tm, d), lambda i, j: (i, j, 0)),
        out_shape=jax.ShapeDtypeStruct(x.shape, x.dtype),
        compiler_params=_params("parallel", "parallel"),
        name="add_pos",
    )(x, pos)


def _mix_in_body(x_ref, m_ref, w_ref, z_ref):
    m = m_ref[0]
    h = x_ref[0] * (1.0 + m[1:2]) + m[0:1]
    z_ref[0] = jnp.dot(h.astype(BF16), w_ref[...], preferred_element_type=F32)


def _mod_spec(mods):
    if mods.shape[0] == 1:
        return pl.BlockSpec((1, SUBLANES, D_MODEL), lambda i, j: (0, 0, 0))
    return pl.BlockSpec((1, SUBLANES, D_MODEL), lambda i, j: (i, 0, 0))


def _mix_in(x, mods, w_in):
    b, t, d = x.shape
    tm = min(t, 512)
    return pl.pallas_call(
        _mix_in_body,
        grid=(b, t // tm),
        in_specs=[pl.BlockSpec((1, tm, d), lambda i, j: (i, j, 0)),
                  _mod_spec(mods),
                  pl.BlockSpec((d, D_IN), lambda i, j: (0, 0))],
        out_specs=pl.BlockSpec((1, tm, D_IN), lambda i, j: (i, j, 0)),
        out_shape=jax.ShapeDtypeStruct((b, t, D_IN), F32),
        compiler_params=_params("parallel", "parallel"),
        name="mix_in",
    )(x, mods, w_in)


PAD = SUBLANES
UNROLL_GROUPS = 8


def _rnn_body(xr_ref, gr_ref, h0_ref, cw_ref, cb_ref, wa_ref, wx_ref, ba_ref, bx_ref, lam_ref,
              out_ref, st_ref, xpad, a_s, u_s):
    t_len = xr_ref.shape[1]
    rc = min(ROW_CHUNK, t_len)
    zeros = jnp.zeros((PAD, LANES), F32)
    xpad[0:PAD] = zeros
    xpad[t_len + PAD:t_len + 2 * PAD] = zeros
    xpad[PAD:t_len + PAD] = xr_ref[0]

    cw = cw_ref[...]
    lam = lam_ref[...]
    softplus_neg = jnp.maximum(-lam, 0.0) + jnp.log1p(jnp.exp(-jnp.abs(lam)))
    coef = -LRU_C * softplus_neg
    row = lax.broadcasted_iota(jnp.int32, (rc // SUBLANES, SUBLANES, LANES), 1)
    left = CONV_W // 2

    for c in range(t_len // rc):
        t0 = c * rc
        xc = cb_ref[...] + xpad[t0 + PAD - left:t0 + PAD - left + rc] * cw[0:1]
        for k in range(1, CONV_W):
            xc = xc + xpad[t0 + PAD - left + k:t0 + PAD - left + k + rc] * cw[k:k + 1]
        xcb = xc.astype(BF16)
        for d in range(2):
            r = jax.nn.sigmoid(jnp.dot(xcb, wa_ref[d, 0], preferred_element_type=F32) + ba_ref[d:d + 1])
            gi = jax.nn.sigmoid(jnp.dot(xcb, wx_ref[d, 0], preferred_element_type=F32) + bx_ref[d:d + 1])
            log_a = coef[d:d + 1] * r
            a = jnp.exp(log_a)
            u = jnp.sqrt(-jnp.tanh(log_a) * (a * a + 1.0)) * (gi * xc)
            a = a.reshape(rc // SUBLANES, SUBLANES, LANES)
            u = u.reshape(rc // SUBLANES, SUBLANES, LANES)
            for s in (1, 2, 4):
                if d == 0:
                    keep = row >= s
                    shift = s
                else:
                    keep = row <= SUBLANES - 1 - s
                    shift = SUBLANES - s
                a_sh = jnp.where(keep, pltpu.roll(a, shift, 1), 1.0)
                u_sh = jnp.where(keep, pltpu.roll(u, shift, 1), 0.0)
                u = a * u_sh + u
                a = a * a_sh
            a_s[d, t0:t0 + rc] = a.reshape(rc, LANES)
            u_s[d, t0:t0 + rc] = u.reshape(rc, LANES)

    n_groups = t_len // SUBLANES
    n_iter = n_groups // UNROLL_GROUPS

    def carry_step(i, carry):
        hf, hb = carry
        for j in range(UNROLL_GROUPS):
            gf = pl.multiple_of((i * UNROLL_GROUPS + j) * SUBLANES, SUBLANES)
            gb = pl.multiple_of((n_groups - 1 - i * UNROLL_GROUPS - j) * SUBLANES, SUBLANES)
            h_f = u_s[0, pl.ds(gf, SUBLANES)] + a_s[0, pl.ds(gf, SUBLANES)] * hf
            h_b = u_s[1, pl.ds(gb, SUBLANES)] + a_s[1, pl.ds(gb, SUBLANES)] * hb
            u_s[0, pl.ds(gf, SUBLANES)] = h_f
            u_s[1, pl.ds(gb, SUBLANES)] = h_b
            hf = h_f[SUBLANES - 1:SUBLANES]
            hb = h_b[0:1]
        return hf, hb

    h0 = h0_ref[0]
    hf, hb = lax.fori_loop(0, n_iter, carry_step, (h0[0:1], h0[1:2]))
    st_ref[0, 0:1] = hf
    st_ref[0, 1:2] = hb

    for c in range(t_len // rc):
        t0 = c * rc
        hsum = u_s[0, t0:t0 + rc] + u_s[1, t0:t0 + rc]
        out_ref[0, t0:t0 + rc] = (hsum * jax.nn.gelu(gr_ref[0, t0:t0 + rc])).astype(BF16)


def _rnn(z, h0, conv_w, conv_b, wa, wx, b_a, b_x, lam):
    b, t, _ = z.shape
    nblk = D_RNN // LANES
    return pl.pallas_call(
        _rnn_body,
        grid=(b, nblk),
        in_specs=[pl.BlockSpec((1, t, LANES), lambda i, j: (i, 0, j)),
                  pl.BlockSpec((1, t, LANES), lambda i, j: (i, 0, nblk + j)),
                  pl.BlockSpec((1, 2, LANES), lambda i, j: (i, 0, j)),
                  pl.BlockSpec((CONV_W, LANES), lambda i, j: (0, j)),
                  pl.BlockSpec((1, LANES), lambda i, j: (0, j)),
                  pl.BlockSpec((2, 1, LANES, LANES), lambda i, j: (0, j, 0, 0)),
                  pl.BlockSpec((2, 1, LANES, LANES), lambda i, j: (0, j, 0, 0)),
                  pl.BlockSpec((2, LANES), lambda i, j: (0, j)),
                  pl.BlockSpec((2, LANES), lambda i, j: (0, j)),
                  pl.BlockSpec((2, LANES), lambda i, j: (0, j))],
        out_specs=[pl.BlockSpec((1, t, LANES), lambda i, j: (i, 0, j)),
                   pl.BlockSpec((1, 2, LANES), lambda i, j: (i, 0, j))],
        out_shape=[jax.ShapeDtypeStruct((b, t, D_RNN), BF16),
                   jax.ShapeDtypeStruct((b, 2, D_RNN), F32)],
        scratch_shapes=[pltpu.VMEM((t + 2 * PAD, LANES), F32),
                        pltpu.VMEM((2, t, LANES), F32),
                        pltpu.VMEM((2, t, LANES), F32)],
        compiler_params=_params("parallel", "parallel"),
        name="rglru",
    )(z, z, h0, conv_w, conv_b.reshape(1, D_RNN), wa, wx, b_a, b_x, lam)


def _block_diag_pairs(w):
    per = LANES // RNN_HEAD_DIM
    w = w.reshape(2, D_RNN // LANES, per, RNN_HEAD_DIM, RNN_HEAD_DIM)
    out = jnp.zeros((2, D_RNN // LANES, LANES, LANES), w.dtype)
    for p in range(per):
        sl = slice(p * RNN_HEAD_DIM, (p + 1) * RNN_HEAD_DIM)
        out = out.at[:, :, sl, sl].set(w[:, :, p])
    return out.astype(BF16)


POOL_PAD = 16


def _pool_body(xq_ref, w_ref, sc_ref, out_ref, ppad):
    t_len = xq_ref.shape[1]
    rc = min(ROW_CHUNK, t_len)
    zeros = jnp.zeros((POOL_PAD, D_POOL), F32)
    ppad[0:POOL_PAD] = zeros
    ppad[t_len + POOL_PAD:t_len + 2 * POOL_PAD] = zeros
    ppad[POOL_PAD:t_len + POOL_PAD] = xq_ref[0]
    for c in range(t_len // rc):
        t0 = c * rc
        tpos = t0 + lax.broadcasted_iota(jnp.int32, (rc, POOL_GROUP_DIM), 0)
        for g, w in enumerate(POOL_WINDOWS):
            cols = slice(g * POOL_GROUP_DIM, (g + 1) * POOL_GROUP_DIM)
            half = w // 2
            base = t0 + POOL_PAD - half
            s = ppad[base:base + rc, cols]
            for k in range(1, w):
                s = s + ppad[base + k:base + k + rc, cols]
            cnt = (jnp.minimum(tpos + half, t_len) - jnp.maximum(tpos - half, 0)).astype(F32)
            pooled = s / cnt - ppad[t0 + POOL_PAD:t0 + POOL_PAD + rc, cols]
            y = jnp.dot(pooled.astype(BF16), w_ref[g], preferred_element_type=F32)
            out_ref[0, t0:t0 + rc, cols] = (y * sc_ref[:, cols]).astype(BF16)


def _pool(z, w_pool, pool_scale):
    b, t, _ = z.shape
    return pl.pallas_call(
        _pool_body,
        grid=(b,),
        in_specs=[pl.BlockSpec((1, t, D_POOL), lambda i: (i, 0, 2 * D_RNN // D_POOL)),
                  pl.BlockSpec((len(POOL_WINDOWS), POOL_GROUP_DIM, POOL_GROUP_DIM), lambda i: (0, 0, 0)),
                  pl.BlockSpec((1, D_POOL), lambda i: (0, 0))],
        out_specs=pl.BlockSpec((1, t, D_POOL), lambda i: (i, 0, 0)),
        out_shape=jax.ShapeDtypeStruct((b, t, D_POOL), BF16),
        scratch_shapes=[pltpu.VMEM((t + 2 * POOL_PAD, D_POOL), F32)],
        compiler_params=_params("parallel"),
        name="pool",
    )(z, w_pool, pool_scale.reshape(1, D_POOL))


def _mix_out_body(rnn_ref, pool_ref, x_ref, m_ref, w_ref, g_ref, b_ref, x1_ref, hq_ref):
    m = m_ref[0]
    mix = (jnp.dot(rnn_ref[0], w_ref[0:D_RNN], preferred_element_type=F32)
           + jnp.dot(pool_ref[0], w_ref[D_RNN:D_RNN + D_POOL], preferred_element_type=F32))
    x1 = _layer_norm(DEEPNORM_ALPHA * x_ref[0] + m[2:3] * mix, g_ref[...], b_ref[...])
    x1_ref[0] = x1
    hq_ref[0] = (x1 * (1.0 + m[4:5]) + m[3:4]).astype(BF16)


def _mix_out(rnn, pool, x, mods, w_out, ln_g, ln_b):
    b, t, d = x.shape
    tm = min(t, 512)
    tok = lambda i, j: (i, j, 0)
    return pl.pallas_call(
        _mix_out_body,
        grid=(b, t // tm),
        in_specs=[pl.BlockSpec((1, tm, D_RNN), tok),
                  pl.BlockSpec((1, tm, D_POOL), tok),
                  pl.BlockSpec((1, tm, d), tok),
                  _mod_spec(mods),
                  pl.BlockSpec((D_RNN + D_POOL, d), lambda i, j: (0, 0)),
                  pl.BlockSpec((1, d), lambda i, j: (0, 0)),
                  pl.BlockSpec((1, d), lambda i, j: (0, 0))],
        out_specs=[pl.BlockSpec((1, tm, d), tok), pl.BlockSpec((1, tm, d), tok)],
        out_shape=[jax.ShapeDtypeStruct(x.shape, F32), jax.ShapeDtypeStruct(x.shape, BF16)],
        compiler_params=_params("parallel", "parallel"),
        name="mix_out",
    )(rnn, pool, x, mods, w_out, ln_g.reshape(1, d), ln_b.reshape(1, d))


NEG_INF = float("-inf")


BIG_ID = 2 ** 30


def _sort_network(n):
    def merge(lo, hi, r):
        step = r * 2
        if step < hi - lo:
            yield from merge(lo, hi, step)
            yield from merge(lo + r, hi, step)
            for i in range(lo + r, hi - r, step):
                yield (i, i + r)
        else:
            yield (lo, lo + r)

    def sort(lo, hi):
        if hi - lo >= 1:
            mid = lo + (hi - lo) // 2
            yield from sort(lo, mid)
            yield from sort(mid + 1, hi)
            yield from merge(lo, hi, 1)

    return list(sort(0, n - 1))


def _sorted_levels(s_t):
    n_lvl = s_t.shape[0] // SUBLANES
    sub = lax.broadcasted_iota(jnp.int32, (SUBLANES, s_t.shape[1]), 0)
    vals = [s_t[l * SUBLANES:(l + 1) * SUBLANES] for l in range(n_lvl)]
    ids = [sub + l * SUBLANES for l in range(n_lvl)]
    untouched = [True] * n_lvl
    for i, j in _sort_network(n_lvl):
        a, b, ia, ib = vals[i], vals[j], ids[i], ids[j]
        swap = b > a
        if not (untouched[i] and untouched[j]):
            swap = swap | ((b == a) & (ib < ia))
        vals[i], vals[j] = jnp.maximum(a, b), jnp.minimum(a, b)
        ids[i], ids[j] = jnp.where(swap, ib, ia), jnp.where(swap, ia, ib)
        untouched[i] = untouched[j] = False
    return vals, ids


def _pop_top(vals, ids, k):
    vals, ids = list(vals), list(ids)
    out_v, out_i = [], []
    for it in range(k):
        m = jnp.max(vals[0], axis=0, keepdims=True)
        pick = jnp.min(jnp.where(vals[0] == m, ids[0], BIG_ID), axis=0, keepdims=True)
        sel = ids[0] == pick
        out_v.append(m)
        out_i.append(pick)
        for l in range(min(len(vals) - 1, k - 1 - it)):
            vals[l] = jnp.where(sel, vals[l + 1], vals[l])
            ids[l] = jnp.where(sel, ids[l + 1], ids[l])
    return jnp.concatenate(out_v, axis=0), jnp.concatenate(out_i, axis=0)


def _pair_top(s1, i1, s2, i2):
    k = PEER_TOPK
    tm = s1.shape[1]
    sub = lax.broadcasted_iota(jnp.int32, (SUBLANES, tm), 0)
    s2lo, i2lo = s2[0:SUBLANES], i2[0:SUBLANES]
    cand, expert = [], []
    for l in range(k):
        ok = (sub + 1) * (l + 1) <= k
        cand.append(jnp.where(ok, s1[l:l + 1] + s2lo, NEG_INF))
        expert.append(i1[l:l + 1] * PEER_NKEYS + i2lo)
    single = s1[0:1] + s2[SUBLANES:k]
    single_expert = i1[0:1] * PEER_NKEYS + i2[SUBLANES:k]
    single_flat = sub + SUBLANES
    pops = jnp.zeros((SUBLANES, tm), jnp.int32)
    out_s, out_e = [], []
    for it in range(k):
        flat = pops * k + sub
        m = jnp.max(jnp.maximum(cand[0], single), axis=0, keepdims=True)
        pick = jnp.min(jnp.minimum(jnp.where(cand[0] == m, flat, BIG_ID),
                                   jnp.where(single == m, single_flat, BIG_ID)), axis=0, keepdims=True)
        sel = flat == pick
        sel_single = single_flat == pick
        out_s.append(m)
        out_e.append(jnp.sum(jnp.where(sel, expert[0], 0) + jnp.where(sel_single, single_expert, 0),
                             axis=0, keepdims=True))
        single = jnp.where(sel_single, NEG_INF, single)
        pops = jnp.where(sel, pops + 1, pops)
        for l in range(k - 1 - it):
            cand[l] = jnp.where(sel, cand[l + 1], cand[l])
            expert[l] = jnp.where(sel, expert[l + 1], expert[l])
    return jnp.concatenate(out_s, axis=0), jnp.concatenate(out_e, axis=0)


def _route_body(hq_ref, wq_ref, keys_ref, idx_ref, gate_ref):
    q = jnp.dot(hq_ref[...], wq_ref[...], preferred_element_type=F32)
    gates, experts = [], []
    for h in range(PEER_HEADS):
        tops = []
        for p in range(2):
            c0 = (h * 2 + p) * PEER_HALF
            qs = q[:, c0:c0 + PEER_HALF].astype(BF16)
            s_t = lax.dot_general(keys_ref[h, p], qs, (((1,), (1,)), ((), ())),
                                  preferred_element_type=F32)
            tops.append(_pop_top(*_sorted_levels(s_t), PEER_TOPK))
        (s1, i1), (s2, i2) = tops
        sc, picked = _pair_top(s1, i1, s2, i2)
        e = jnp.exp(sc - sc[0:1])
        gates.append(e / jnp.sum(e, axis=0, keepdims=True))
        experts.append(picked)
    gate_ref[...] = jnp.concatenate(gates, axis=0).T
    idx_ref[...] = jnp.concatenate(experts, axis=0).T


def _route(hq, wq, keys):
    n, d = hq.shape
    tm = 256
    return pl.pallas_call(
        _route_body,
        grid=(n // tm,),
        in_specs=[pl.BlockSpec((tm, d), lambda i: (i, 0)),
                  pl.BlockSpec((d, PEER_HEADS * PEER_QDIM), lambda i: (0, 0)),
                  pl.BlockSpec((PEER_HEADS, 2, PEER_NKEYS, PEER_HALF), lambda i: (0, 0, 0, 0))],
        out_specs=[pl.BlockSpec((tm, PEER_PICKS), lambda i: (i, 0)),
                   pl.BlockSpec((tm, PEER_PICKS), lambda i: (i, 0))],
        out_shape=[jax.ShapeDtypeStruct((n, PEER_PICKS), jnp.int32),
                   jax.ShapeDtypeStruct((n, PEER_PICKS), F32)],
        compiler_params=_params("parallel"),
        name="peer_route",
    )(hq, wq, keys)


def _final_body(x1_ref, f_ref, m_ref, g_ref, b_ref, o_ref):
    m = m_ref[0]
    o_ref[0] = _layer_norm(DEEPNORM_ALPHA * x1_ref[0] + m[5:6] * f_ref[0], g_ref[...], b_ref[...])


def _final(x1, ffn, mods, ln_g, ln_b):
    b, t, d = x1.shape
    tm = min(t, 512)
    tok = lambda i, j: (i, j, 0)
    return pl.pallas_call(
        _final_body,
        grid=(b, t // tm),
        in_specs=[pl.BlockSpec((1, tm, d), tok), pl.BlockSpec((1, tm, d), tok), _mod_spec(mods),
                  pl.BlockSpec((1, d), lambda i, j: (0, 0)), pl.BlockSpec((1, d), lambda i, j: (0, 0))],
        out_specs=pl.BlockSpec((1, tm, d), tok),
        out_shape=jax.ShapeDtypeStruct(x1.shape, F32),
        compiler_params=_params("parallel", "parallel"),
        name="final_ln",
    )(x1, ffn, mods, ln_g.reshape(1, d), ln_b.reshape(1, d))


N_EXPERTS = PEER_NKEYS * PEER_NKEYS
GATE_ROWS = 4
GATE_TOKENS = 32
PEER_TB = 1024
PEER_EC = 2048


def _gate_matrix(idx, gates):
    n = idx.shape[0]
    info = plsc.get_sparse_core_info()
    n_workers = info.num_cores * info.num_subcores
    lanes = info.num_lanes
    per_w = n // n_workers
    assert n % (n_workers * GATE_TOKENS) == 0 and GATE_TOKENS % GATE_ROWS == 0 and PEER_PICKS % lanes == 0
    vecs = PEER_PICKS // lanes
    mesh = plsc.VectorSubcoreMesh(core_axis_name="core", subcore_axis_name="subcore")

    @functools.partial(
        pl.kernel, out_type=jax.ShapeDtypeStruct((n, N_EXPERTS), F32), mesh=mesh,
        scratch_types=[pltpu.VMEM((GATE_TOKENS * PEER_PICKS,), jnp.int32),
                       pltpu.VMEM((GATE_TOKENS * PEER_PICKS,), F32)]
                      + [pltpu.VMEM((N_EXPERTS,), F32)] * GATE_ROWS
                      + [pltpu.SemaphoreType.DMA] * GATE_ROWS,
        compiler_params=pltpu.CompilerParams(needs_layout_passes=False), name="sc_gate_matrix")
    def run(idx_hbm, gate_hbm, out_hbm, idx_v, g_v, *bufs):
        rows, sems = bufs[:GATE_ROWS], bufs[GATE_ROWS:]
        wid = lax.axis_index("subcore") * info.num_cores + lax.axis_index("core")
        tok_base = wid * per_w
        zeros = jnp.zeros((lanes,), F32)

        @pl.loop(0, N_EXPERTS // lanes)
        def _(i):
            for r in rows:
                r[pl.ds(pl.multiple_of(i * lanes, lanes), lanes)] = zeros

        @pl.loop(0, per_w // GATE_TOKENS)
        def _(blk):
            tok0 = tok_base + blk * GATE_TOKENS
            k_at = pl.ds(pl.multiple_of(tok0 * PEER_PICKS, GATE_TOKENS * PEER_PICKS), GATE_TOKENS * PEER_PICKS)
            pltpu.sync_copy(idx_hbm.at[k_at], idx_v)
            pltpu.sync_copy(gate_hbm.at[k_at], g_v)

            @pl.loop(0, GATE_TOKENS // GATE_ROWS)
            def _(q):
                def picks(b, c):
                    return pl.ds(pl.multiple_of(((q * GATE_ROWS + b) * vecs + c) * lanes, lanes), lanes)

                for b in range(GATE_ROWS):
                    for c in range(vecs):
                        plsc.addupdate_scatter(rows[b], [idx_v[picks(b, c)]], g_v[picks(b, c)])
                    pltpu.make_async_copy(rows[b], out_hbm.at[tok0 + q * GATE_ROWS + b], sems[b]).start()
                for b in range(GATE_ROWS):
                    pltpu.make_async_copy(rows[b], out_hbm.at[tok0 + q * GATE_ROWS + b], sems[b]).wait()
                    for c in range(vecs):
                        plsc.store_scatter(rows[b], [idx_v[picks(b, c)]], zeros)

    return run(idx.reshape(-1), gates.reshape(-1))


def _dense_body(x_ref, u_ref, v_ref, g_ref, o_ref, acc):
    j = pl.program_id(1)

    @pl.when(j == 0)
    def _():
        acc[...] = jnp.zeros_like(acc)

    s = lax.dot_general(x_ref[...], u_ref[...], (((1,), (1,)), ((), ())), preferred_element_type=F32)
    g = g_ref[...]
    a = jnp.where(g != 0.0, jax.nn.gelu(s) * g, 0.0).astype(BF16)
    acc[...] += jnp.dot(a, v_ref[...], preferred_element_type=F32)

    @pl.when(j == pl.num_programs(1) - 1)
    def _():
        o_ref[...] = acc[...]


def _experts(hq, gate_mat, u_tab, v_tab, layer):
    n, d = hq.shape
    tb = min(PEER_TB, n)
    return pl.pallas_call(
        _dense_body,
        grid=(n // tb, N_EXPERTS // PEER_EC),
        in_specs=[pl.BlockSpec((tb, d), lambda i, j: (i, 0)),
                  pl.BlockSpec((None, PEER_EC, d), lambda i, j: (layer, j, 0)),
                  pl.BlockSpec((None, PEER_EC, d), lambda i, j: (layer, j, 0)),
                  pl.BlockSpec((tb, PEER_EC), lambda i, j: (i, j))],
        out_specs=pl.BlockSpec((tb, d), lambda i, j: (i, 0)),
        out_shape=jax.ShapeDtypeStruct((n, d), F32),
        scratch_shapes=[pltpu.VMEM((tb, d), F32)],
        compiler_params=_params("parallel", "arbitrary"),
        name="peer_dense",
    )(hq, u_tab, v_tab, gate_mat)


def _grid_pos_embed(rows):
    t = jnp.arange(rows * GRID_W)
    r = (t // GRID_W).astype(F32)
    col = (t % GRID_W).astype(F32)
    n_freq = D_MODEL // 4
    omega = 1.0 / (10000.0 ** (jnp.arange(n_freq, dtype=F32) / n_freq))

    def enc(p):
        ang = p[:, None] * omega[None, :]
        return jnp.concatenate([jnp.sin(ang), jnp.cos(ang)], -1)
    return jnp.concatenate([enc(r), enc(col)], -1)


def _layer(x, mods, h0, p):
    b, t, d = x.shape
    z = _mix_in(x, mods, p["w_in"])
    rnn, st = _rnn(z, h0, p["conv_w"], p["conv_b"], p["wa"], p["wx"], p["b_a"], p["b_x"], p["lam"])
    pool = _pool(z, p["w_pool"], p["pool_scale"])
    x1, hq = _mix_out(rnn, pool, x, mods, p["w_out"], p["ln1_g"], p["ln1_b"])
    hq = hq.reshape(b * t, d)
    idx, gates = _route(hq, p["wq"], p["keys"])
    gate_mat = _gate_matrix(idx, gates)
    ffn = _experts(hq, gate_mat, p["peer_u"], p["peer_v"], p["layer"])
    x2 = _final(x1, ffn.reshape(b, t, d), mods, p["ln2_g"], p["ln2_b"])
    return x2, st


def kernel(x_prompt, x_sample, state_rglru, c, c_ctx, w_mod, b_mod, w_in, conv_w, conv_b,
           w_rg_a, b_rg_a, w_rg_x, b_rg_x, lru_lambda, w_pool, pool_scale, w_out,
           ln1_g, ln1_b, ln2_g, ln2_b, peer_wq, peer_keys, peer_u, peer_v):
    n_req = c.shape[0]
    cvecs = jnp.zeros((MOD_ROWS, D_MODEL), F32).at[:n_req].set(c).at[n_req].set(c_ctx)
    mod = _modulation(cvecs, w_mod, b_mod).reshape(DEPTH, MOD_ROWS, N_MOD, D_MODEL)
    mod = jnp.pad(mod, ((0, 0), (0, 0), (0, SUBLANES - N_MOD), (0, 0)))

    rows = x_sample.shape[1] // GRID_W
    xs = _add_pos(x_sample, _grid_pos_embed(rows))
    xp = x_prompt
    zero_state = jnp.zeros((x_prompt.shape[0], 2, D_RNN), F32)
    ctx_states = []
    u_bf16, v_bf16 = peer_u.astype(BF16), peer_v.astype(BF16)
    for l in range(DEPTH):
        p = dict(w_in=w_in[l].astype(BF16), conv_w=conv_w[l], conv_b=conv_b[l],
                 wa=_block_diag_pairs(w_rg_a[l]), wx=_block_diag_pairs(w_rg_x[l]),
                 b_a=b_rg_a[l], b_x=b_rg_x[l], lam=lru_lambda[l],
                 w_pool=w_pool[l].astype(BF16), pool_scale=pool_scale[l], w_out=w_out[l].astype(BF16),
                 ln1_g=ln1_g[l], ln1_b=ln1_b[l], ln2_g=ln2_g[l], ln2_b=ln2_b[l],
                 wq=peer_wq[l].astype(BF16), keys=peer_keys[l].astype(BF16),
                 peer_u=u_bf16, peer_v=v_bf16, layer=l)
        for stream in ("ctx", "latent") if l % 2 == 0 else ("latent", "ctx"):
            if stream == "ctx":
                xp, st = _layer(xp, mod[l, n_req:n_req + 1], zero_state, p)
                ctx_states.append(st)
            else:
                xs, _ = _layer(xs, mod[l, :n_req], state_rglru[:, l], p)
    return xp, xs, jnp.stack(ctx_states, axis=1)
```

```python
import functools
import math

import jax
import jax.numpy as jnp
from jax import lax
from jax.experimental import pallas as pl
from jax.experimental.pallas import tpu as pltpu
from jax.experimental.pallas import tpu_sc as plsc

F32 = jnp.float32
BF16 = jnp.bfloat16

D_MODEL = 1024
DEPTH = 2
GRID_W = 64
D_RNN = 512
N_RNN_HEADS = 8
RNN_HEAD_DIM = D_RNN // N_RNN_HEADS
CONV_W = 4
LRU_C = 8.0
D_POOL = 512
POOL_WINDOWS = (2, 4, 8, 16)
POOL_GROUP_DIM = D_POOL // len(POOL_WINDOWS)
D_IN = 2 * D_RNN + D_POOL
PEER_HEADS = 8
PEER_NKEYS = 128
PEER_TOPK = 16
PEER_QDIM = 256
PEER_HALF = PEER_QDIM // 2
PEER_PICKS = PEER_HEADS * PEER_TOPK
N_MOD = 6
DEEPNORM_ALPHA = (2 * DEPTH) ** 0.25
LN_EPS = 1e-5

LANES = 128
SUBLANES = 8
VMEM_LIMIT_BYTES = 56 * 1024 * 1024
DENSE_VMEM_LIMIT_BYTES = 62 * 1024 * 1024
MOD_ROWS = 16
ROW_CHUNK = 256


def _params(*sem):
    return pltpu.CompilerParams(dimension_semantics=sem, vmem_limit_bytes=VMEM_LIMIT_BYTES)


def _layer_norm(y, g, b):
    mu = jnp.mean(y, -1, keepdims=True)
    var = jnp.mean(jnp.square(y - mu), -1, keepdims=True)
    return (y - mu) * lax.rsqrt(var + LN_EPS) * g + b


def _mod_body(c_ref, w_ref, b_ref, o_ref):
    c = c_ref[...]
    s = (c * jax.nn.sigmoid(c)).astype(BF16)
    o_ref[0] = jnp.dot(s, w_ref[0].astype(BF16), preferred_element_type=F32) + b_ref[0]


def _modulation(cvecs, w_mod, b_mod):
    n_out = N_MOD * D_MODEL
    tn = 1536
    return pl.pallas_call(
        _mod_body,
        grid=(DEPTH, n_out // tn),
        in_specs=[pl.BlockSpec((MOD_ROWS, D_MODEL), lambda l, j: (0, 0)),
                  pl.BlockSpec((1, D_MODEL, tn), lambda l, j: (l, 0, j)),
                  pl.BlockSpec((1, 1, tn), lambda l, j: (l, 0, j))],
        out_specs=pl.BlockSpec((1, MOD_ROWS, tn), lambda l, j: (l, 0, j)),
        out_shape=jax.ShapeDtypeStruct((DEPTH, MOD_ROWS, n_out), F32),
        compiler_params=_params("parallel", "parallel"),
        name="modulation",
    )(cvecs, w_mod, b_mod.reshape(DEPTH, 1, n_out))


def _add_pos_body(x_ref, p_ref, o_ref):
    o_ref[0] = x_ref[0] + p_ref[...]


def _add_pos(x, pos):
    b, t, d = x.shape
    tm = min(t, 512)
    return pl.pallas_call(
        _add_pos_body,
        grid=(b, t // tm),
        in_specs=[pl.BlockSpec((1, tm, d), lambda i, j: (i, j, 0)),
                  pl.BlockSpec((tm, d), lambda i, j: (j, 0))],
        out_specs=pl.BlockSpec((1, tm, d), lambda i, j: (i, j, 0)),
        out_shape=jax.ShapeDtypeStruct(x.shape, x.dtype),
        compiler_params=_params("parallel", "parallel"),
        name="add_pos",
    )(x, pos)


def _mix_in_body(x_ref, m_ref, w_ref, z_ref):
    m = m_ref[0]
    h = x_ref[0] * (1.0 + m[1:2]) + m[0:1]
    z_ref[0] = jnp.dot(h.astype(BF16), w_ref[...], preferred_element_type=F32)


def _mod_spec(mods):
    if mods.shape[0] == 1:
        return pl.BlockSpec((1, SUBLANES, D_MODEL), lambda i, j: (0, 0, 0))
    return pl.BlockSpec((1, SUBLANES, D_MODEL), lambda i, j: (i, 0, 0))


def _mix_in(x, mods, w_in):
    b, t, d = x.shape
    tm = min(t, 512)
    return pl.pallas_call(
        _mix_in_body,
        grid=(b, t // tm),
        in_specs=[pl.BlockSpec((1, tm, d), lambda i, j: (i, j, 0)),
                  _mod_spec(mods),
                  pl.BlockSpec((d, D_IN), lambda i, j: (0, 0))],
        out_specs=pl.BlockSpec((1, tm, D_IN), lambda i, j: (i, j, 0)),
        out_shape=jax.ShapeDtypeStruct((b, t, D_IN), F32),
        compiler_params=_params("parallel", "parallel"),
        name="mix_in",
    )(x, mods, w_in)


PAD = SUBLANES
UNROLL_GROUPS = 8


def _rnn_body(xr_ref, gr_ref, h0_ref, cw_ref, cb_ref, wa_ref, wx_ref, ba_ref, bx_ref, lam_ref,
              out_ref, st_ref, xpad, a_s, u_s):
    t_len = xr_ref.shape[1]
    rc = min(ROW_CHUNK, t_len)
    zeros = jnp.zeros((PAD, LANES), F32)
    xpad[0:PAD] = zeros
    xpad[t_len + PAD:t_len + 2 * PAD] = zeros
    xpad[PAD:t_len + PAD] = xr_ref[0]

    cw = cw_ref[...]
    lam = lam_ref[...]
    softplus_neg = jnp.maximum(-lam, 0.0) + jnp.log1p(jnp.exp(-jnp.abs(lam)))
    coef = -LRU_C * softplus_neg
    row = lax.broadcasted_iota(jnp.int32, (rc // SUBLANES, SUBLANES, LANES), 1)
    left = CONV_W // 2

    for c in range(t_len // rc):
        t0 = c * rc
        xc = cb_ref[...] + xpad[t0 + PAD - left:t0 + PAD - left + rc] * cw[0:1]
        for k in range(1, CONV_W):
            xc = xc + xpad[t0 + PAD - left + k:t0 + PAD - left + k + rc] * cw[k:k + 1]
        xcb = xc.astype(BF16)
        for d in range(2):
            r = jax.nn.sigmoid(jnp.dot(xcb, wa_ref[d, 0], preferred_element_type=F32) + ba_ref[d:d + 1])
            gi = jax.nn.sigmoid(jnp.dot(xcb, wx_ref[d, 0], preferred_element_type=F32) + bx_ref[d:d + 1])
            log_a = coef[d:d + 1] * r
            a = jnp.exp(log_a)
            u = jnp.sqrt(-jnp.tanh(log_a) * (a * a + 1.0)) * (gi * xc)
            a = a.reshape(rc // SUBLANES, SUBLANES, LANES)
            u = u.reshape(rc // SUBLANES, SUBLANES, LANES)
            for s in (1, 2, 4):
                if d == 0:
                    keep = row >= s
                    shift = s
                else:
                    keep = row <= SUBLANES - 1 - s
                    shift = SUBLANES - s
                a_sh = jnp.where(keep, pltpu.roll(a, shift, 1), 1.0)
                u_sh = jnp.where(keep, pltpu.roll(u, shift, 1), 0.0)
                u = a * u_sh + u
                a = a * a_sh
            a_s[d, t0:t0 + rc] = a.reshape(rc, LANES)
            u_s[d, t0:t0 + rc] = u.reshape(rc, LANES)

    n_groups = t_len // SUBLANES
    n_iter = n_groups // UNROLL_GROUPS

    def carry_step(i, carry):
        hf, hb = carry
        for j in range(UNROLL_GROUPS):
            gf = pl.multiple_of((i * UNROLL_GROUPS + j) * SUBLANES, SUBLANES)
            gb = pl.multiple_of((n_groups - 1 - i * UNROLL_GROUPS - j) * SUBLANES, SUBLANES)
            h_f = u_s[0, pl.ds(gf, SUBLANES)] + a_s[0, pl.ds(gf, SUBLANES)] * hf
            h_b = u_s[1, pl.ds(gb, SUBLANES)] + a_s[1, pl.ds(gb, SUBLANES)] * hb
            u_s[0, pl.ds(gf, SUBLANES)] = h_f
            u_s[1, pl.ds(gb, SUBLANES)] = h_b
            hf = h_f[SUBLANES - 1:SUBLANES]
            hb = h_b[0:1]
        return hf, hb

    h0 = h0_ref[0]
    hf, hb = lax.fori_loop(0, n_iter, carry_step, (h0[0:1], h0[1:2]))
    st_ref[0, 0:1] = hf
    st_ref[0, 1:2] = hb

    for c in range(t_len // rc):
        t0 = c * rc
        hsum = u_s[0, t0:t0 + rc] + u_s[1, t0:t0 + rc]
        out_ref[0, t0:t0 + rc] = (hsum * jax.nn.gelu(gr_ref[0, t0:t0 + rc])).astype(BF16)


def _rnn(z, h0, conv_w, conv_b, wa, wx, b_a, b_x, lam):
    b, t, _ = z.shape
    nblk = D_RNN // LANES
    return pl.pallas_call(
        _rnn_body,
        grid=(b, nblk),
        in_specs=[pl.BlockSpec((1, t, LANES), lambda i, j: (i, 0, j)),
                  pl.BlockSpec((1, t, LANES), lambda i, j: (i, 0, nblk + j)),
                  pl.BlockSpec((1, 2, LANES), lambda i, j: (i, 0, j)),
                  pl.BlockSpec((CONV_W, LANES), lambda i, j: (0, j)),
                  pl.BlockSpec((1, LANES), lambda i, j: (0, j)),
                  pl.BlockSpec((2, 1, LANES, LANES), lambda i, j: (0, j, 0, 0)),
                  pl.BlockSpec((2, 1, LANES, LANES), lambda i, j: (0, j, 0, 0)),
                  pl.BlockSpec((2, LANES), lambda i, j: (0, j)),
                  pl.BlockSpec((2, LANES), lambda i, j: (0, j)),
                  pl.BlockSpec((2, LANES), lambda i, j: (0, j))],
        out_specs=[pl.BlockSpec((1, t, LANES), lambda i, j: (i, 0, j)),
                   pl.BlockSpec((1, 2, LANES), lambda i, j: (i, 0, j))],
        out_shape=[jax.ShapeDtypeStruct((b, t, D_RNN), BF16),
                   jax.ShapeDtypeStruct((b, 2, D_RNN), F32)],
        scratch_shapes=[pltpu.VMEM((t + 2 * PAD, LANES), F32),
                        pltpu.VMEM((2, t, LANES), F32),
                        pltpu.VMEM((2, t, LANES), F32)],
        compiler_params=_params("parallel", "parallel"),
        name="rglru",
    )(z, z, h0, conv_w, conv_b.reshape(1, D_RNN), wa, wx, b_a, b_x, lam)


def _block_diag_pairs(w):
    per = LANES // RNN_HEAD_DIM
    w = w.reshape(2, D_RNN // LANES, per, RNN_HEAD_DIM, RNN_HEAD_DIM)
    out = jnp.zeros((2, D_RNN // LANES, LANES, LANES), w.dtype)
    for p in range(per):
        sl = slice(p * RNN_HEAD_DIM, (p + 1) * RNN_HEAD_DIM)
        out = out.at[:, :, sl, sl].set(w[:, :, p])
    return out.astype(BF16)


POOL_PAD = 16


def _pool_body(xq_ref, w_ref, sc_ref, out_ref, ppad):
    t_len = xq_ref.shape[1]
    rc = min(ROW_CHUNK, t_len)
    zeros = jnp.zeros((POOL_PAD, D_POOL), F32)
    ppad[0:POOL_PAD] = zeros
    ppad[t_len + POOL_PAD:t_len + 2 * POOL_PAD] = zeros
    ppad[POOL_PAD:t_len + POOL_PAD] = xq_ref[0]
    for c in range(t_len // rc):
        t0 = c * rc
        tpos = t0 + lax.broadcasted_iota(jnp.int32, (rc, POOL_GROUP_DIM), 0)
        for g, w in enumerate(POOL_WINDOWS):
            cols = slice(g * POOL_GROUP_DIM, (g + 1) * POOL_GROUP_DIM)
            half = w // 2
            base = t0 + POOL_PAD - half
            s = ppad[base:base + rc, cols]
            for k in range(1, w):
                s = s + ppad[base + k:base + k + rc, cols]
            cnt = (jnp.minimum(tpos + half, t_len) - jnp.maximum(tpos - half, 0)).astype(F32)
            pooled = s / cnt - ppad[t0 + POOL_PAD:t0 + POOL_PAD + rc, cols]
            y = jnp.dot(pooled.astype(BF16), w_ref[g], preferred_element_type=F32)
            out_ref[0, t0:t0 + rc, cols] = (y * sc_ref[:, cols]).astype(BF16)


def _pool(z, w_pool, pool_scale):
    b, t, _ = z.shape
    return pl.pallas_call(
        _pool_body,
        grid=(b,),
        in_specs=[pl.BlockSpec((1, t, D_POOL), lambda i: (i, 0, 2 * D_RNN // D_POOL)),
                  pl.BlockSpec((len(POOL_WINDOWS), POOL_GROUP_DIM, POOL_GROUP_DIM), lambda i: (0, 0, 0)),
                  pl.BlockSpec((1, D_POOL), lambda i: (0, 0))],
        out_specs=pl.BlockSpec((1, t, D_POOL), lambda i: (i, 0, 0)),
        out_shape=jax.ShapeDtypeStruct((b, t, D_POOL), BF16),
        scratch_shapes=[pltpu.VMEM((t + 2 * POOL_PAD, D_POOL), F32)],
        compiler_params=_params("parallel"),
        name="pool",
    )(z, w_pool, pool_scale.reshape(1, D_POOL))


def _mix_out_body(rnn_ref, pool_ref, x_ref, m_ref, w_ref, g_ref, b_ref, x1_ref, hq_ref):
    m = m_ref[0]
    mix = (jnp.dot(rnn_ref[0], w_ref[0:D_RNN], preferred_element_type=F32)
           + jnp.dot(pool_ref[0], w_ref[D_RNN:D_RNN + D_POOL], preferred_element_type=F32))
    x1 = _layer_norm(DEEPNORM_ALPHA * x_ref[0] + m[2:3] * mix, g_ref[...], b_ref[...])
    x1_ref[0] = x1
    hq_ref[0] = (x1 * (1.0 + m[4:5]) + m[3:4]).astype(BF16)


def _mix_out(rnn, pool, x, mods, w_out, ln_g, ln_b):
    b, t, d = x.shape
    tm = min(t, 512)
    tok = lambda i, j: (i, j, 0)
    return pl.pallas_call(
        _mix_out_body,
        grid=(b, t // tm),
        in_specs=[pl.BlockSpec((1, tm, D_RNN), tok),
                  pl.BlockSpec((1, tm, D_POOL), tok),
                  pl.BlockSpec((1, tm, d), tok),
                  _mod_spec(mods),
                  pl.BlockSpec((D_RNN + D_POOL, d), lambda i, j: (0, 0)),
                  pl.BlockSpec((1, d), lambda i, j: (0, 0)),
                  pl.BlockSpec((1, d), lambda i, j: (0, 0))],
        out_specs=[pl.BlockSpec((1, tm, d), tok), pl.BlockSpec((1, tm, d), tok)],
        out_shape=[jax.ShapeDtypeStruct(x.shape, F32), jax.ShapeDtypeStruct(x.shape, BF16)],
        compiler_params=_params("parallel", "parallel"),
        name="mix_out",
    )(rnn, pool, x, mods, w_out, ln_g.reshape(1, d), ln_b.reshape(1, d))


NEG_INF = float("-inf")


BIG_ID = 2 ** 30


def _sort_network(n):
    def merge(lo, hi, r):
        step = r * 2
        if step < hi - lo:
            yield from merge(lo, hi, step)
            yield from merge(lo + r, hi, step)
            for i in range(lo + r, hi - r, step):
                yield (i, i + r)
        else:
            yield (lo, lo + r)

    def sort(lo, hi):
        if hi - lo >= 1:
            mid = lo + (hi - lo) // 2
            yield from sort(lo, mid)
            yield from sort(mid + 1, hi)
            yield from merge(lo, hi, 1)

    return list(sort(0, n - 1))


def _sorted_levels(s_t):
    n_lvl = s_t.shape[0] // SUBLANES
    sub = lax.broadcasted_iota(jnp.int32, (SUBLANES, s_t.shape[1]), 0)
    vals = [s_t[l * SUBLANES:(l + 1) * SUBLANES] for l in range(n_lvl)]
    ids = [sub + l * SUBLANES for l in range(n_lvl)]
    untouched = [True] * n_lvl
    for i, j in _sort_network(n_lvl):
        a, b, ia, ib = vals[i], vals[j], ids[i], ids[j]
        swap = b > a
        if not (untouched[i] and untouched[j]):
            swap = swap | ((b == a) & (ib < ia))
        vals[i], vals[j] = jnp.maximum(a, b), jnp.minimum(a, b)
        ids[i], ids[j] = jnp.where(swap, ib, ia), jnp.where(swap, ia, ib)
        untouched[i] = untouched[j] = False
    return vals, ids


def _pop_top(vals, ids, k):
    vals, ids = list(vals), list(ids)
    out_v, out_i = [], []
    for it in range(k):
        m = jnp.max(vals[0], axis=0, keepdims=True)
        pick = jnp.min(jnp.where(vals[0] == m, ids[0], BIG_ID), axis=0, keepdims=True)
        sel = ids[0] == pick
        out_v.append(m)
        out_i.append(pick)
        for l in range(min(len(vals) - 1, k - 1 - it)):
            vals[l] = jnp.where(sel, vals[l + 1], vals[l])
            ids[l] = jnp.where(sel, ids[l + 1], ids[l])
    return jnp.concatenate(out_v, axis=0), jnp.concatenate(out_i, axis=0)


def _pair_top(s1, i1, s2, i2):
    k = PEER_TOPK
    tm = s1.shape[1]
    sub = lax.broadcasted_iota(jnp.int32, (SUBLANES, tm), 0)
    s2lo, i2lo = s2[0:SUBLANES], i2[0:SUBLANES]
    cand, expert = [], []
    for l in range(k):
        ok = (sub + 1) * (l + 1) <= k
        cand.append(jnp.where(ok, s1[l:l + 1] + s2lo, NEG_INF))
        expert.append(i1[l:l + 1] * PEER_NKEYS + i2lo)
    single = s1[0:1] + s2[SUBLANES:k]
    single_expert = i1[0:1] * PEER_NKEYS + i2[SUBLANES:k]
    single_flat = sub + SUBLANES
    pops = jnp.zeros((SUBLANES, tm), jnp.int32)
    out_s, out_e = [], []
    for it in range(k):
        flat = pops * k + sub
        m = jnp.max(jnp.maximum(cand[0], single), axis=0, keepdims=True)
        pick = jnp.min(jnp.minimum(jnp.where(cand[0] == m, flat, BIG_ID),
                                   jnp.where(single == m, single_flat, BIG_ID)), axis=0, keepdims=True)
        sel = flat == pick
        sel_single = single_flat == pick
        out_s.append(m)
        out_e.append(jnp.sum(jnp.where(sel, expert[0], 0) + jnp.where(sel_single, single_expert, 0),
                             axis=0, keepdims=True))
        single = jnp.where(sel_single, NEG_INF, single)
        pops = jnp.where(sel, pops + 1, pops)
        for l in range(k - 1 - it):
            cand[l] = jnp.where(sel, cand[l + 1], cand[l])
            expert[l] = jnp.where(sel, expert[l + 1], expert[l])
    return jnp.concatenate(out_s, axis=0), jnp.concatenate(out_e, axis=0)


def _route_body(hq_ref, wq_ref, keys_ref, idx_ref, gate_ref):
    q = jnp.dot(hq_ref[...], wq_ref[...], preferred_element_type=F32)
    gates, experts = [], []
    for h in range(PEER_HEADS):
        tops = []
        for p in range(2):
            c0 = (h * 2 + p) * PEER_HALF
            qs = q[:, c0:c0 + PEER_HALF].astype(BF16)
            s_t = lax.dot_general(keys_ref[h, p], qs, (((1,), (1,)), ((), ())),
                                  preferred_element_type=F32)
            tops.append(_pop_top(*_sorted_levels(s_t), PEER_TOPK))
        (s1, i1), (s2, i2) = tops
        sc, picked = _pair_top(s1, i1, s2, i2)
        e = jnp.exp(sc - sc[0:1])
        gates.append(e / jnp.sum(e, axis=0, keepdims=True))
        experts.append(picked)
    gate_ref[...] = jnp.concatenate(gates, axis=0).T
    idx_ref[...] = jnp.concatenate(experts, axis=0).T


def _route(hq, wq, keys):
    n, d = hq.shape
    tm = 256
    return pl.pallas_call(
        _route_body,
        grid=(n // tm,),
        in_specs=[pl.BlockSpec((tm, d), lambda i: (i, 0)),
                  pl.BlockSpec((d, PEER_HEADS * PEER_QDIM), lambda i: (0, 0)),
                  pl.BlockSpec((PEER_HEADS, 2, PEER_NKEYS, PEER_HALF), lambda i: (0, 0, 0, 0))],
        out_specs=[pl.BlockSpec((tm, PEER_PICKS), lambda i: (i, 0)),
                   pl.BlockSpec((tm, PEER_PICKS), lambda i: (i, 0))],
        out_shape=[jax.ShapeDtypeStruct((n, PEER_PICKS), jnp.int32),
                   jax.ShapeDtypeStruct((n, PEER_PICKS), F32)],
        compiler_params=_params("parallel"),
        name="peer_route",
    )(hq, wq, keys)


N_EXPERTS = PEER_NKEYS * PEER_NKEYS
GATE_ROWS = 4
GATE_TOKENS = 32
PEER_TB = 1024
PEER_EC = 2048


def _gate_matrix(idx, gates):
    n = idx.shape[0]
    info = plsc.get_sparse_core_info()
    n_workers = info.num_cores * info.num_subcores
    lanes = info.num_lanes
    per_w = n // n_workers
    assert n % (n_workers * GATE_TOKENS) == 0 and GATE_TOKENS % GATE_ROWS == 0 and PEER_PICKS % lanes == 0
    vecs = PEER_PICKS // lanes
    mesh = plsc.VectorSubcoreMesh(core_axis_name="core", subcore_axis_name="subcore")

    @functools.partial(
        pl.kernel, out_type=jax.ShapeDtypeStruct((n, N_EXPERTS), F32), mesh=mesh,
        scratch_types=[pltpu.VMEM((GATE_TOKENS * PEER_PICKS,), jnp.int32),
                       pltpu.VMEM((GATE_TOKENS * PEER_PICKS,), F32)]
                      + [pltpu.VMEM((N_EXPERTS,), F32)] * GATE_ROWS
                      + [pltpu.SemaphoreType.DMA] * GATE_ROWS,
        compiler_params=pltpu.CompilerParams(needs_layout_passes=False), name="sc_gate_matrix")
    def run(idx_hbm, gate_hbm, out_hbm, idx_v, g_v, *bufs):
        rows, sems = bufs[:GATE_ROWS], bufs[GATE_ROWS:]
        wid = lax.axis_index("subcore") * info.num_cores + lax.axis_index("core")
        tok_base = wid * per_w
        zeros = jnp.zeros((lanes,), F32)

        @pl.loop(0, N_EXPERTS // lanes)
        def _(i):
            for r in rows:
                r[pl.ds(pl.multiple_of(i * lanes, lanes), lanes)] = zeros

        @pl.loop(0, per_w // GATE_TOKENS)
        def _(blk):
            tok0 = tok_base + blk * GATE_TOKENS
            k_at = pl.ds(pl.multiple_of(tok0 * PEER_PICKS, GATE_TOKENS * PEER_PICKS), GATE_TOKENS * PEER_PICKS)
            pltpu.sync_copy(idx_hbm.at[k_at], idx_v)
            pltpu.sync_copy(gate_hbm.at[k_at], g_v)

            @pl.loop(0, GATE_TOKENS // GATE_ROWS)
            def _(q):
                def picks(b, c):
                    return pl.ds(pl.multiple_of(((q * GATE_ROWS + b) * vecs + c) * lanes, lanes), lanes)

                for b in range(GATE_ROWS):
                    for c in range(vecs):
                        plsc.addupdate_scatter(rows[b], [idx_v[picks(b, c)]], g_v[picks(b, c)])
                    pltpu.make_async_copy(rows[b], out_hbm.at[tok0 + q * GATE_ROWS + b], sems[b]).start()
                for b in range(GATE_ROWS):
                    pltpu.make_async_copy(rows[b], out_hbm.at[tok0 + q * GATE_ROWS + b], sems[b]).wait()
                    for c in range(vecs):
                        plsc.store_scatter(rows[b], [idx_v[picks(b, c)]], zeros)

    return run(idx.reshape(-1), gates.reshape(-1))


def _dense_body(x_ref, u_ref, v_ref, g_ref, x1_ref, m_ref, lg_ref, lb_ref, o_ref):
    j = pl.program_id(1)
    s = lax.dot_general(x_ref[...], u_ref[...], (((1,), (1,)), ((), ())), preferred_element_type=F32)
    g = g_ref[...]
    a = jnp.where(g != 0.0, jax.nn.gelu(s) * g, 0.0).astype(BF16)
    part = jnp.dot(a, v_ref[...], preferred_element_type=F32)

    @pl.when(j == 0)
    def _():
        o_ref[...] = part

    @pl.when(j > 0)
    def _():
        o_ref[...] += part

    @pl.when(j == pl.num_programs(1) - 1)
    def _():
        m = m_ref[0]
        o_ref[...] = _layer_norm(DEEPNORM_ALPHA * x1_ref[...] + m[5:6] * o_ref[...], lg_ref[...], lb_ref[...])


def _experts(hq, gate_mat, u_tab, v_tab, layer, x1, mods, seq_len, ln_g, ln_b):
    n, d = hq.shape
    if mods.shape[0] == 1:
        tb = min(PEER_TB, n)
        mod_spec = pl.BlockSpec((1, SUBLANES, d), lambda i, j: (0, 0, 0))
    else:
        tb = min(PEER_TB, seq_len)
        assert seq_len % tb == 0
        mod_spec = pl.BlockSpec((1, SUBLANES, d), lambda i, j: (i * tb // seq_len, 0, 0))
    return pl.pallas_call(
        _dense_body,
        grid=(n // tb, N_EXPERTS // PEER_EC),
        in_specs=[pl.BlockSpec((tb, d), lambda i, j: (i, 0), pipeline_mode=pl.Buffered(1)),
                  pl.BlockSpec((None, PEER_EC, d), lambda i, j: (layer, j, 0)),
                  pl.BlockSpec((None, PEER_EC, d), lambda i, j: (layer, j, 0)),
                  pl.BlockSpec((tb, PEER_EC), lambda i, j: (i, j)),
                  pl.BlockSpec((tb, d), lambda i, j: (i, 0), pipeline_mode=pl.Buffered(1)),
                  mod_spec,
                  pl.BlockSpec((1, d), lambda i, j: (0, 0)),
                  pl.BlockSpec((1, d), lambda i, j: (0, 0))],
        out_specs=pl.BlockSpec((tb, d), lambda i, j: (i, 0)),
        out_shape=jax.ShapeDtypeStruct((n, d), F32),
        compiler_params=pltpu.CompilerParams(dimension_semantics=("parallel", "arbitrary"),
                                             vmem_limit_bytes=DENSE_VMEM_LIMIT_BYTES),
        name="peer_dense",
    )(hq, u_tab, v_tab, gate_mat, x1, mods, ln_g.reshape(1, d), ln_b.reshape(1, d))


def _grid_pos_embed(rows):
    t = jnp.arange(rows * GRID_W)
    r = (t // GRID_W).astype(F32)
    col = (t % GRID_W).astype(F32)
    n_freq = D_MODEL // 4
    omega = 1.0 / (10000.0 ** (jnp.arange(n_freq, dtype=F32) / n_freq))

    def enc(p):
        ang = p[:, None] * omega[None, :]
        return jnp.concatenate([jnp.sin(ang), jnp.cos(ang)], -1)
    return jnp.concatenate([enc(r), enc(col)], -1)


def _layer(x, mods, h0, p):
    b, t, d = x.shape
    z = _mix_in(x, mods, p["w_in"])
    rnn, st = _rnn(z, h0, p["conv_w"], p["conv_b"], p["wa"], p["wx"], p["b_a"], p["b_x"], p["lam"])
    pool = _pool(z, p["w_pool"], p["pool_scale"])
    x1, hq = _mix_out(rnn, pool, x, mods, p["w_out"], p["ln1_g"], p["ln1_b"])
    hq = hq.reshape(b * t, d)
    idx, gates = _route(hq, p["wq"], p["keys"])
    gate_mat = _gate_matrix(idx, gates)
    x2 = _experts(hq, gate_mat, p["peer_u"], p["peer_v"], p["layer"], x1.reshape(b * t, d), mods, t,
                  p["ln2_g"], p["ln2_b"])
    return x2.reshape(b, t, d), st


def kernel(x_prompt, x_sample, state_rglru, c, c_ctx, w_mod, b_mod, w_in, conv_w, conv_b,
           w_rg_a, b_rg_a, w_rg_x, b_rg_x, lru_lambda, w_pool, pool_scale, w_out,
           ln1_g, ln1_b, ln2_g, ln2_b, peer_wq, peer_keys, peer_u, peer_v):
    n_req = c.shape[0]
    cvecs = jnp.zeros((MOD_ROWS, D_MODEL), F32).at[:n_req].set(c).at[n_req].set(c_ctx)
    mod = _modulation(cvecs, w_mod, b_mod).reshape(DEPTH, MOD_ROWS, N_MOD, D_MODEL)
    mod = jnp.pad(mod, ((0, 0), (0, 0), (0, SUBLANES - N_MOD), (0, 0)))

    rows = x_sample.shape[1] // GRID_W
    xs = _add_pos(x_sample, _grid_pos_embed(rows))
    xp = x_prompt
    zero_state = jnp.zeros((x_prompt.shape[0], 2, D_RNN), F32)
    ctx_states = []
    u_bf16, v_bf16 = peer_u.astype(BF16), peer_v.astype(BF16)
    for l in range(DEPTH):
        p = dict(w_in=w_in[l].astype(BF16), conv_w=conv_w[l], conv_b=conv_b[l],
                 wa=_block_diag_pairs(w_rg_a[l]), wx=_block_diag_pairs(w_rg_x[l]),
                 b_a=b_rg_a[l], b_x=b_rg_x[l], lam=lru_lambda[l],
                 w_pool=w_pool[l].astype(BF16), pool_scale=pool_scale[l], w_out=w_out[l].astype(BF16),
                 ln1_g=ln1_g[l], ln1_b=ln1_b[l], ln2_g=ln2_g[l], ln2_b=ln2_b[l],
                 wq=peer_wq[l].astype(BF16), keys=peer_keys[l].astype(BF16),
                 peer_u=u_bf16, peer_v=v_bf16, layer=l)
        for stream in ("ctx", "latent") if l % 2 == 0 else ("latent", "ctx"):
            if stream == "ctx":
                xp, st = _layer(xp, mod[l, n_req:n_req + 1], zero_state, p)
                ctx_states.append(st)
            else:
                xs, _ = _layer(xs, mod[l, :n_req], state_rglru[:, l], p)
    return xp, xs, jnp.stack(ctx_states, axis=1)
```

```python
import functools
import math

import jax
import jax.numpy as jnp
from jax import lax
from jax.experimental import pallas as pl
from jax.experimental.pallas import tpu as pltpu
from jax.experimental.pallas import tpu_sc as plsc

F32 = jnp.float32
BF16 = jnp.bfloat16

D_MODEL = 1024
DEPTH = 2
GRID_W = 64
D_RNN = 512
N_RNN_HEADS = 8
RNN_HEAD_DIM = D_RNN // N_RNN_HEADS
CONV_W = 4
LRU_C = 8.0
D_POOL = 512
POOL_WINDOWS = (2, 4, 8, 16)
POOL_GROUP_DIM = D_POOL // len(POOL_WINDOWS)
D_IN = 2 * D_RNN + D_POOL
PEER_HEADS = 8
PEER_NKEYS = 128
PEER_TOPK = 16
PEER_QDIM = 256
PEER_HALF = PEER_QDIM // 2
PEER_PICKS = PEER_HEADS * PEER_TOPK
N_MOD = 6
DEEPNORM_ALPHA = (2 * DEPTH) ** 0.25
LN_EPS = 1e-5

LANES = 128
SUBLANES = 8
VMEM_LIMIT_BYTES = 56 * 1024 * 1024
MOD_ROWS = 16
ROW_CHUNK = 256


def _params(*sem):
    return pltpu.CompilerParams(dimension_semantics=sem, vmem_limit_bytes=VMEM_LIMIT_BYTES)


def _layer_norm(y, g, b):
    mu = jnp.mean(y, -1, keepdims=True)
    var = jnp.mean(jnp.square(y - mu), -1, keepdims=True)
    return (y - mu) * lax.rsqrt(var + LN_EPS) * g + b


def _mod_body(c_ref, w_ref, b_ref, o_ref):
    c = c_ref[...]
    s = (c * jax.nn.sigmoid(c)).astype(BF16)
    o_ref[0] = jnp.dot(s, w_ref[0].astype(BF16), preferred_element_type=F32) + b_ref[0]


def _modulation(cvecs, w_mod, b_mod):
    n_out = N_MOD * D_MODEL
    tn = 1536
    return pl.pallas_call(
        _mod_body,
        grid=(DEPTH, n_out // tn),
        in_specs=[pl.BlockSpec((MOD_ROWS, D_MODEL), lambda l, j: (0, 0)),
                  pl.BlockSpec((1, D_MODEL, tn), lambda l, j: (l, 0, j)),
                  pl.BlockSpec((1, 1, tn), lambda l, j: (l, 0, j))],
        out_specs=pl.BlockSpec((1, MOD_ROWS, tn), lambda l, j: (l, 0, j)),
        out_shape=jax.ShapeDtypeStruct((DEPTH, MOD_ROWS, n_out), F32),
        compiler_params=_params("parallel", "parallel"),
        name="modulation",
    )(cvecs, w_mod, b_mod.reshape(DEPTH, 1, n_out))


def _add_pos_body(x_ref, p_ref, o_ref):
    o_ref[0] = x_ref[0] + p_ref[...]


def _add_pos(x, pos):
    b, t, d = x.shape
    tm = min(t, 512)
    return pl.pallas_call(
        _add_pos_body,
        grid=(b, t // tm),
        in_specs=[pl.BlockSpec((1, tm, d), lambda i, j: (i, j, 0)),
                  pl.BlockSpec((tm, d), lambda i, j: (j, 0))],
        out_specs=pl.BlockSpec((1, tm, d), lambda i, j: (i, j, 0)),
        out_shape=jax.ShapeDtypeStruct(x.shape, x.dtype),
        compiler_params=_params("parallel", "parallel"),
        name="add_pos",
    )(x, pos)


def _mix_in_body(x_ref, m_ref, w_ref, z_ref):
    m = m_ref[0]
    h = x_ref[0] * (1.0 + m[1:2]) + m[0:1]
    z_ref[0] = jnp.dot(h.astype(BF16), w_ref[...], preferred_element_type=F32)


def _mod_spec(mods):
    if mods.shape[0] == 1:
        return pl.BlockSpec((1, SUBLANES, D_MODEL), lambda i, j: (0, 0, 0))
    return pl.BlockSpec((1, SUBLANES, D_MODEL), lambda i, j: (i, 0, 0))


def _mix_in(x, mods, w_in):
    b, t, d = x.shape
    tm = min(t, 512)
    return pl.pallas_call(
        _mix_in_body,
        grid=(b, t // tm),
        in_specs=[pl.BlockSpec((1, tm, d), lambda i, j: (i, j, 0)),
                  _mod_spec(mods),
                  pl.BlockSpec((d, D_IN), lambda i, j: (0, 0))],
        out_specs=pl.BlockSpec((1, tm, D_IN), lambda i, j: (i, j, 0)),
        out_shape=jax.ShapeDtypeStruct((b, t, D_IN), F32),
        compiler_params=_params("parallel", "parallel"),
        name="mix_in",
    )(x, mods, w_in)


PAD = SUBLANES
CARRY_PAD = 2 * SUBLANES


def _group_scan(a, u, reverse):
    row = lax.broadcasted_iota(jnp.int32, a.shape, 1)
    for s in (1, 2, 4):
        if reverse:
            keep = row <= SUBLANES - 1 - s
            shift = SUBLANES - s
        else:
            keep = row >= s
            shift = s
        a_sh = jnp.where(keep, pltpu.roll(a, shift, 1), 1.0)
        u_sh = jnp.where(keep, pltpu.roll(u, shift, 1), 0.0)
        u = a * u_sh + u
        a = a * a_sh
    return a, u


def _rnn_body(xr_ref, gr_ref, h0_ref, cw_ref, cb_ref, wa_ref, wx_ref, ba_ref, bx_ref, lam_ref,
              out_ref, st_ref, xpad, a_s, u_s, a2_s, u2_s, e_s):
    t_len = xr_ref.shape[1]
    rc = min(ROW_CHUNK, t_len)
    zeros = jnp.zeros((PAD, LANES), F32)
    xpad[0:PAD] = zeros
    xpad[t_len + PAD:t_len + 2 * PAD] = zeros
    xpad[PAD:t_len + PAD] = xr_ref[0]

    cw = cw_ref[...]
    lam = lam_ref[...]
    softplus_neg = jnp.maximum(-lam, 0.0) + jnp.log1p(jnp.exp(-jnp.abs(lam)))
    coef = -LRU_C * softplus_neg
    left = CONV_W // 2

    for c in range(t_len // rc):
        t0 = c * rc
        xc = cb_ref[...] + xpad[t0 + PAD - left:t0 + PAD - left + rc] * cw[0:1]
        for k in range(1, CONV_W):
            xc = xc + xpad[t0 + PAD - left + k:t0 + PAD - left + k + rc] * cw[k:k + 1]
        xcb = xc.astype(BF16)
        for d in range(2):
            r = jax.nn.sigmoid(jnp.dot(xcb, wa_ref[d, 0], preferred_element_type=F32) + ba_ref[d:d + 1])
            gi = jax.nn.sigmoid(jnp.dot(xcb, wx_ref[d, 0], preferred_element_type=F32) + bx_ref[d:d + 1])
            log_a = coef[d:d + 1] * r
            a = jnp.exp(log_a)
            u = jnp.sqrt(-jnp.tanh(log_a) * (a * a + 1.0)) * (gi * xc)
            a, u = _group_scan(a.reshape(rc // SUBLANES, SUBLANES, LANES),
                               u.reshape(rc // SUBLANES, SUBLANES, LANES), reverse=(d == 1))
            a_s[d, t0:t0 + rc] = a.reshape(rc, LANES)
            u_s[d, t0:t0 + rc] = u.reshape(rc, LANES)

    n_groups = t_len // SUBLANES
    n_super = n_groups // SUBLANES
    for d in range(2):
        edge = pl.ds(SUBLANES - 1 if d == 0 else 0, n_groups, stride=SUBLANES)
        a2, u2 = _group_scan(a_s[d, edge].reshape(n_super, SUBLANES, LANES),
                             u_s[d, edge].reshape(n_super, SUBLANES, LANES), reverse=(d == 1))
        a2_s[d] = a2.reshape(n_groups, LANES)
        u2_s[d] = u2.reshape(n_groups, LANES)

    h0 = h0_ref[0]
    e_s[0, SUBLANES - 1:SUBLANES] = h0[0:1]
    e_s[1, n_groups:n_groups + 1] = h0[1:2]
    unroll = min(SUBLANES, n_super)

    def carry_step(i, carry):
        hf, hb = carry
        for j in range(unroll):
            gf = pl.multiple_of((i * unroll + j) * SUBLANES, SUBLANES)
            gb = pl.multiple_of((n_super - 1 - i * unroll - j) * SUBLANES, SUBLANES)
            h_f = u2_s[0, pl.ds(gf, SUBLANES)] + a2_s[0, pl.ds(gf, SUBLANES)] * hf
            h_b = u2_s[1, pl.ds(gb, SUBLANES)] + a2_s[1, pl.ds(gb, SUBLANES)] * hb
            e_s[0, pl.ds(gf + SUBLANES, SUBLANES)] = h_f
            e_s[1, pl.ds(gb, SUBLANES)] = h_b
            hf = h_f[SUBLANES - 1:SUBLANES]
            hb = h_b[0:1]
        return hf, hb

    hf, hb = lax.fori_loop(0, n_super // unroll, carry_step, (h0[0:1], h0[1:2]))
    st_ref[0, 0:1] = hf
    st_ref[0, 1:2] = hb

    gc = rc // SUBLANES
    for c in range(t_len // rc):
        t0 = c * rc
        g0 = t0 // SUBLANES
        hs = []
        for d, off in ((0, SUBLANES - 1), (1, 1)):
            entry = e_s[d, g0 + off:g0 + off + gc]
            hs.append(u_s[d, t0:t0 + rc].reshape(gc, SUBLANES, LANES)
                      + a_s[d, t0:t0 + rc].reshape(gc, SUBLANES, LANES) * entry[:, None, :])
        hsum = (hs[0] + hs[1]).reshape(rc, LANES)
        out_ref[0, t0:t0 + rc] = (hsum * jax.nn.gelu(gr_ref[0, t0:t0 + rc])).astype(BF16)


def _rnn(z, h0, conv_w, conv_b, wa, wx, b_a, b_x, lam):
    b, t, _ = z.shape
    nblk = D_RNN // LANES
    return pl.pallas_call(
        _rnn_body,
        grid=(b, nblk),
        in_specs=[pl.BlockSpec((1, t, LANES), lambda i, j: (i, 0, j)),
                  pl.BlockSpec((1, t, LANES), lambda i, j: (i, 0, nblk + j)),
                  pl.BlockSpec((1, 2, LANES), lambda i, j: (i, 0, j)),
                  pl.BlockSpec((CONV_W, LANES), lambda i, j: (0, j)),
                  pl.BlockSpec((1, LANES), lambda i, j: (0, j)),
                  pl.BlockSpec((2, 1, LANES, LANES), lambda i, j: (0, j, 0, 0)),
                  pl.BlockSpec((2, 1, LANES, LANES), lambda i, j: (0, j, 0, 0)),
                  pl.BlockSpec((2, LANES), lambda i, j: (0, j)),
                  pl.BlockSpec((2, LANES), lambda i, j: (0, j)),
                  pl.BlockSpec((2, LANES), lambda i, j: (0, j))],
        out_specs=[pl.BlockSpec((1, t, LANES), lambda i, j: (i, 0, j)),
                   pl.BlockSpec((1, 2, LANES), lambda i, j: (i, 0, j))],
        out_shape=[jax.ShapeDtypeStruct((b, t, D_RNN), BF16),
                   jax.ShapeDtypeStruct((b, 2, D_RNN), F32)],
        scratch_shapes=[pltpu.VMEM((t + 2 * PAD, LANES), F32),
                        pltpu.VMEM((2, t, LANES), F32),
                        pltpu.VMEM((2, t, LANES), F32),
                        pltpu.VMEM((2, t // SUBLANES, LANES), F32),
                        pltpu.VMEM((2, t // SUBLANES, LANES), F32),
                        pltpu.VMEM((2, t // SUBLANES + CARRY_PAD, LANES), F32)],
        compiler_params=_params("parallel", "parallel"),
        name="rglru",
    )(z, z, h0, conv_w, conv_b.reshape(1, D_RNN), wa, wx, b_a, b_x, lam)


def _block_diag_pairs(w):
    per = LANES // RNN_HEAD_DIM
    w = w.reshape(2, D_RNN // LANES, per, RNN_HEAD_DIM, RNN_HEAD_DIM)
    out = jnp.zeros((2, D_RNN // LANES, LANES, LANES), w.dtype)
    for p in range(per):
        sl = slice(p * RNN_HEAD_DIM, (p + 1) * RNN_HEAD_DIM)
        out = out.at[:, :, sl, sl].set(w[:, :, p])
    return out.astype(BF16)


POOL_PAD = 16


def _pool_body(xq_ref, w_ref, sc_ref, out_ref, ppad):
    t_len = xq_ref.shape[1]
    rc = min(ROW_CHUNK, t_len)
    zeros = jnp.zeros((POOL_PAD, D_POOL), F32)
    ppad[0:POOL_PAD] = zeros
    ppad[t_len + POOL_PAD:t_len + 2 * POOL_PAD] = zeros
    ppad[POOL_PAD:t_len + POOL_PAD] = xq_ref[0]
    for c in range(t_len // rc):
        t0 = c * rc
        tpos = t0 + lax.broadcasted_iota(jnp.int32, (rc, POOL_GROUP_DIM), 0)
        for g, w in enumerate(POOL_WINDOWS):
            cols = slice(g * POOL_GROUP_DIM, (g + 1) * POOL_GROUP_DIM)
            half = w // 2
            base = t0 + POOL_PAD - half
            s = ppad[base:base + rc, cols]
            for k in range(1, w):
                s = s + ppad[base + k:base + k + rc, cols]
            cnt = (jnp.minimum(tpos + half, t_len) - jnp.maximum(tpos - half, 0)).astype(F32)
            pooled = s / cnt - ppad[t0 + POOL_PAD:t0 + POOL_PAD + rc, cols]
            y = jnp.dot(pooled.astype(BF16), w_ref[g], preferred_element_type=F32)
            out_ref[0, t0:t0 + rc, cols] = (y * sc_ref[:, cols]).astype(BF16)


def _pool(z, w_pool, pool_scale):
    b, t, _ = z.shape
    return pl.pallas_call(
        _pool_body,
        grid=(b,),
        in_specs=[pl.BlockSpec((1, t, D_POOL), lambda i: (i, 0, 2 * D_RNN // D_POOL)),
                  pl.BlockSpec((len(POOL_WINDOWS), POOL_GROUP_DIM, POOL_GROUP_DIM), lambda i: (0, 0, 0)),
                  pl.BlockSpec((1, D_POOL), lambda i: (0, 0))],
        out_specs=pl.BlockSpec((1, t, D_POOL), lambda i: (i, 0, 0)),
        out_shape=jax.ShapeDtypeStruct((b, t, D_POOL), BF16),
        scratch_shapes=[pltpu.VMEM((t + 2 * POOL_PAD, D_POOL), F32)],
        compiler_params=_params("parallel"),
        name="pool",
    )(z, w_pool, pool_scale.reshape(1, D_POOL))


def _mix_out_body(rnn_ref, pool_ref, x_ref, m_ref, w_ref, g_ref, b_ref, x1_ref, hq_ref):
    m = m_ref[0]
    mix = (jnp.dot(rnn_ref[0], w_ref[0:D_RNN], preferred_element_type=F32)
           + jnp.dot(pool_ref[0], w_ref[D_RNN:D_RNN + D_POOL], preferred_element_type=F32))
    x1 = _layer_norm(DEEPNORM_ALPHA * x_ref[0] + m[2:3] * mix, g_ref[...], b_ref[...])
    x1_ref[0] = x1
    hq_ref[0] = (x1 * (1.0 + m[4:5]) + m[3:4]).astype(BF16)


def _mix_out(rnn, pool, x, mods, w_out, ln_g, ln_b):
    b, t, d = x.shape
    tm = min(t, 512)
    tok = lambda i, j: (i, j, 0)
    return pl.pallas_call(
        _mix_out_body,
        grid=(b, t // tm),
        in_specs=[pl.BlockSpec((1, tm, D_RNN), tok),
                  pl.BlockSpec((1, tm, D_POOL), tok),
                  pl.BlockSpec((1, tm, d), tok),
                  _mod_spec(mods),
                  pl.BlockSpec((D_RNN + D_POOL, d), lambda i, j: (0, 0)),
                  pl.BlockSpec((1, d), lambda i, j: (0, 0)),
                  pl.BlockSpec((1, d), lambda i, j: (0, 0))],
        out_specs=[pl.BlockSpec((1, tm, d), tok), pl.BlockSpec((1, tm, d), tok)],
        out_shape=[jax.ShapeDtypeStruct(x.shape, F32), jax.ShapeDtypeStruct(x.shape, BF16)],
        compiler_params=_params("parallel", "parallel"),
        name="mix_out",
    )(rnn, pool, x, mods, w_out, ln_g.reshape(1, d), ln_b.reshape(1, d))


NEG_INF = float("-inf")


BIG_ID = 2 ** 30


def _sort_network(n):
    def merge(lo, hi, r):
        step = r * 2
        if step < hi - lo:
            yield from merge(lo, hi, step)
            yield from merge(lo + r, hi, step)
            for i in range(lo + r, hi - r, step):
                yield (i, i + r)
        else:
            yield (lo, lo + r)

    def sort(lo, hi):
        if hi - lo >= 1:
            mid = lo + (hi - lo) // 2
            yield from sort(lo, mid)
            yield from sort(mid + 1, hi)
            yield from merge(lo, hi, 1)

    return list(sort(0, n - 1))


def _sorted_levels(s_t):
    n_lvl = s_t.shape[0] // SUBLANES
    sub = lax.broadcasted_iota(jnp.int32, (SUBLANES, s_t.shape[1]), 0)
    vals = [s_t[l * SUBLANES:(l + 1) * SUBLANES] for l in range(n_lvl)]
    ids = [sub + l * SUBLANES for l in range(n_lvl)]
    untouched = [True] * n_lvl
    for i, j in _sort_network(n_lvl):
        a, b, ia, ib = vals[i], vals[j], ids[i], ids[j]
        swap = b > a
        if not (untouched[i] and untouched[j]):
            swap = swap | ((b == a) & (ib < ia))
        vals[i], vals[j] = jnp.maximum(a, b), jnp.minimum(a, b)
        ids[i], ids[j] = jnp.where(swap, ib, ia), jnp.where(swap, ia, ib)
        untouched[i] = untouched[j] = False
    return vals, ids


def _pop_top(vals, ids, k):
    vals, ids = list(vals), list(ids)
    out_v, out_i = [], []
    for it in range(k):
        m = jnp.max(vals[0], axis=0, keepdims=True)
        pick = jnp.min(jnp.where(vals[0] == m, ids[0], BIG_ID), axis=0, keepdims=True)
        sel = ids[0] == pick
        out_v.append(m)
        out_i.append(pick)
        for l in range(min(len(vals) - 1, k - 1 - it)):
            vals[l] = jnp.where(sel, vals[l + 1], vals[l])
            ids[l] = jnp.where(sel, ids[l + 1], ids[l])
    return jnp.concatenate(out_v, axis=0), jnp.concatenate(out_i, axis=0)


def _pair_top(s1, i1, s2, i2):
    k = PEER_TOPK
    tm = s1.shape[1]
    sub = lax.broadcasted_iota(jnp.int32, (SUBLANES, tm), 0)
    s2lo, i2lo = s2[0:SUBLANES], i2[0:SUBLANES]
    cand, expert = [], []
    for l in range(k):
        ok = (sub + 1) * (l + 1) <= k
        cand.append(jnp.where(ok, s1[l:l + 1] + s2lo, NEG_INF))
        expert.append(i1[l:l + 1] * PEER_NKEYS + i2lo)
    single = s1[0:1] + s2[SUBLANES:k]
    single_expert = i1[0:1] * PEER_NKEYS + i2[SUBLANES:k]
    single_flat = sub + SUBLANES
    pops = jnp.zeros((SUBLANES, tm), jnp.int32)
    out_s, out_e = [], []
    for it in range(k):
        flat = pops * k + sub
        m = jnp.max(jnp.maximum(cand[0], single), axis=0, keepdims=True)
        pick = jnp.min(jnp.minimum(jnp.where(cand[0] == m, flat, BIG_ID),
                                   jnp.where(single == m, single_flat, BIG_ID)), axis=0, keepdims=True)
        sel = flat == pick
        sel_single = single_flat == pick
        out_s.append(m)
        out_e.append(jnp.sum(jnp.where(sel, expert[0], 0) + jnp.where(sel_single, single_expert, 0),
                             axis=0, keepdims=True))
        single = jnp.where(sel_single, NEG_INF, single)
        pops = jnp.where(sel, pops + 1, pops)
        for l in range(k - 1 - it):
            cand[l] = jnp.where(sel, cand[l + 1], cand[l])
            expert[l] = jnp.where(sel, expert[l + 1], expert[l])
    return jnp.concatenate(out_s, axis=0), jnp.concatenate(out_e, axis=0)


def _route_body(hq_ref, wq_ref, keys_ref, idx_ref, gate_ref):
    q = jnp.dot(hq_ref[...], wq_ref[...], preferred_element_type=F32)
    gates, experts = [], []
    for h in range(PEER_HEADS):
        tops = []
        for p in range(2):
            c0 = (h * 2 + p) * PEER_HALF
            qs = q[:, c0:c0 + PEER_HALF].astype(BF16)
            s_t = lax.dot_general(keys_ref[h, p], qs, (((1,), (1,)), ((), ())),
                                  preferred_element_type=F32)
            tops.append(_pop_top(*_sorted_levels(s_t), PEER_TOPK))
        (s1, i1), (s2, i2) = tops
        sc, picked = _pair_top(s1, i1, s2, i2)
        e = jnp.exp(sc - sc[0:1])
        gates.append(e / jnp.sum(e, axis=0, keepdims=True))
        experts.append(picked)
    gate_ref[...] = jnp.concatenate(gates, axis=0).T
    idx_ref[...] = jnp.concatenate(experts, axis=0).T


def _route(hq, wq, keys):
    n, d = hq.shape
    tm = 256
    return pl.pallas_call(
        _route_body,
        grid=(n // tm,),
        in_specs=[pl.BlockSpec((tm, d), lambda i: (i, 0)),
                  pl.BlockSpec((d, PEER_HEADS * PEER_QDIM), lambda i: (0, 0)),
                  pl.BlockSpec((PEER_HEADS, 2, PEER_NKEYS, PEER_HALF), lambda i: (0, 0, 0, 0))],
        out_specs=[pl.BlockSpec((tm, PEER_PICKS), lambda i: (i, 0)),
                   pl.BlockSpec((tm, PEER_PICKS), lambda i: (i, 0))],
        out_shape=[jax.ShapeDtypeStruct((n, PEER_PICKS), jnp.int32),
                   jax.ShapeDtypeStruct((n, PEER_PICKS), F32)],
        compiler_params=_params("parallel"),
        name="peer_route",
    )(hq, wq, keys)


def _final_body(x1_ref, f_ref, m_ref, g_ref, b_ref, o_ref):
    m = m_ref[0]
    o_ref[0] = _layer_norm(DEEPNORM_ALPHA * x1_ref[0] + m[5:6] * f_ref[0], g_ref[...], b_ref[...])


def _final(x1, ffn, mods, ln_g, ln_b):
    b, t, d = x1.shape
    tm = min(t, 512)
    tok = lambda i, j: (i, j, 0)
    return pl.pallas_call(
        _final_body,
        grid=(b, t // tm),
        in_specs=[pl.BlockSpec((1, tm, d), tok), pl.BlockSpec((1, tm, d), tok), _mod_spec(mods),
                  pl.BlockSpec((1, d), lambda i, j: (0, 0)), pl.BlockSpec((1, d), lambda i, j: (0, 0))],
        out_specs=pl.BlockSpec((1, tm, d), tok),
        out_shape=jax.ShapeDtypeStruct(x1.shape, F32),
        compiler_params=_params("parallel", "parallel"),
        name="final_ln",
    )(x1, ffn, mods, ln_g.reshape(1, d), ln_b.reshape(1, d))


N_EXPERTS = PEER_NKEYS * PEER_NKEYS
GATE_ROWS = 4
GATE_TOKENS = 32
PEER_TB = 1024
PEER_EC = 2048


def _gate_matrix(idx, gates):
    n = idx.shape[0]
    info = plsc.get_sparse_core_info()
    n_workers = info.num_cores * info.num_subcores
    lanes = info.num_lanes
    per_w = n // n_workers
    assert n % (n_workers * GATE_TOKENS) == 0 and GATE_TOKENS % GATE_ROWS == 0 and PEER_PICKS % lanes == 0
    vecs = PEER_PICKS // lanes
    mesh = plsc.VectorSubcoreMesh(core_axis_name="core", subcore_axis_name="subcore")

    @functools.partial(
        pl.kernel, out_type=jax.ShapeDtypeStruct((n, N_EXPERTS), F32), mesh=mesh,
        scratch_types=[pltpu.VMEM((GATE_TOKENS * PEER_PICKS,), jnp.int32),
                       pltpu.VMEM((GATE_TOKENS * PEER_PICKS,), F32)]
                      + [pltpu.VMEM((N_EXPERTS,), F32)] * GATE_ROWS
                      + [pltpu.SemaphoreType.DMA] * GATE_ROWS,
        compiler_params=pltpu.CompilerParams(needs_layout_passes=False), name="sc_gate_matrix")
    def run(idx_hbm, gate_hbm, out_hbm, idx_v, g_v, *bufs):
        rows, sems = bufs[:GATE_ROWS], bufs[GATE_ROWS:]
        wid = lax.axis_index("subcore") * info.num_cores + lax.axis_index("core")
        tok_base = wid * per_w
        zeros = jnp.zeros((lanes,), F32)

        @pl.loop(0, N_EXPERTS // lanes)
        def _(i):
            for r in rows:
                r[pl.ds(pl.multiple_of(i * lanes, lanes), lanes)] = zeros

        @pl.loop(0, per_w // GATE_TOKENS)
        def _(blk):
            tok0 = tok_base + blk * GATE_TOKENS
            k_at = pl.ds(pl.multiple_of(tok0 * PEER_PICKS, GATE_TOKENS * PEER_PICKS), GATE_TOKENS * PEER_PICKS)
            pltpu.sync_copy(idx_hbm.at[k_at], idx_v)
            pltpu.sync_copy(gate_hbm.at[k_at], g_v)

            @pl.loop(0, GATE_TOKENS // GATE_ROWS)
            def _(q):
                def picks(b, c):
                    return pl.ds(pl.multiple_of(((q * GATE_ROWS + b) * vecs + c) * lanes, lanes), lanes)

                for b in range(GATE_ROWS):
                    for c in range(vecs):
                        plsc.addupdate_scatter(rows[b], [idx_v[picks(b, c)]], g_v[picks(b, c)])
                    pltpu.make_async_copy(rows[b], out_hbm.at[tok0 + q * GATE_ROWS + b], sems[b]).start()
                for b in range(GATE_ROWS):
                    pltpu.make_async_copy(rows[b], out_hbm.at[tok0 + q * GATE_ROWS + b], sems[b]).wait()
                    for c in range(vecs):
                        plsc.store_scatter(rows[b], [idx_v[picks(b, c)]], zeros)

    return run(idx.reshape(-1), gates.reshape(-1))


def _dense_body(x_ref, u_ref, v_ref, g_ref, o_ref, acc):
    j = pl.program_id(1)

    @pl.when(j == 0)
    def _():
        acc[...] = jnp.zeros_like(acc)

    s = lax.dot_general(x_ref[...], u_ref[...], (((1,), (1,)), ((), ())), preferred_element_type=F32)
    g = g_ref[...]
    a = jnp.where(g != 0.0, jax.nn.gelu(s) * g, 0.0).astype(BF16)
    acc[...] += jnp.dot(a, v_ref[...], preferred_element_type=F32)

    @pl.when(j == pl.num_programs(1) - 1)
    def _():
        o_ref[...] = acc[...]


def _experts(hq, gate_mat, u_tab, v_tab, layer):
    n, d = hq.shape
    tb = min(PEER_TB, n)
    return pl.pallas_call(
        _dense_body,
        grid=(n // tb, N_EXPERTS // PEER_EC),
        in_specs=[pl.BlockSpec((tb, d), lambda i, j: (i, 0)),
                  pl.BlockSpec((None, PEER_EC, d), lambda i, j: (layer, j, 0)),
                  pl.BlockSpec((None, PEER_EC, d), lambda i, j: (layer, j, 0)),
                  pl.BlockSpec((tb, PEER_EC), lambda i, j: (i, j))],
        out_specs=pl.BlockSpec((tb, d), lambda i, j: (i, 0)),
        out_shape=jax.ShapeDtypeStruct((n, d), F32),
        scratch_shapes=[pltpu.VMEM((tb, d), F32)],
        compiler_params=_params("parallel", "arbitrary"),
        name="peer_dense",
    )(hq, u_tab, v_tab, gate_mat)


def _grid_pos_embed(rows):
    t = jnp.arange(rows * GRID_W)
    r = (t // GRID_W).astype(F32)
    col = (t % GRID_W).astype(F32)
    n_freq = D_MODEL // 4
    omega = 1.0 / (10000.0 ** (jnp.arange(n_freq, dtype=F32) / n_freq))

    def enc(p):
        ang = p[:, None] * omega[None, :]
        return jnp.concatenate([jnp.sin(ang), jnp.cos(ang)], -1)
    return jnp.concatenate([enc(r), enc(col)], -1)


def _layer(x, mods, h0, p):
    b, t, d = x.shape
    z = _mix_in(x, mods, p["w_in"])
    rnn, st = _rnn(z, h0, p["conv_w"], p["conv_b"], p["wa"], p["wx"], p["b_a"], p["b_x"], p["lam"])
    pool = _pool(z, p["w_pool"], p["pool_scale"])
    x1, hq = _mix_out(rnn, pool, x, mods, p["w_out"], p["ln1_g"], p["ln1_b"])
    hq = hq.reshape(b * t, d)
    idx, gates = _route(hq, p["wq"], p["keys"])
    gate_mat = _gate_matrix(idx, gates)
    ffn = _experts(hq, gate_mat, p["peer_u"], p["peer_v"], p["layer"])
    x2 = _final(x1, ffn.reshape(b, t, d), mods, p["ln2_g"], p["ln2_b"])
    return x2, st


def kernel(x_prompt, x_sample, state_rglru, c, c_ctx, w_mod, b_mod, w_in, conv_w, conv_b,
           w_rg_a, b_rg_a, w_rg_x, b_rg_x, lru_lambda, w_pool, pool_scale, w_out,
           ln1_g, ln1_b, ln2_g, ln2_b, peer_wq, peer_keys, peer_u, peer_v):
    n_req = c.shape[0]
    cvecs = jnp.zeros((MOD_ROWS, D_MODEL), F32).at[:n_req].set(c).at[n_req].set(c_ctx)
    mod = _modulation(cvecs, w_mod, b_mod).reshape(DEPTH, MOD_ROWS, N_MOD, D_MODEL)
    mod = jnp.pad(mod, ((0, 0), (0, 0), (0, SUBLANES - N_MOD), (0, 0)))

    rows = x_sample.shape[1] // GRID_W
    xs = _add_pos(x_sample, _grid_pos_embed(rows))
    xp = x_prompt
    zero_state = jnp.zeros((x_prompt.shape[0], 2, D_RNN), F32)
    ctx_states = []
    u_bf16, v_bf16 = peer_u.astype(BF16), peer_v.astype(BF16)
    for l in range(DEPTH):
        p = dict(w_in=w_in[l].astype(BF16), conv_w=conv_w[l], conv_b=conv_b[l],
                 wa=_block_diag_pairs(w_rg_a[l]), wx=_block_diag_pairs(w_rg_x[l]),
                 b_a=b_rg_a[l], b_x=b_rg_x[l], lam=lru_lambda[l],
                 w_pool=w_pool[l].astype(BF16), pool_scale=pool_scale[l], w_out=w_out[l].astype(BF16),
                 ln1_g=ln1_g[l], ln1_b=ln1_b[l], ln2_g=ln2_g[l], ln2_b=ln2_b[l],
                 wq=peer_wq[l].astype(BF16), keys=peer_keys[l].astype(BF16),
                 peer_u=u_bf16, peer_v=v_bf16, layer=l)
        for stream in ("ctx", "latent") if l % 2 == 0 else ("latent", "ctx"):
            if stream == "ctx":
                xp, st = _layer(xp, mod[l, n_req:n_req + 1], zero_state, p)
                ctx_states.append(st)
            else:
                xs, _ = _layer(xs, mod[l, :n_req], state_rglru[:, l], p)
    return xp, xs, jnp.stack(ctx_states, axis=1)
```

```python
import functools
import math

import jax
import jax.numpy as jnp
from jax import lax
from jax.experimental import pallas as pl
from jax.experimental.pallas import tpu as pltpu
from jax.experimental.pallas import tpu_sc as plsc

F32 = jnp.float32
BF16 = jnp.bfloat16

D_MODEL = 1024
DEPTH = 2
GRID_W = 64
D_RNN = 512
N_RNN_HEADS = 8
RNN_HEAD_DIM = D_RNN // N_RNN_HEADS
CONV_W = 4
LRU_C = 8.0
D_POOL = 512
POOL_WINDOWS = (2, 4, 8, 16)
POOL_GROUP_DIM = D_POOL // len(POOL_WINDOWS)
D_IN = 2 * D_RNN + D_POOL
PEER_HEADS = 8
PEER_NKEYS = 128
PEER_TOPK = 16
PEER_QDIM = 256
PEER_HALF = PEER_QDIM // 2
PEER_PICKS = PEER_HEADS * PEER_TOPK
N_MOD = 6
DEEPNORM_ALPHA = (2 * DEPTH) ** 0.25
LN_EPS = 1e-5

LANES = 128
SUBLANES = 8
VMEM_LIMIT_BYTES = 56 * 1024 * 1024
MOD_ROWS = 16
ROW_CHUNK = 256


def _params(*sem):
    return pltpu.CompilerParams(dimension_semantics=sem, vmem_limit_bytes=VMEM_LIMIT_BYTES)


def _layer_norm(y, g, b):
    mu = jnp.mean(y, -1, keepdims=True)
    var = jnp.mean(jnp.square(y - mu), -1, keepdims=True)
    return (y - mu) * lax.rsqrt(var + LN_EPS) * g + b


def _mod_body(c_ref, w_ref, b_ref, o_ref):
    c = c_ref[...]
    s = (c * jax.nn.sigmoid(c)).astype(BF16)
    o_ref[0] = jnp.dot(s, w_ref[0].astype(BF16), preferred_element_type=F32) + b_ref[0]


def _modulation(cvecs, w_mod, b_mod):
    n_out = N_MOD * D_MODEL
    tn = 1536
    return pl.pallas_call(
        _mod_body,
        grid=(DEPTH, n_out // tn),
        in_specs=[pl.BlockSpec((MOD_ROWS, D_MODEL), lambda l, j: (0, 0)),
                  pl.BlockSpec((1, D_MODEL, tn), lambda l, j: (l, 0, j)),
                  pl.BlockSpec((1, 1, tn), lambda l, j: (l, 0, j))],
        out_specs=pl.BlockSpec((1, MOD_ROWS, tn), lambda l, j: (l, 0, j)),
        out_shape=jax.ShapeDtypeStruct((DEPTH, MOD_ROWS, n_out), F32),
        compiler_params=_params("parallel", "parallel"),
        name="modulation",
    )(cvecs, w_mod, b_mod.reshape(DEPTH, 1, n_out))


def _add_pos_body(x_ref, p_ref, o_ref):
    o_ref[0] = x_ref[0] + p_ref[...]


def _add_pos(x, pos):
    b, t, d = x.shape
    tm = min(t, 512)
    return pl.pallas_call(
        _add_pos_body,
        grid=(b, t // tm),
        in_specs=[pl.BlockSpec((1, tm, d), lambda i, j: (i, j, 0)),
                  pl.BlockSpec((tm, d), lambda i, j: (j, 0))],
        out_specs=pl.BlockSpec((1, tm, d), lambda i, j: (i, j, 0)),
        out_shape=jax.ShapeDtypeStruct(x.shape, x.dtype),
        compiler_params=_params("parallel", "parallel"),
        name="add_pos",
    )(x, pos)


def _mix_in_body(x_ref, m_ref, w_ref, z_ref):
    m = m_ref[0]
    h = x_ref[0] * (1.0 + m[1:2]) + m[0:1]
    z = jnp.dot(h.astype(BF16), w_ref[...], preferred_element_type=F32)
    for c in range(D_IN // LANES):
        z_ref[0, c] = z[:, c * LANES:(c + 1) * LANES]


def _mod_spec(mods):
    if mods.shape[0] == 1:
        return pl.BlockSpec((1, SUBLANES, D_MODEL), lambda i, j: (0, 0, 0))
    return pl.BlockSpec((1, SUBLANES, D_MODEL), lambda i, j: (i, 0, 0))


def _mix_in(x, mods, w_in):
    b, t, d = x.shape
    tm = min(t, 512)
    return pl.pallas_call(
        _mix_in_body,
        grid=(b, t // tm),
        in_specs=[pl.BlockSpec((1, tm, d), lambda i, j: (i, j, 0)),
                  _mod_spec(mods),
                  pl.BlockSpec((d, D_IN), lambda i, j: (0, 0))],
        out_specs=pl.BlockSpec((1, D_IN // LANES, tm, LANES), lambda i, j: (i, 0, j, 0)),
        out_shape=jax.ShapeDtypeStruct((b, D_IN // LANES, t, LANES), F32),
        compiler_params=_params("parallel", "parallel"),
        name="mix_in",
    )(x, mods, w_in)


PAD = SUBLANES
CARRY_PAD = 2 * SUBLANES


def _group_scan(a, u, reverse):
    row = lax.broadcasted_iota(jnp.int32, a.shape, 1)
    for s in (1, 2, 4):
        if reverse:
            keep = row <= SUBLANES - 1 - s
            shift = SUBLANES - s
        else:
            keep = row >= s
            shift = s
        a_sh = jnp.where(keep, pltpu.roll(a, shift, 1), 1.0)
        u_sh = jnp.where(keep, pltpu.roll(u, shift, 1), 0.0)
        u = a * u_sh + u
        a = a * a_sh
    return a, u


def _rnn_body(xr_ref, gr_ref, h0_ref, cw_ref, cb_ref, wa_ref, wx_ref, ba_ref, bx_ref, lam_ref,
              out_ref, st_ref, xpad, a_s, u_s, a2_s, u2_s, e_s):
    t_len = xr_ref.shape[2]
    rc = min(ROW_CHUNK, t_len)
    zeros = jnp.zeros((PAD, LANES), F32)
    xpad[0:PAD] = zeros
    xpad[t_len + PAD:t_len + 2 * PAD] = zeros
    xpad[PAD:t_len + PAD] = xr_ref[0, 0]

    cw = cw_ref[...]
    lam = lam_ref[...]
    softplus_neg = jnp.maximum(-lam, 0.0) + jnp.log1p(jnp.exp(-jnp.abs(lam)))
    coef = -LRU_C * softplus_neg
    left = CONV_W // 2

    for c in range(t_len // rc):
        t0 = c * rc
        xc = cb_ref[...] + xpad[t0 + PAD - left:t0 + PAD - left + rc] * cw[0:1]
        for k in range(1, CONV_W):
            xc = xc + xpad[t0 + PAD - left + k:t0 + PAD - left + k + rc] * cw[k:k + 1]
        xcb = xc.astype(BF16)
        for d in range(2):
            r = jax.nn.sigmoid(jnp.dot(xcb, wa_ref[d, 0], preferred_element_type=F32) + ba_ref[d:d + 1])
            gi = jax.nn.sigmoid(jnp.dot(xcb, wx_ref[d, 0], preferred_element_type=F32) + bx_ref[d:d + 1])
            log_a = coef[d:d + 1] * r
            a = jnp.exp(log_a)
            u = jnp.sqrt(-jnp.tanh(log_a) * (a * a + 1.0)) * (gi * xc)
            a, u = _group_scan(a.reshape(rc // SUBLANES, SUBLANES, LANES),
                               u.reshape(rc // SUBLANES, SUBLANES, LANES), reverse=(d == 1))
            a_s[d, t0:t0 + rc] = a.reshape(rc, LANES)
            u_s[d, t0:t0 + rc] = u.reshape(rc, LANES)

    n_groups = t_len // SUBLANES
    n_super = n_groups // SUBLANES
    for d in range(2):
        edge = pl.ds(SUBLANES - 1 if d == 0 else 0, n_groups, stride=SUBLANES)
        a2, u2 = _group_scan(a_s[d, edge].reshape(n_super, SUBLANES, LANES),
                             u_s[d, edge].reshape(n_super, SUBLANES, LANES), reverse=(d == 1))
        a2_s[d] = a2.reshape(n_groups, LANES)
        u2_s[d] = u2.reshape(n_groups, LANES)

    h0 = h0_ref[0]
    e_s[0, SUBLANES - 1:SUBLANES] = h0[0:1]
    e_s[1, n_groups:n_groups + 1] = h0[1:2]
    unroll = min(SUBLANES, n_super)

    def carry_step(i, carry):
        hf, hb = carry
        for j in range(unroll):
            gf = pl.multiple_of((i * unroll + j) * SUBLANES, SUBLANES)
            gb = pl.multiple_of((n_super - 1 - i * unroll - j) * SUBLANES, SUBLANES)
            h_f = u2_s[0, pl.ds(gf, SUBLANES)] + a2_s[0, pl.ds(gf, SUBLANES)] * hf
            h_b = u2_s[1, pl.ds(gb, SUBLANES)] + a2_s[1, pl.ds(gb, SUBLANES)] * hb
            e_s[0, pl.ds(gf + SUBLANES, SUBLANES)] = h_f
            e_s[1, pl.ds(gb, SUBLANES)] = h_b
            hf = h_f[SUBLANES - 1:SUBLANES]
            hb = h_b[0:1]
        return hf, hb

    hf, hb = lax.fori_loop(0, n_super // unroll, carry_step, (h0[0:1], h0[1:2]))
    st_ref[0, 0:1] = hf
    st_ref[0, 1:2] = hb

    gc = rc // SUBLANES
    for c in range(t_len // rc):
        t0 = c * rc
        g0 = t0 // SUBLANES
        hs = []
        for d, off in ((0, SUBLANES - 1), (1, 1)):
            entry = e_s[d, g0 + off:g0 + off + gc]
            hs.append(u_s[d, t0:t0 + rc].reshape(gc, SUBLANES, LANES)
                      + a_s[d, t0:t0 + rc].reshape(gc, SUBLANES, LANES) * entry[:, None, :])
        hsum = (hs[0] + hs[1]).reshape(rc, LANES)
        out_ref[0, 0, t0:t0 + rc] = (hsum * jax.nn.gelu(gr_ref[0, 0, t0:t0 + rc])).astype(BF16)


def _rnn(z, h0, conv_w, conv_b, wa, wx, b_a, b_x, lam):
    b, _, t, _ = z.shape
    nblk = D_RNN // LANES
    return pl.pallas_call(
        _rnn_body,
        grid=(b, nblk),
        in_specs=[pl.BlockSpec((1, 1, t, LANES), lambda i, j: (i, j, 0, 0)),
                  pl.BlockSpec((1, 1, t, LANES), lambda i, j: (i, nblk + j, 0, 0)),
                  pl.BlockSpec((1, 2, LANES), lambda i, j: (i, 0, j)),
                  pl.BlockSpec((CONV_W, LANES), lambda i, j: (0, j)),
                  pl.BlockSpec((1, LANES), lambda i, j: (0, j)),
                  pl.BlockSpec((2, 1, LANES, LANES), lambda i, j: (0, j, 0, 0)),
                  pl.BlockSpec((2, 1, LANES, LANES), lambda i, j: (0, j, 0, 0)),
                  pl.BlockSpec((2, LANES), lambda i, j: (0, j)),
                  pl.BlockSpec((2, LANES), lambda i, j: (0, j)),
                  pl.BlockSpec((2, LANES), lambda i, j: (0, j))],
        out_specs=[pl.BlockSpec((1, 1, t, LANES), lambda i, j: (i, j, 0, 0)),
                   pl.BlockSpec((1, 2, LANES), lambda i, j: (i, 0, j))],
        out_shape=[jax.ShapeDtypeStruct((b, nblk, t, LANES), BF16),
                   jax.ShapeDtypeStruct((b, 2, D_RNN), F32)],
        scratch_shapes=[pltpu.VMEM((t + 2 * PAD, LANES), F32),
                        pltpu.VMEM((2, t, LANES), F32),
                        pltpu.VMEM((2, t, LANES), F32),
                        pltpu.VMEM((2, t // SUBLANES, LANES), F32),
                        pltpu.VMEM((2, t // SUBLANES, LANES), F32),
                        pltpu.VMEM((2, t // SUBLANES + CARRY_PAD, LANES), F32)],
        compiler_params=_params("parallel", "parallel"),
        name="rglru",
    )(z, z, h0, conv_w, conv_b.reshape(1, D_RNN), wa, wx, b_a, b_x, lam)


def _block_diag_pairs(w):
    per = LANES // RNN_HEAD_DIM
    w = w.reshape(2, D_RNN // LANES, per, RNN_HEAD_DIM, RNN_HEAD_DIM)
    out = jnp.zeros((2, D_RNN // LANES, LANES, LANES), w.dtype)
    for p in range(per):
        sl = slice(p * RNN_HEAD_DIM, (p + 1) * RNN_HEAD_DIM)
        out = out.at[:, :, sl, sl].set(w[:, :, p])
    return out.astype(BF16)


POOL_PAD = 16


def _pool_body(xq_ref, w_ref, sc_ref, out_ref, ppad):
    t_len = xq_ref.shape[2]
    rc = min(ROW_CHUNK, t_len)
    zeros = jnp.zeros((POOL_PAD, D_POOL), F32)
    ppad[0:POOL_PAD] = zeros
    ppad[t_len + POOL_PAD:t_len + 2 * POOL_PAD] = zeros
    for g in range(len(POOL_WINDOWS)):
        ppad[POOL_PAD:t_len + POOL_PAD, g * POOL_GROUP_DIM:(g + 1) * POOL_GROUP_DIM] = xq_ref[0, g]
    for c in range(t_len // rc):
        t0 = c * rc
        tpos = t0 + lax.broadcasted_iota(jnp.int32, (rc, POOL_GROUP_DIM), 0)
        for g, w in enumerate(POOL_WINDOWS):
            cols = slice(g * POOL_GROUP_DIM, (g + 1) * POOL_GROUP_DIM)
            half = w // 2
            base = t0 + POOL_PAD - half
            s = ppad[base:base + rc, cols]
            for k in range(1, w):
                s = s + ppad[base + k:base + k + rc, cols]
            cnt = (jnp.minimum(tpos + half, t_len) - jnp.maximum(tpos - half, 0)).astype(F32)
            pooled = s / cnt - ppad[t0 + POOL_PAD:t0 + POOL_PAD + rc, cols]
            y = jnp.dot(pooled.astype(BF16), w_ref[g], preferred_element_type=F32)
            out_ref[0, g, t0:t0 + rc] = (y * sc_ref[:, cols]).astype(BF16)


def _pool(z, w_pool, pool_scale):
    b, _, t, _ = z.shape
    n_grp = len(POOL_WINDOWS)
    return pl.pallas_call(
        _pool_body,
        grid=(b,),
        in_specs=[pl.BlockSpec((1, n_grp, t, POOL_GROUP_DIM), lambda i: (i, 2 * D_RNN // D_POOL, 0, 0)),
                  pl.BlockSpec((len(POOL_WINDOWS), POOL_GROUP_DIM, POOL_GROUP_DIM), lambda i: (0, 0, 0)),
                  pl.BlockSpec((1, D_POOL), lambda i: (0, 0))],
        out_specs=pl.BlockSpec((1, n_grp, t, POOL_GROUP_DIM), lambda i: (i, 0, 0, 0)),
        out_shape=jax.ShapeDtypeStruct((b, n_grp, t, POOL_GROUP_DIM), BF16),
        scratch_shapes=[pltpu.VMEM((t + 2 * POOL_PAD, D_POOL), F32)],
        compiler_params=_params("parallel"),
        name="pool",
    )(z, w_pool, pool_scale.reshape(1, D_POOL))


def _mix_out_body(rnn_ref, pool_ref, x_ref, m_ref, w_ref, g_ref, b_ref, x1_ref, hq_ref):
    m = m_ref[0]
    rnn = jnp.concatenate([rnn_ref[0, c] for c in range(rnn_ref.shape[1])], axis=1)
    pool = jnp.concatenate([pool_ref[0, c] for c in range(pool_ref.shape[1])], axis=1)
    mix = (jnp.dot(rnn, w_ref[0:D_RNN], preferred_element_type=F32)
           + jnp.dot(pool, w_ref[D_RNN:D_RNN + D_POOL], preferred_element_type=F32))
    x1 = _layer_norm(DEEPNORM_ALPHA * x_ref[0] + m[2:3] * mix, g_ref[...], b_ref[...])
    x1_ref[0] = x1
    hq_ref[0] = (x1 * (1.0 + m[4:5]) + m[3:4]).astype(BF16)


def _mix_out(rnn, pool, x, mods, w_out, ln_g, ln_b):
    b, t, d = x.shape
    tm = min(t, 512)
    tok = lambda i, j: (i, j, 0)
    return pl.pallas_call(
        _mix_out_body,
        grid=(b, t // tm),
        in_specs=[pl.BlockSpec((1, D_RNN // LANES, tm, LANES), lambda i, j: (i, 0, j, 0)),
                  pl.BlockSpec((1, D_POOL // LANES, tm, LANES), lambda i, j: (i, 0, j, 0)),
                  pl.BlockSpec((1, tm, d), tok),
                  _mod_spec(mods),
                  pl.BlockSpec((D_RNN + D_POOL, d), lambda i, j: (0, 0)),
                  pl.BlockSpec((1, d), lambda i, j: (0, 0)),
                  pl.BlockSpec((1, d), lambda i, j: (0, 0))],
        out_specs=[pl.BlockSpec((1, tm, d), tok), pl.BlockSpec((1, tm, d), tok)],
        out_shape=[jax.ShapeDtypeStruct(x.shape, F32), jax.ShapeDtypeStruct(x.shape, BF16)],
        compiler_params=_params("parallel", "parallel"),
        name="mix_out",
    )(rnn, pool, x, mods, w_out, ln_g.reshape(1, d), ln_b.reshape(1, d))


NEG_INF = float("-inf")


BIG_ID = 2 ** 30


def _sort_network(n):
    def merge(lo, hi, r):
        step = r * 2
        if step < hi - lo:
            yield from merge(lo, hi, step)
            yield from merge(lo + r, hi, step)
            for i in range(lo + r, hi - r, step):
                yield (i, i + r)
        else:
            yield (lo, lo + r)

    def sort(lo, hi):
        if hi - lo >= 1:
            mid = lo + (hi - lo) // 2
            yield from sort(lo, mid)
            yield from sort(mid + 1, hi)
            yield from merge(lo, hi, 1)

    return list(sort(0, n - 1))


def _sorted_levels(s_t):
    n_lvl = s_t.shape[0] // SUBLANES
    sub = lax.broadcasted_iota(jnp.int32, (SUBLANES, s_t.shape[1]), 0)
    vals = [s_t[l * SUBLANES:(l + 1) * SUBLANES] for l in range(n_lvl)]
    ids = [sub + l * SUBLANES for l in range(n_lvl)]
    untouched = [True] * n_lvl
    for i, j in _sort_network(n_lvl):
        a, b, ia, ib = vals[i], vals[j], ids[i], ids[j]
        swap = b > a
        if not (untouched[i] and untouched[j]):
            swap = swap | ((b == a) & (ib < ia))
        vals[i], vals[j] = jnp.maximum(a, b), jnp.minimum(a, b)
        ids[i], ids[j] = jnp.where(swap, ib, ia), jnp.where(swap, ia, ib)
        untouched[i] = untouched[j] = False
    return vals, ids


def _pop_top(vals, ids, k):
    vals, ids = list(vals), list(ids)
    out_v, out_i = [], []
    for it in range(k):
        m = jnp.max(vals[0], axis=0, keepdims=True)
        pick = jnp.min(jnp.where(vals[0] == m, ids[0], BIG_ID), axis=0, keepdims=True)
        sel = ids[0] == pick
        out_v.append(m)
        out_i.append(pick)
        for l in range(min(len(vals) - 1, k - 1 - it)):
            vals[l] = jnp.where(sel, vals[l + 1], vals[l])
            ids[l] = jnp.where(sel, ids[l + 1], ids[l])
    return jnp.concatenate(out_v, axis=0), jnp.concatenate(out_i, axis=0)


def _pair_top(s1, i1, s2, i2):
    k = PEER_TOPK
    tm = s1.shape[1]
    sub = lax.broadcasted_iota(jnp.int32, (SUBLANES, tm), 0)
    s2lo, i2lo = s2[0:SUBLANES], i2[0:SUBLANES]
    cand, expert = [], []
    for l in range(k):
        ok = (sub + 1) * (l + 1) <= k
        cand.append(jnp.where(ok, s1[l:l + 1] + s2lo, NEG_INF))
        expert.append(i1[l:l + 1] * PEER_NKEYS + i2lo)
    single = s1[0:1] + s2[SUBLANES:k]
    single_expert = i1[0:1] * PEER_NKEYS + i2[SUBLANES:k]
    single_flat = sub + SUBLANES
    pops = jnp.zeros((SUBLANES, tm), jnp.int32)
    out_s, out_e = [], []
    for it in range(k):
        flat = pops * k + sub
        m = jnp.max(jnp.maximum(cand[0], single), axis=0, keepdims=True)
        pick = jnp.min(jnp.minimum(jnp.where(cand[0] == m, flat, BIG_ID),
                                   jnp.where(single == m, single_flat, BIG_ID)), axis=0, keepdims=True)
        sel = flat == pick
        sel_single = single_flat == pick
        out_s.append(m)
        out_e.append(jnp.sum(jnp.where(sel, expert[0], 0) + jnp.where(sel_single, single_expert, 0),
                             axis=0, keepdims=True))
        single = jnp.where(sel_single, NEG_INF, single)
        pops = jnp.where(sel, pops + 1, pops)
        for l in range(k - 1 - it):
            cand[l] = jnp.where(sel, cand[l + 1], cand[l])
            expert[l] = jnp.where(sel, expert[l + 1], expert[l])
    return jnp.concatenate(out_s, axis=0), jnp.concatenate(out_e, axis=0)


def _route_body(hq_ref, wq_ref, keys_ref, idx_ref, gate_ref):
    q = jnp.dot(hq_ref[...], wq_ref[...], preferred_element_type=F32)
    gates, experts = [], []
    for h in range(PEER_HEADS):
        tops = []
        for p in range(2):
            c0 = (h * 2 + p) * PEER_HALF
            qs = q[:, c0:c0 + PEER_HALF].astype(BF16)
            s_t = lax.dot_general(keys_ref[h, p], qs, (((1,), (1,)), ((), ())),
                                  preferred_element_type=F32)
            tops.append(_pop_top(*_sorted_levels(s_t), PEER_TOPK))
        (s1, i1), (s2, i2) = tops
        sc, picked = _pair_top(s1, i1, s2, i2)
        e = jnp.exp(sc - sc[0:1])
        gates.append(e / jnp.sum(e, axis=0, keepdims=True))
        experts.append(picked)
    gate_ref[...] = jnp.concatenate(gates, axis=0).T
    idx_ref[...] = jnp.concatenate(experts, axis=0).T


def _route(hq, wq, keys):
    n, d = hq.shape
    tm = 256
    return pl.pallas_call(
        _route_body,
        grid=(n // tm,),
        in_specs=[pl.BlockSpec((tm, d), lambda i: (i, 0)),
                  pl.BlockSpec((d, PEER_HEADS * PEER_QDIM), lambda i: (0, 0)),
                  pl.BlockSpec((PEER_HEADS, 2, PEER_NKEYS, PEER_HALF), lambda i: (0, 0, 0, 0))],
        out_specs=[pl.BlockSpec((tm, PEER_PICKS), lambda i: (i, 0)),
                   pl.BlockSpec((tm, PEER_PICKS), lambda i: (i, 0))],
        out_shape=[jax.ShapeDtypeStruct((n, PEER_PICKS), jnp.int32),
                   jax.ShapeDtypeStruct((n, PEER_PICKS), F32)],
        compiler_params=_params("parallel"),
        name="peer_route",
    )(hq, wq, keys)


def _final_body(x1_ref, f_ref, m_ref, g_ref, b_ref, o_ref):
    m = m_ref[0]
    o_ref[0] = _layer_norm(DEEPNORM_ALPHA * x1_ref[0] + m[5:6] * f_ref[0], g_ref[...], b_ref[...])


def _final(x1, ffn, mods, ln_g, ln_b):
    b, t, d = x1.shape
    tm = min(t, 512)
    tok = lambda i, j: (i, j, 0)
    return pl.pallas_call(
        _final_body,
        grid=(b, t // tm),
        in_specs=[pl.BlockSpec((1, tm, d), tok), pl.BlockSpec((1, tm, d), tok), _mod_spec(mods),
                  pl.BlockSpec((1, d), lambda i, j: (0, 0)), pl.BlockSpec((1, d), lambda i, j: (0, 0))],
        out_specs=pl.BlockSpec((1, tm, d), tok),
        out_shape=jax.ShapeDtypeStruct(x1.shape, F32),
        compiler_params=_params("parallel", "parallel"),
        name="final_ln",
    )(x1, ffn, mods, ln_g.reshape(1, d), ln_b.reshape(1, d))


N_EXPERTS = PEER_NKEYS * PEER_NKEYS
GATE_ROWS = 4
GATE_TOKENS = 32
PEER_TB = 1024
PEER_EC = 2048


def _gate_matrix(idx, gates):
    n = idx.shape[0]
    info = plsc.get_sparse_core_info()
    n_workers = info.num_cores * info.num_subcores
    lanes = info.num_lanes
    per_w = n // n_workers
    assert n % (n_workers * GATE_TOKENS) == 0 and GATE_TOKENS % GATE_ROWS == 0 and PEER_PICKS % lanes == 0
    vecs = PEER_PICKS // lanes
    mesh = plsc.VectorSubcoreMesh(core_axis_name="core", subcore_axis_name="subcore")

    @functools.partial(
        pl.kernel, out_type=jax.ShapeDtypeStruct((n, N_EXPERTS), F32), mesh=mesh,
        scratch_types=[pltpu.VMEM((GATE_TOKENS * PEER_PICKS,), jnp.int32),
                       pltpu.VMEM((GATE_TOKENS * PEER_PICKS,), F32)]
                      + [pltpu.VMEM((N_EXPERTS,), F32)] * GATE_ROWS
                      + [pltpu.SemaphoreType.DMA] * GATE_ROWS,
        compiler_params=pltpu.CompilerParams(needs_layout_passes=False), name="sc_gate_matrix")
    def run(idx_hbm, gate_hbm, out_hbm, idx_v, g_v, *bufs):
        rows, sems = bufs[:GATE_ROWS], bufs[GATE_ROWS:]
        wid = lax.axis_index("subcore") * info.num_cores + lax.axis_index("core")
        tok_base = wid * per_w
        zeros = jnp.zeros((lanes,), F32)

        @pl.loop(0, N_EXPERTS // lanes)
        def _(i):
            for r in rows:
                r[pl.ds(pl.multiple_of(i * lanes, lanes), lanes)] = zeros

        @pl.loop(0, per_w // GATE_TOKENS)
        def _(blk):
            tok0 = tok_base + blk * GATE_TOKENS
            k_at = pl.ds(pl.multiple_of(tok0 * PEER_PICKS, GATE_TOKENS * PEER_PICKS), GATE_TOKENS * PEER_PICKS)
            pltpu.sync_copy(idx_hbm.at[k_at], idx_v)
            pltpu.sync_copy(gate_hbm.at[k_at], g_v)

            @pl.loop(0, GATE_TOKENS // GATE_ROWS)
            def _(q):
                def picks(b, c):
                    return pl.ds(pl.multiple_of(((q * GATE_ROWS + b) * vecs + c) * lanes, lanes), lanes)

                for b in range(GATE_ROWS):
                    for c in range(vecs):
                        plsc.addupdate_scatter(rows[b], [idx_v[picks(b, c)]], g_v[picks(b, c)])
                    pltpu.make_async_copy(rows[b], out_hbm.at[tok0 + q * GATE_ROWS + b], sems[b]).start()
                for b in range(GATE_ROWS):
                    pltpu.make_async_copy(rows[b], out_hbm.at[tok0 + q * GATE_ROWS + b], sems[b]).wait()
                    for c in range(vecs):
                        plsc.store_scatter(rows[b], [idx_v[picks(b, c)]], zeros)

    return run(idx.reshape(-1), gates.reshape(-1))


def _dense_body(x_ref, u_ref, v_ref, g_ref, o_ref, acc):
    j = pl.program_id(1)

    @pl.when(j == 0)
    def _():
        acc[...] = jnp.zeros_like(acc)

    s = lax.dot_general(x_ref[...], u_ref[...], (((1,), (1,)), ((), ())), preferred_element_type=F32)
    g = g_ref[...]
    a = jnp.where(g != 0.0, jax.nn.gelu(s) * g, 0.0).astype(BF16)
    acc[...] += jnp.dot(a, v_ref[...], preferred_element_type=F32)

    @pl.when(j == pl.num_programs(1) - 1)
    def _():
        o_ref[...] = acc[...]


def _experts(hq, gate_mat, u_tab, v_tab, layer):
    n, d = hq.shape
    tb = min(PEER_TB, n)
    return pl.pallas_call(
        _dense_body,
        grid=(n // tb, N_EXPERTS // PEER_EC),
        in_specs=[pl.BlockSpec((tb, d), lambda i, j: (i, 0)),
                  pl.BlockSpec((None, PEER_EC, d), lambda i, j: (layer, j, 0)),
                  pl.BlockSpec((None, PEER_EC, d), lambda i, j: (layer, j, 0)),
                  pl.BlockSpec((tb, PEER_EC), lambda i, j: (i, j))],
        out_specs=pl.BlockSpec((tb, d), lambda i, j: (i, 0)),
        out_shape=jax.ShapeDtypeStruct((n, d), F32),
        scratch_shapes=[pltpu.VMEM((tb, d), F32)],
        compiler_params=_params("parallel", "arbitrary"),
        name="peer_dense",
    )(hq, u_tab, v_tab, gate_mat)


def _grid_pos_embed(rows):
    t = jnp.arange(rows * GRID_W)
    r = (t // GRID_W).astype(F32)
    col = (t % GRID_W).astype(F32)
    n_freq = D_MODEL // 4
    omega = 1.0 / (10000.0 ** (jnp.arange(n_freq, dtype=F32) / n_freq))

    def enc(p):
        ang = p[:, None] * omega[None, :]
        return jnp.concatenate([jnp.sin(ang), jnp.cos(ang)], -1)
    return jnp.concatenate([enc(r), enc(col)], -1)


def _layer(x, mods, h0, p):
    b, t, d = x.shape
    z = _mix_in(x, mods, p["w_in"])
    rnn, st = _rnn(z, h0, p["conv_w"], p["conv_b"], p["wa"], p["wx"], p["b_a"], p["b_x"], p["lam"])
    pool = _pool(z, p["w_pool"], p["pool_scale"])
    x1, hq = _mix_out(rnn, pool, x, mods, p["w_out"], p["ln1_g"], p["ln1_b"])
    hq = hq.reshape(b * t, d)
    idx, gates = _route(hq, p["wq"], p["keys"])
    gate_mat = _gate_matrix(idx, gates)
    ffn = _experts(hq, gate_mat, p["peer_u"], p["peer_v"], p["layer"])
    x2 = _final(x1, ffn.reshape(b, t, d), mods, p["ln2_g"], p["ln2_b"])
    return x2, st


def kernel(x_prompt, x_sample, state_rglru, c, c_ctx, w_mod, b_mod, w_in, conv_w, conv_b,
           w_rg_a, b_rg_a, w_rg_x, b_rg_x, lru_lambda, w_pool, pool_scale, w_out,
           ln1_g, ln1_b, ln2_g, ln2_b, peer_wq, peer_keys, peer_u, peer_v):
    n_req = c.shape[0]
    cvecs = jnp.zeros((MOD_ROWS, D_MODEL), F32).at[:n_req].set(c).at[n_req].set(c_ctx)
    mod = _modulation(cvecs, w_mod, b_mod).reshape(DEPTH, MOD_ROWS, N_MOD, D_MODEL)
    mod = jnp.pad(mod, ((0, 0), (0, 0), (0, SUBLANES - N_MOD), (0, 0)))

    rows = x_sample.shape[1] // GRID_W
    xs = _add_pos(x_sample, _grid_pos_embed(rows))
    xp = x_prompt
    zero_state = jnp.zeros((x_prompt.shape[0], 2, D_RNN), F32)
    ctx_states = []
    u_bf16, v_bf16 = peer_u.astype(BF16), peer_v.astype(BF16)
    for l in range(DEPTH):
        p = dict(w_in=w_in[l].astype(BF16), conv_w=conv_w[l], conv_b=conv_b[l],
                 wa=_block_diag_pairs(w_rg_a[l]), wx=_block_diag_pairs(w_rg_x[l]),
                 b_a=b_rg_a[l], b_x=b_rg_x[l], lam=lru_lambda[l],
                 w_pool=w_pool[l].astype(BF16), pool_scale=pool_scale[l], w_out=w_out[l].astype(BF16),
                 ln1_g=ln1_g[l], ln1_b=ln1_b[l], ln2_g=ln2_g[l], ln2_b=ln2_b[l],
                 wq=peer_wq[l].astype(BF16), keys=peer_keys[l].astype(BF16),
                 peer_u=u_bf16, peer_v=v_bf16, layer=l)
        for stream in ("ctx", "latent") if l % 2 == 0 else ("latent", "ctx"):
            if stream == "ctx":
                xp, st = _layer(xp, mod[l, n_req:n_req + 1], zero_state, p)
                ctx_states.append(st)
            else:
                xs, _ = _layer(xs, mod[l, :n_req], state_rglru[:, l], p)
    return xp, xs, jnp.stack(ctx_states, axis=1)
```

```python
import functools

import jax
import jax.numpy as jnp
from jax import lax
from jax.experimental import pallas as pl
from jax.experimental.pallas import tpu as pltpu
from jax.experimental.pallas import tpu_sc as plsc

F32 = jnp.float32
BF16 = jnp.bfloat16

D_MODEL = 1024
DEPTH = 2
GRID_W = 64
D_RNN = 512
N_RNN_HEADS = 8
RNN_HEAD_DIM = D_RNN // N_RNN_HEADS
CONV_W = 4
LRU_C = 8.0
D_POOL = 512
POOL_WINDOWS = (2, 4, 8, 16)
POOL_GROUP_DIM = D_POOL // len(POOL_WINDOWS)
D_IN = 2 * D_RNN + D_POOL
PEER_HEADS = 8
PEER_NKEYS = 128
PEER_TOPK = 16
PEER_QDIM = 256
PEER_HALF = PEER_QDIM // 2
PEER_PICKS = PEER_HEADS * PEER_TOPK
N_MOD = 6
DEEPNORM_ALPHA = (2 * DEPTH) ** 0.25
LN_EPS = 1e-5

LANES = 128
SUBLANES = 8
VMEM_LIMIT_BYTES = 56 * 1024 * 1024
MOD_ROWS = 16
ROW_CHUNK = 256
TOKEN_TILE = 512
ROUTE_TILE = 256
MOD_TILE = 1536


def _params(*sem):
    return pltpu.CompilerParams(dimension_semantics=sem, vmem_limit_bytes=VMEM_LIMIT_BYTES)


def _layer_norm(y, g, b):
    mu = jnp.mean(y, -1, keepdims=True)
    var = jnp.mean(jnp.square(y - mu), -1, keepdims=True)
    return (y - mu) * lax.rsqrt(var + LN_EPS) * g + b


def _mod_body(c_ref, w_ref, b_ref, o_ref):
    c = c_ref[...]
    s = (c * jax.nn.sigmoid(c)).astype(BF16)
    o_ref[0] = jnp.dot(s, w_ref[0].astype(BF16), preferred_element_type=F32) + b_ref[0]


def _modulation(cvecs, w_mod, b_mod):
    n_out = N_MOD * D_MODEL
    tn = MOD_TILE
    return pl.pallas_call(
        _mod_body,
        grid=(DEPTH, n_out // tn),
        in_specs=[pl.BlockSpec((MOD_ROWS, D_MODEL), lambda l, j: (0, 0)),
                  pl.BlockSpec((1, D_MODEL, tn), lambda l, j: (l, 0, j)),
                  pl.BlockSpec((1, 1, tn), lambda l, j: (l, 0, j))],
        out_specs=pl.BlockSpec((1, MOD_ROWS, tn), lambda l, j: (l, 0, j)),
        out_shape=jax.ShapeDtypeStruct((DEPTH, MOD_ROWS, n_out), F32),
        compiler_params=_params("parallel", "parallel"),
        name="modulation",
    )(cvecs, w_mod, b_mod.reshape(DEPTH, 1, n_out))


def _add_pos_body(x_ref, p_ref, o_ref):
    o_ref[0] = x_ref[0] + p_ref[...]


def _add_pos(x, pos):
    b, t, d = x.shape
    tm = min(t, TOKEN_TILE)
    return pl.pallas_call(
        _add_pos_body,
        grid=(b, t // tm),
        in_specs=[pl.BlockSpec((1, tm, d), lambda i, j: (i, j, 0)),
                  pl.BlockSpec((tm, d), lambda i, j: (j, 0))],
        out_specs=pl.BlockSpec((1, tm, d), lambda i, j: (i, j, 0)),
        out_shape=jax.ShapeDtypeStruct(x.shape, x.dtype),
        compiler_params=_params("parallel", "parallel"),
        name="add_pos",
    )(x, pos)


def _mix_in_body(x_ref, m_ref, w_ref, z_ref):
    m = m_ref[0]
    h = x_ref[0] * (1.0 + m[1:2]) + m[0:1]
    z = jnp.dot(h.astype(BF16), w_ref[...], preferred_element_type=F32)
    for c in range(D_IN // LANES):
        z_ref[0, c] = z[:, c * LANES:(c + 1) * LANES]


def _mod_spec(mods):
    if mods.shape[0] == 1:
        return pl.BlockSpec((1, SUBLANES, D_MODEL), lambda i, j: (0, 0, 0))
    return pl.BlockSpec((1, SUBLANES, D_MODEL), lambda i, j: (i, 0, 0))


def _mix_in(x, mods, w_in):
    b, t, d = x.shape
    tm = min(t, TOKEN_TILE)
    return pl.pallas_call(
        _mix_in_body,
        grid=(b, t // tm),
        in_specs=[pl.BlockSpec((1, tm, d), lambda i, j: (i, j, 0)),
                  _mod_spec(mods),
                  pl.BlockSpec((d, D_IN), lambda i, j: (0, 0))],
        out_specs=pl.BlockSpec((1, D_IN // LANES, tm, LANES), lambda i, j: (i, 0, j, 0)),
        out_shape=jax.ShapeDtypeStruct((b, D_IN // LANES, t, LANES), F32),
        compiler_params=_params("parallel", "parallel"),
        name="mix_in",
    )(x, mods, w_in)


PAD = SUBLANES
CARRY_PAD = 2 * SUBLANES


def _group_scan(a, u, reverse):
    row = lax.broadcasted_iota(jnp.int32, a.shape, 1)
    for s in (1, 2, 4):
        if reverse:
            keep = row <= SUBLANES - 1 - s
            shift = SUBLANES - s
        else:
            keep = row >= s
            shift = s
        a_sh = jnp.where(keep, pltpu.roll(a, shift, 1), 1.0)
        u_sh = jnp.where(keep, pltpu.roll(u, shift, 1), 0.0)
        u = a * u_sh + u
        a = a * a_sh
    return a, u


def _rnn_body(xr_ref, gr_ref, h0_ref, cw_ref, cb_ref, wa_ref, wx_ref, ba_ref, bx_ref, lam_ref,
              out_ref, st_ref, xpad, a_s, u_s, a2_s, u2_s, e_s):
    t_len = xr_ref.shape[2]
    rc = min(ROW_CHUNK, t_len)
    zeros = jnp.zeros((PAD, LANES), F32)
    xpad[0:PAD] = zeros
    xpad[t_len + PAD:t_len + 2 * PAD] = zeros
    xpad[PAD:t_len + PAD] = xr_ref[0, 0]

    cw = cw_ref[...]
    lam = lam_ref[...]
    softplus_neg = jnp.maximum(-lam, 0.0) + jnp.log1p(jnp.exp(-jnp.abs(lam)))
    coef = -LRU_C * softplus_neg
    left = CONV_W // 2

    for c in range(t_len // rc):
        t0 = c * rc
        xc = cb_ref[...] + xpad[t0 + PAD - left:t0 + PAD - left + rc] * cw[0:1]
        for k in range(1, CONV_W):
            xc = xc + xpad[t0 + PAD - left + k:t0 + PAD - left + k + rc] * cw[k:k + 1]
        xcb = xc.astype(BF16)
        for d in range(2):
            r = jax.nn.sigmoid(jnp.dot(xcb, wa_ref[d, 0], preferred_element_type=F32) + ba_ref[d:d + 1])
            gi = jax.nn.sigmoid(jnp.dot(xcb, wx_ref[d, 0], preferred_element_type=F32) + bx_ref[d:d + 1])
            log_a = coef[d:d + 1] * r
            a = jnp.exp(log_a)
            u = jnp.sqrt(-jnp.tanh(log_a) * (a * a + 1.0)) * (gi * xc)
            a, u = _group_scan(a.reshape(rc // SUBLANES, SUBLANES, LANES),
                               u.reshape(rc // SUBLANES, SUBLANES, LANES), reverse=(d == 1))
            a_s[d, t0:t0 + rc] = a.reshape(rc, LANES)
            u_s[d, t0:t0 + rc] = u.reshape(rc, LANES)

    n_groups = t_len // SUBLANES
    n_super = n_groups // SUBLANES
    for d in range(2):
        edge = pl.ds(SUBLANES - 1 if d == 0 else 0, n_groups, stride=SUBLANES)
        a2, u2 = _group_scan(a_s[d, edge].reshape(n_super, SUBLANES, LANES),
                             u_s[d, edge].reshape(n_super, SUBLANES, LANES), reverse=(d == 1))
        a2_s[d] = a2.reshape(n_groups, LANES)
        u2_s[d] = u2.reshape(n_groups, LANES)

    h0 = h0_ref[0]
    e_s[0, SUBLANES - 1:SUBLANES] = h0[0:1]
    e_s[1, n_groups:n_groups + 1] = h0[1:2]
    unroll = min(SUBLANES, n_super)

    def carry_step(i, carry):
        hf, hb = carry
        for j in range(unroll):
            gf = pl.multiple_of((i * unroll + j) * SUBLANES, SUBLANES)
            gb = pl.multiple_of((n_super - 1 - i * unroll - j) * SUBLANES, SUBLANES)
            h_f = u2_s[0, pl.ds(gf, SUBLANES)] + a2_s[0, pl.ds(gf, SUBLANES)] * hf
            h_b = u2_s[1, pl.ds(gb, SUBLANES)] + a2_s[1, pl.ds(gb, SUBLANES)] * hb
            e_s[0, pl.ds(gf + SUBLANES, SUBLANES)] = h_f
            e_s[1, pl.ds(gb, SUBLANES)] = h_b
            hf = h_f[SUBLANES - 1:SUBLANES]
            hb = h_b[0:1]
        return hf, hb

    hf, hb = lax.fori_loop(0, n_super // unroll, carry_step, (h0[0:1], h0[1:2]))
    st_ref[0, 0:1] = hf
    st_ref[0, 1:2] = hb

    gc = rc // SUBLANES
    for c in range(t_len // rc):
        t0 = c * rc
        g0 = t0 // SUBLANES
        hs = []
        for d, off in ((0, SUBLANES - 1), (1, 1)):
            entry = e_s[d, g0 + off:g0 + off + gc]
            hs.append(u_s[d, t0:t0 + rc].reshape(gc, SUBLANES, LANES)
                      + a_s[d, t0:t0 + rc].reshape(gc, SUBLANES, LANES) * entry[:, None, :])
        hsum = (hs[0] + hs[1]).reshape(rc, LANES)
        out_ref[0, 0, t0:t0 + rc] = (hsum * jax.nn.gelu(gr_ref[0, 0, t0:t0 + rc])).astype(BF16)


def _rnn(z, h0, conv_w, conv_b, wa, wx, b_a, b_x, lam):
    b, _, t, _ = z.shape
    nblk = D_RNN // LANES
    return pl.pallas_call(
        _rnn_body,
        grid=(b, nblk),
        in_specs=[pl.BlockSpec((1, 1, t, LANES), lambda i, j: (i, j, 0, 0)),
                  pl.BlockSpec((1, 1, t, LANES), lambda i, j: (i, nblk + j, 0, 0)),
                  pl.BlockSpec((1, 2, LANES), lambda i, j: (i, 0, j)),
                  pl.BlockSpec((CONV_W, LANES), lambda i, j: (0, j)),
                  pl.BlockSpec((1, LANES), lambda i, j: (0, j)),
                  pl.BlockSpec((2, 1, LANES, LANES), lambda i, j: (0, j, 0, 0)),
                  pl.BlockSpec((2, 1, LANES, LANES), lambda i, j: (0, j, 0, 0)),
                  pl.BlockSpec((2, LANES), lambda i, j: (0, j)),
                  pl.BlockSpec((2, LANES), lambda i, j: (0, j)),
                  pl.BlockSpec((2, LANES), lambda i, j: (0, j))],
        out_specs=[pl.BlockSpec((1, 1, t, LANES), lambda i, j: (i, j, 0, 0)),
                   pl.BlockSpec((1, 2, LANES), lambda i, j: (i, 0, j))],
        out_shape=[jax.ShapeDtypeStruct((b, nblk, t, LANES), BF16),
                   jax.ShapeDtypeStruct((b, 2, D_RNN), F32)],
        scratch_shapes=[pltpu.VMEM((t + 2 * PAD, LANES), F32),
                        pltpu.VMEM((2, t, LANES), F32),
                        pltpu.VMEM((2, t, LANES), F32),
                        pltpu.VMEM((2, t // SUBLANES, LANES), F32),
                        pltpu.VMEM((2, t // SUBLANES, LANES), F32),
                        pltpu.VMEM((2, t // SUBLANES + CARRY_PAD, LANES), F32)],
        compiler_params=_params("parallel", "parallel"),
        name="rglru",
    )(z, z, h0, conv_w, conv_b.reshape(1, D_RNN), wa, wx, b_a, b_x, lam)


def _block_diag_pairs(w):
    per = LANES // RNN_HEAD_DIM
    w = w.reshape(2, D_RNN // LANES, per, RNN_HEAD_DIM, RNN_HEAD_DIM)
    out = jnp.zeros((2, D_RNN // LANES, LANES, LANES), w.dtype)
    for p in range(per):
        sl = slice(p * RNN_HEAD_DIM, (p + 1) * RNN_HEAD_DIM)
        out = out.at[:, :, sl, sl].set(w[:, :, p])
    return out.astype(BF16)


POOL_PAD = 16


def _pool_body(xq_ref, w_ref, sc_ref, out_ref, ppad):
    t_len = xq_ref.shape[2]
    rc = min(ROW_CHUNK, t_len)
    zeros = jnp.zeros((POOL_PAD, D_POOL), F32)
    ppad[0:POOL_PAD] = zeros
    ppad[t_len + POOL_PAD:t_len + 2 * POOL_PAD] = zeros
    for g in range(len(POOL_WINDOWS)):
        ppad[POOL_PAD:t_len + POOL_PAD, g * POOL_GROUP_DIM:(g + 1) * POOL_GROUP_DIM] = xq_ref[0, g]
    for c in range(t_len // rc):
        t0 = c * rc
        tpos = t0 + lax.broadcasted_iota(jnp.int32, (rc, POOL_GROUP_DIM), 0)
        for g, w in enumerate(POOL_WINDOWS):
            cols = slice(g * POOL_GROUP_DIM, (g + 1) * POOL_GROUP_DIM)
            half = w // 2
            base = t0 + POOL_PAD - half
            s = ppad[base:base + rc, cols]
            for k in range(1, w):
                s = s + ppad[base + k:base + k + rc, cols]
            cnt = (jnp.minimum(tpos + half, t_len) - jnp.maximum(tpos - half, 0)).astype(F32)
            pooled = s / cnt - ppad[t0 + POOL_PAD:t0 + POOL_PAD + rc, cols]
            y = jnp.dot(pooled.astype(BF16), w_ref[g], preferred_element_type=F32)
            out_ref[0, g, t0:t0 + rc] = (y * sc_ref[:, cols]).astype(BF16)


def _pool(z, w_pool, pool_scale):
    b, _, t, _ = z.shape
    n_grp = len(POOL_WINDOWS)
    return pl.pallas_call(
        _pool_body,
        grid=(b,),
        in_specs=[pl.BlockSpec((1, n_grp, t, POOL_GROUP_DIM), lambda i: (i, 2 * D_RNN // D_POOL, 0, 0)),
                  pl.BlockSpec((len(POOL_WINDOWS), POOL_GROUP_DIM, POOL_GROUP_DIM), lambda i: (0, 0, 0)),
                  pl.BlockSpec((1, D_POOL), lambda i: (0, 0))],
        out_specs=pl.BlockSpec((1, n_grp, t, POOL_GROUP_DIM), lambda i: (i, 0, 0, 0)),
        out_shape=jax.ShapeDtypeStruct((b, n_grp, t, POOL_GROUP_DIM), BF16),
        scratch_shapes=[pltpu.VMEM((t + 2 * POOL_PAD, D_POOL), F32)],
        compiler_params=_params("parallel"),
        name="pool",
    )(z, w_pool, pool_scale.reshape(1, D_POOL))


def _mix_out_body(rnn_ref, pool_ref, x_ref, m_ref, w_ref, g_ref, b_ref, x1_ref, hq_ref):
    m = m_ref[0]
    rnn = jnp.concatenate([rnn_ref[0, c] for c in range(rnn_ref.shape[1])], axis=1)
    pool = jnp.concatenate([pool_ref[0, c] for c in range(pool_ref.shape[1])], axis=1)
    mix = (jnp.dot(rnn, w_ref[0:D_RNN], preferred_element_type=F32)
           + jnp.dot(pool, w_ref[D_RNN:D_RNN + D_POOL], preferred_element_type=F32))
    x1 = _layer_norm(DEEPNORM_ALPHA * x_ref[0] + m[2:3] * mix, g_ref[...], b_ref[...])
    x1_ref[0] = x1
    hq_ref[0] = (x1 * (1.0 + m[4:5]) + m[3:4]).astype(BF16)


def _mix_out(rnn, pool, x, mods, w_out, ln_g, ln_b):
    b, t, d = x.shape
    tm = min(t, TOKEN_TILE)
    tok = lambda i, j: (i, j, 0)
    return pl.pallas_call(
        _mix_out_body,
        grid=(b, t // tm),
        in_specs=[pl.BlockSpec((1, D_RNN // LANES, tm, LANES), lambda i, j: (i, 0, j, 0)),
                  pl.BlockSpec((1, D_POOL // LANES, tm, LANES), lambda i, j: (i, 0, j, 0)),
                  pl.BlockSpec((1, tm, d), tok),
                  _mod_spec(mods),
                  pl.BlockSpec((D_RNN + D_POOL, d), lambda i, j: (0, 0)),
                  pl.BlockSpec((1, d), lambda i, j: (0, 0)),
                  pl.BlockSpec((1, d), lambda i, j: (0, 0))],
        out_specs=[pl.BlockSpec((1, tm, d), tok), pl.BlockSpec((1, tm, d), tok)],
        out_shape=[jax.ShapeDtypeStruct(x.shape, F32), jax.ShapeDtypeStruct(x.shape, BF16)],
        compiler_params=_params("parallel", "parallel"),
        name="mix_out",
    )(rnn, pool, x, mods, w_out, ln_g.reshape(1, d), ln_b.reshape(1, d))


NEG_INF = float("-inf")


BIG_ID = 2 ** 30


def _sort_network(n):
    def merge(lo, hi, r):
        step = r * 2
        if step < hi - lo:
            yield from merge(lo, hi, step)
            yield from merge(lo + r, hi, step)
            for i in range(lo + r, hi - r, step):
                yield (i, i + r)
        else:
            yield (lo, lo + r)

    def sort(lo, hi):
        if hi - lo >= 1:
            mid = lo + (hi - lo) // 2
            yield from sort(lo, mid)
            yield from sort(mid + 1, hi)
            yield from merge(lo, hi, 1)

    return list(sort(0, n - 1))


def _sorted_levels(s_t):
    n_lvl = s_t.shape[0] // SUBLANES
    sub = lax.broadcasted_iota(jnp.int32, (SUBLANES, s_t.shape[1]), 0)
    vals = [s_t[l * SUBLANES:(l + 1) * SUBLANES] for l in range(n_lvl)]
    ids = [sub + l * SUBLANES for l in range(n_lvl)]
    untouched = [True] * n_lvl
    for i, j in _sort_network(n_lvl):
        a, b, ia, ib = vals[i], vals[j], ids[i], ids[j]
        swap = b > a
        if not (untouched[i] and untouched[j]):
            swap = swap | ((b == a) & (ib < ia))
        vals[i], vals[j] = jnp.maximum(a, b), jnp.minimum(a, b)
        ids[i], ids[j] = jnp.where(swap, ib, ia), jnp.where(swap, ia, ib)
        untouched[i] = untouched[j] = False
    return vals, ids


def _pop_top(vals, ids, k):
    vals, ids = list(vals), list(ids)
    out_v, out_i = [], []
    for it in range(k):
        m = jnp.max(vals[0], axis=0, keepdims=True)
        pick = jnp.min(jnp.where(vals[0] == m, ids[0], BIG_ID), axis=0, keepdims=True)
        sel = ids[0] == pick
        out_v.append(m)
        out_i.append(pick)
        for l in range(min(len(vals) - 1, k - 1 - it)):
            vals[l] = jnp.where(sel, vals[l + 1], vals[l])
            ids[l] = jnp.where(sel, ids[l + 1], ids[l])
    return jnp.concatenate(out_v, axis=0), jnp.concatenate(out_i, axis=0)


def _pair_top(s1, i1, s2, i2):
    k = PEER_TOPK
    tm = s1.shape[1]
    sub = lax.broadcasted_iota(jnp.int32, (SUBLANES, tm), 0)
    s2lo, i2lo = s2[0:SUBLANES], i2[0:SUBLANES]
    cand, expert = [], []
    for l in range(k):
        ok = (sub + 1) * (l + 1) <= k
        cand.append(jnp.where(ok, s1[l:l + 1] + s2lo, NEG_INF))
        expert.append(i1[l:l + 1] * PEER_NKEYS + i2lo)
    single = s1[0:1] + s2[SUBLANES:k]
    single_expert = i1[0:1] * PEER_NKEYS + i2[SUBLANES:k]
    single_flat = sub + SUBLANES
    pops = jnp.zeros((SUBLANES, tm), jnp.int32)
    out_s, out_e = [], []
    for it in range(k):
        flat = pops * k + sub
        m = jnp.max(jnp.maximum(cand[0], single), axis=0, keepdims=True)
        pick = jnp.min(jnp.minimum(jnp.where(cand[0] == m, flat, BIG_ID),
                                   jnp.where(single == m, single_flat, BIG_ID)), axis=0, keepdims=True)
        sel = flat == pick
        sel_single = single_flat == pick
        out_s.append(m)
        out_e.append(jnp.sum(jnp.where(sel, expert[0], 0) + jnp.where(sel_single, single_expert, 0),
                             axis=0, keepdims=True))
        single = jnp.where(sel_single, NEG_INF, single)
        pops = jnp.where(sel, pops + 1, pops)
        for l in range(k - 1 - it):
            cand[l] = jnp.where(sel, cand[l + 1], cand[l])
            expert[l] = jnp.where(sel, expert[l + 1], expert[l])
    return jnp.concatenate(out_s, axis=0), jnp.concatenate(out_e, axis=0)


def _route_body(hq_ref, wq_ref, keys_ref, idx_ref, gate_ref):
    q = jnp.dot(hq_ref[...], wq_ref[...], preferred_element_type=F32)
    gates, experts = [], []
    for h in range(PEER_HEADS):
        tops = []
        for p in range(2):
            c0 = (h * 2 + p) * PEER_HALF
            qs = q[:, c0:c0 + PEER_HALF].astype(BF16)
            s_t = lax.dot_general(keys_ref[h, p], qs, (((1,), (1,)), ((), ())),
                                  preferred_element_type=F32)
            tops.append(_pop_top(*_sorted_levels(s_t), PEER_TOPK))
        (s1, i1), (s2, i2) = tops
        sc, picked = _pair_top(s1, i1, s2, i2)
        e = jnp.exp(sc - sc[0:1])
        gates.append(e / jnp.sum(e, axis=0, keepdims=True))
        experts.append(picked)
    gate_ref[...] = jnp.concatenate(gates, axis=0).T
    idx_ref[...] = jnp.concatenate(experts, axis=0).T


def _route(hq, wq, keys):
    n, d = hq.shape
    tm = ROUTE_TILE
    return pl.pallas_call(
        _route_body,
        grid=(n // tm,),
        in_specs=[pl.BlockSpec((tm, d), lambda i: (i, 0)),
                  pl.BlockSpec((d, PEER_HEADS * PEER_QDIM), lambda i: (0, 0)),
                  pl.BlockSpec((PEER_HEADS, 2, PEER_NKEYS, PEER_HALF), lambda i: (0, 0, 0, 0))],
        out_specs=[pl.BlockSpec((tm, PEER_PICKS), lambda i: (i, 0)),
                   pl.BlockSpec((tm, PEER_PICKS), lambda i: (i, 0))],
        out_shape=[jax.ShapeDtypeStruct((n, PEER_PICKS), jnp.int32),
                   jax.ShapeDtypeStruct((n, PEER_PICKS), F32)],
        compiler_params=_params("parallel"),
        name="peer_route",
    )(hq, wq, keys)


def _final_body(x1_ref, f_ref, m_ref, g_ref, b_ref, o_ref):
    m = m_ref[0]
    o_ref[0] = _layer_norm(DEEPNORM_ALPHA * x1_ref[0] + m[5:6] * f_ref[0], g_ref[...], b_ref[...])


def _final(x1, ffn, mods, ln_g, ln_b):
    b, t, d = x1.shape
    tm = min(t, TOKEN_TILE)
    tok = lambda i, j: (i, j, 0)
    return pl.pallas_call(
        _final_body,
        grid=(b, t // tm),
        in_specs=[pl.BlockSpec((1, tm, d), tok), pl.BlockSpec((1, tm, d), tok), _mod_spec(mods),
                  pl.BlockSpec((1, d), lambda i, j: (0, 0)), pl.BlockSpec((1, d), lambda i, j: (0, 0))],
        out_specs=pl.BlockSpec((1, tm, d), tok),
        out_shape=jax.ShapeDtypeStruct(x1.shape, F32),
        compiler_params=_params("parallel", "parallel"),
        name="final_ln",
    )(x1, ffn, mods, ln_g.reshape(1, d), ln_b.reshape(1, d))


N_EXPERTS = PEER_NKEYS * PEER_NKEYS
GATE_ROWS = 4
GATE_TOKENS = 32
PEER_TB = 1024
PEER_EC = 2048


def _gate_matrix(idx, gates):
    n = idx.shape[0]
    info = plsc.get_sparse_core_info()
    n_workers = info.num_cores * info.num_subcores
    lanes = info.num_lanes
    per_w = n // n_workers
    assert n % (n_workers * GATE_TOKENS) == 0 and GATE_TOKENS % GATE_ROWS == 0 and PEER_PICKS % lanes == 0
    vecs = PEER_PICKS // lanes
    mesh = plsc.VectorSubcoreMesh(core_axis_name="core", subcore_axis_name="subcore")

    @functools.partial(
        pl.kernel, out_type=jax.ShapeDtypeStruct((n, N_EXPERTS), F32), mesh=mesh,
        scratch_types=[pltpu.VMEM((GATE_TOKENS * PEER_PICKS,), jnp.int32),
                       pltpu.VMEM((GATE_TOKENS * PEER_PICKS,), F32)]
                      + [pltpu.VMEM((N_EXPERTS,), F32)] * GATE_ROWS
                      + [pltpu.SemaphoreType.DMA] * GATE_ROWS,
        compiler_params=pltpu.CompilerParams(needs_layout_passes=False), name="sc_gate_matrix")
    def run(idx_hbm, gate_hbm, out_hbm, idx_v, g_v, *bufs):
        rows, sems = bufs[:GATE_ROWS], bufs[GATE_ROWS:]
        wid = lax.axis_index("subcore") * info.num_cores + lax.axis_index("core")
        tok_base = wid * per_w
        zeros = jnp.zeros((lanes,), F32)

        @pl.loop(0, N_EXPERTS // lanes)
        def _(i):
            for r in rows:
                r[pl.ds(pl.multiple_of(i * lanes, lanes), lanes)] = zeros

        @pl.loop(0, per_w // GATE_TOKENS)
        def _(blk):
            tok0 = tok_base + blk * GATE_TOKENS
            k_at = pl.ds(pl.multiple_of(tok0 * PEER_PICKS, GATE_TOKENS * PEER_PICKS), GATE_TOKENS * PEER_PICKS)
            pltpu.sync_copy(idx_hbm.at[k_at], idx_v)
            pltpu.sync_copy(gate_hbm.at[k_at], g_v)

            @pl.loop(0, GATE_TOKENS // GATE_ROWS)
            def _(q):
                def picks(b, c):
                    return pl.ds(pl.multiple_of(((q * GATE_ROWS + b) * vecs + c) * lanes, lanes), lanes)

                for b in range(GATE_ROWS):
                    for c in range(vecs):
                        plsc.addupdate_scatter(rows[b], [idx_v[picks(b, c)]], g_v[picks(b, c)])
                    pltpu.make_async_copy(rows[b], out_hbm.at[tok0 + q * GATE_ROWS + b], sems[b]).start()
                for b in range(GATE_ROWS):
                    pltpu.make_async_copy(rows[b], out_hbm.at[tok0 + q * GATE_ROWS + b], sems[b]).wait()
                    for c in range(vecs):
                        plsc.store_scatter(rows[b], [idx_v[picks(b, c)]], zeros)

    return run(idx.reshape(-1), gates.reshape(-1))


def _dense_body(x_ref, u_ref, v_ref, g_ref, o_ref, acc):
    j = pl.program_id(1)

    @pl.when(j == 0)
    def _():
        acc[...] = jnp.zeros_like(acc)

    s = lax.dot_general(x_ref[...], u_ref[...], (((1,), (1,)), ((), ())), preferred_element_type=F32)
    g = g_ref[...]
    a = jnp.where(g != 0.0, jax.nn.gelu(s) * g, 0.0).astype(BF16)
    acc[...] += jnp.dot(a, v_ref[...], preferred_element_type=F32)

    @pl.when(j == pl.num_programs(1) - 1)
    def _():
        o_ref[...] = acc[...]


def _experts(hq, gate_mat, u_tab, v_tab, layer):
    n, d = hq.shape
    tb = min(PEER_TB, n)
    return pl.pallas_call(
        _dense_body,
        grid=(n // tb, N_EXPERTS // PEER_EC),
        in_specs=[pl.BlockSpec((tb, d), lambda i, j: (i, 0)),
                  pl.BlockSpec((None, PEER_EC, d), lambda i, j: (layer, j, 0)),
                  pl.BlockSpec((None, PEER_EC, d), lambda i, j: (layer, j, 0)),
                  pl.BlockSpec((tb, PEER_EC), lambda i, j: (i, j))],
        out_specs=pl.BlockSpec((tb, d), lambda i, j: (i, 0)),
        out_shape=jax.ShapeDtypeStruct((n, d), F32),
        scratch_shapes=[pltpu.VMEM((tb, d), F32)],
        compiler_params=_params("parallel", "arbitrary"),
        name="peer_dense",
    )(hq, u_tab, v_tab, gate_mat)


def _grid_pos_embed(rows):
    t = jnp.arange(rows * GRID_W)
    r = (t // GRID_W).astype(F32)
    col = (t % GRID_W).astype(F32)
    n_freq = D_MODEL // 4
    omega = 1.0 / (10000.0 ** (jnp.arange(n_freq, dtype=F32) / n_freq))

    def enc(p):
        ang = p[:, None] * omega[None, :]
        return jnp.concatenate([jnp.sin(ang), jnp.cos(ang)], -1)
    return jnp.concatenate([enc(r), enc(col)], -1)


def _layer(x, mods, h0, p):
    b, t, d = x.shape
    z = _mix_in(x, mods, p["w_in"])
    rnn, st = _rnn(z, h0, p["conv_w"], p["conv_b"], p["wa"], p["wx"], p["b_a"], p["b_x"], p["lam"])
    pool = _pool(z, p["w_pool"], p["pool_scale"])
    x1, hq = _mix_out(rnn, pool, x, mods, p["w_out"], p["ln1_g"], p["ln1_b"])
    hq = hq.reshape(b * t, d)
    idx, gates = _route(hq, p["wq"], p["keys"])
    gate_mat = _gate_matrix(idx, gates)
    ffn = _experts(hq, gate_mat, p["peer_u"], p["peer_v"], p["layer"])
    x2 = _final(x1, ffn.reshape(b, t, d), mods, p["ln2_g"], p["ln2_b"])
    return x2, st


def kernel(x_prompt, x_sample, state_rglru, c, c_ctx, w_mod, b_mod, w_in, conv_w, conv_b,
           w_rg_a, b_rg_a, w_rg_x, b_rg_x, lru_lambda, w_pool, pool_scale, w_out,
           ln1_g, ln1_b, ln2_g, ln2_b, peer_wq, peer_keys, peer_u, peer_v):
    n_req = c.shape[0]
    cvecs = jnp.zeros((MOD_ROWS, D_MODEL), F32).at[:n_req].set(c).at[n_req].set(c_ctx)
    mod = _modulation(cvecs, w_mod, b_mod).reshape(DEPTH, MOD_ROWS, N_MOD, D_MODEL)
    mod = jnp.pad(mod, ((0, 0), (0, 0), (0, SUBLANES - N_MOD), (0, 0)))

    rows = x_sample.shape[1] // GRID_W
    xs = _add_pos(x_sample, _grid_pos_embed(rows))
    xp = x_prompt
    zero_state = jnp.zeros((x_prompt.shape[0], 2, D_RNN), F32)
    ctx_states = []
    u_bf16, v_bf16 = peer_u.astype(BF16), peer_v.astype(BF16)
    for l in range(DEPTH):
        p = dict(w_in=w_in[l].astype(BF16), conv_w=conv_w[l], conv_b=conv_b[l],
                 wa=_block_diag_pairs(w_rg_a[l]), wx=_block_diag_pairs(w_rg_x[l]),
                 b_a=b_rg_a[l], b_x=b_rg_x[l], lam=lru_lambda[l],
                 w_pool=w_pool[l].astype(BF16), pool_scale=pool_scale[l], w_out=w_out[l].astype(BF16),
                 ln1_g=ln1_g[l], ln1_b=ln1_b[l], ln2_g=ln2_g[l], ln2_b=ln2_b[l],
                 wq=peer_wq[l].astype(BF16), keys=peer_keys[l].astype(BF16),
                 peer_u=u_bf16, peer_v=v_bf16, layer=l)
        for stream in ("ctx", "latent") if l % 2 == 0 else ("latent", "ctx"):
            if stream == "ctx":
                xp, st = _layer(xp, mod[l, n_req:n_req + 1], zero_state, p)
                ctx_states.append(st)
            else:
                xs, _ = _layer(xs, mod[l, :n_req], state_rglru[:, l], p)
    return xp, xs, jnp.stack(ctx_states, axis=1)
```

```python
import functools

import jax
import jax.numpy as jnp
from jax import lax
from jax.experimental import pallas as pl
from jax.experimental.pallas import tpu as pltpu
from jax.experimental.pallas import tpu_sc as plsc

F32 = jnp.float32
BF16 = jnp.bfloat16

D_MODEL = 1024
DEPTH = 2
GRID_W = 64
D_RNN = 512
N_RNN_HEADS = 8
RNN_HEAD_DIM = D_RNN // N_RNN_HEADS
CONV_W = 4
LRU_C = 8.0
D_POOL = 512
POOL_WINDOWS = (2, 4, 8, 16)
POOL_GROUP_DIM = D_POOL // len(POOL_WINDOWS)
D_IN = 2 * D_RNN + D_POOL
PEER_HEADS = 8
PEER_NKEYS = 128
PEER_TOPK = 16
PEER_QDIM = 256
PEER_HALF = PEER_QDIM // 2
PEER_PICKS = PEER_HEADS * PEER_TOPK
N_MOD = 6
DEEPNORM_ALPHA = (2 * DEPTH) ** 0.25
LN_EPS = 1e-5

LANES = 128
SUBLANES = 8
VMEM_LIMIT_BYTES = 56 * 1024 * 1024
MOD_ROWS = 16
ROW_CHUNK = 256
TOKEN_TILE = 1024
ROUTE_TILE = 256
MOD_TILE = 1536


def _params(*sem):
    return pltpu.CompilerParams(dimension_semantics=sem, vmem_limit_bytes=VMEM_LIMIT_BYTES)


def _layer_norm(y, g, b):
    mu = jnp.mean(y, -1, keepdims=True)
    var = jnp.mean(jnp.square(y - mu), -1, keepdims=True)
    return (y - mu) * lax.rsqrt(var + LN_EPS) * g + b


def _mod_body(c_ref, w_ref, b_ref, o_ref):
    c = c_ref[...]
    s = (c * jax.nn.sigmoid(c)).astype(BF16)
    o_ref[0] = jnp.dot(s, w_ref[0].astype(BF16), preferred_element_type=F32) + b_ref[0]


def _modulation(cvecs, w_mod, b_mod):
    n_out = N_MOD * D_MODEL
    tn = MOD_TILE
    return pl.pallas_call(
        _mod_body,
        grid=(DEPTH, n_out // tn),
        in_specs=[pl.BlockSpec((MOD_ROWS, D_MODEL), lambda l, j: (0, 0)),
                  pl.BlockSpec((1, D_MODEL, tn), lambda l, j: (l, 0, j)),
                  pl.BlockSpec((1, 1, tn), lambda l, j: (l, 0, j))],
        out_specs=pl.BlockSpec((1, MOD_ROWS, tn), lambda l, j: (l, 0, j)),
        out_shape=jax.ShapeDtypeStruct((DEPTH, MOD_ROWS, n_out), F32),
        compiler_params=_params("parallel", "parallel"),
        name="modulation",
    )(cvecs, w_mod, b_mod.reshape(DEPTH, 1, n_out))


def _add_pos_body(x_ref, p_ref, o_ref):
    o_ref[0] = x_ref[0] + p_ref[...]


def _add_pos(x, pos):
    b, t, d = x.shape
    tm = min(t, TOKEN_TILE)
    return pl.pallas_call(
        _add_pos_body,
        grid=(b, t // tm),
        in_specs=[pl.BlockSpec((1, tm, d), lambda i, j: (i, j, 0)),
                  pl.BlockSpec((tm, d), lambda i, j: (j, 0))],
        out_specs=pl.BlockSpec((1, tm, d), lambda i, j: (i, j, 0)),
        out_shape=jax.ShapeDtypeStruct(x.shape, x.dtype),
        compiler_params=_params("parallel", "parallel"),
        name="add_pos",
    )(x, pos)


def _mix_in_body(x_ref, m_ref, w_ref, z_ref):
    m = m_ref[0]
    h = x_ref[0] * (1.0 + m[1:2]) + m[0:1]
    z = jnp.dot(h.astype(BF16), w_ref[...], preferred_element_type=F32)
    for c in range(D_IN // LANES):
        z_ref[0, c] = z[:, c * LANES:(c + 1) * LANES]


def _mod_spec(mods):
    if mods.shape[0] == 1:
        return pl.BlockSpec((1, SUBLANES, D_MODEL), lambda i, j: (0, 0, 0))
    return pl.BlockSpec((1, SUBLANES, D_MODEL), lambda i, j: (i, 0, 0))


def _mix_in(x, mods, w_in):
    b, t, d = x.shape
    tm = min(t, TOKEN_TILE)
    return pl.pallas_call(
        _mix_in_body,
        grid=(b, t // tm),
        in_specs=[pl.BlockSpec((1, tm, d), lambda i, j: (i, j, 0)),
                  _mod_spec(mods),
                  pl.BlockSpec((d, D_IN), lambda i, j: (0, 0))],
        out_specs=pl.BlockSpec((1, D_IN // LANES, tm, LANES), lambda i, j: (i, 0, j, 0)),
        out_shape=jax.ShapeDtypeStruct((b, D_IN // LANES, t, LANES), F32),
        compiler_params=_params("parallel", "parallel"),
        name="mix_in",
    )(x, mods, w_in)


PAD = SUBLANES
CARRY_PAD = 2 * SUBLANES


def _group_scan(a, u, reverse):
    row = lax.broadcasted_iota(jnp.int32, a.shape, 1)
    for s in (1, 2, 4):
        if reverse:
            keep = row <= SUBLANES - 1 - s
            shift = SUBLANES - s
        else:
            keep = row >= s
            shift = s
        a_sh = jnp.where(keep, pltpu.roll(a, shift, 1), 1.0)
        u_sh = jnp.where(keep, pltpu.roll(u, shift, 1), 0.0)
        u = a * u_sh + u
        a = a * a_sh
    return a, u


def _rnn_body(xr_ref, gr_ref, h0_ref, cw_ref, cb_ref, wa_ref, wx_ref, ba_ref, bx_ref, lam_ref,
              out_ref, st_ref, xpad, a_s, u_s, a2_s, u2_s, e_s):
    t_len = xr_ref.shape[2]
    rc = min(ROW_CHUNK, t_len)
    zeros = jnp.zeros((PAD, LANES), F32)
    xpad[0:PAD] = zeros
    xpad[t_len + PAD:t_len + 2 * PAD] = zeros
    xpad[PAD:t_len + PAD] = xr_ref[0, 0]

    cw = cw_ref[...]
    lam = lam_ref[...]
    softplus_neg = jnp.maximum(-lam, 0.0) + jnp.log1p(jnp.exp(-jnp.abs(lam)))
    coef = -LRU_C * softplus_neg
    left = CONV_W // 2

    for c in range(t_len // rc):
        t0 = c * rc
        xc = cb_ref[...] + xpad[t0 + PAD - left:t0 + PAD - left + rc] * cw[0:1]
        for k in range(1, CONV_W):
            xc = xc + xpad[t0 + PAD - left + k:t0 + PAD - left + k + rc] * cw[k:k + 1]
        xcb = xc.astype(BF16)
        for d in range(2):
            r = jax.nn.sigmoid(jnp.dot(xcb, wa_ref[d, 0], preferred_element_type=F32) + ba_ref[d:d + 1])
            gi = jax.nn.sigmoid(jnp.dot(xcb, wx_ref[d, 0], preferred_element_type=F32) + bx_ref[d:d + 1])
            log_a = coef[d:d + 1] * r
            a = jnp.exp(log_a)
            u = jnp.sqrt(-jnp.tanh(log_a) * (a * a + 1.0)) * (gi * xc)
            a, u = _group_scan(a.reshape(rc // SUBLANES, SUBLANES, LANES),
                               u.reshape(rc // SUBLANES, SUBLANES, LANES), reverse=(d == 1))
            a_s[d, t0:t0 + rc] = a.reshape(rc, LANES)
            u_s[d, t0:t0 + rc] = u.reshape(rc, LANES)

    n_groups = t_len // SUBLANES
    n_super = n_groups // SUBLANES
    for d in range(2):
        edge = pl.ds(SUBLANES - 1 if d == 0 else 0, n_groups, stride=SUBLANES)
        a2, u2 = _group_scan(a_s[d, edge].reshape(n_super, SUBLANES, LANES),
                             u_s[d, edge].reshape(n_super, SUBLANES, LANES), reverse=(d == 1))
        a2_s[d] = a2.reshape(n_groups, LANES)
        u2_s[d] = u2.reshape(n_groups, LANES)

    h0 = h0_ref[0]
    e_s[0, SUBLANES - 1:SUBLANES] = h0[0:1]
    e_s[1, n_groups:n_groups + 1] = h0[1:2]
    unroll = min(SUBLANES, n_super)

    def carry_step(i, carry):
        hf, hb = carry
        for j in range(unroll):
            gf = pl.multiple_of((i * unroll + j) * SUBLANES, SUBLANES)
            gb = pl.multiple_of((n_super - 1 - i * unroll - j) * SUBLANES, SUBLANES)
            h_f = u2_s[0, pl.ds(gf, SUBLANES)] + a2_s[0, pl.ds(gf, SUBLANES)] * hf
            h_b = u2_s[1, pl.ds(gb, SUBLANES)] + a2_s[1, pl.ds(gb, SUBLANES)] * hb
            e_s[0, pl.ds(gf + SUBLANES, SUBLANES)] = h_f
            e_s[1, pl.ds(gb, SUBLANES)] = h_b
            hf = h_f[SUBLANES - 1:SUBLANES]
            hb = h_b[0:1]
        return hf, hb

    hf, hb = lax.fori_loop(0, n_super // unroll, carry_step, (h0[0:1], h0[1:2]))
    st_ref[0, 0:1] = hf
    st_ref[0, 1:2] = hb

    gc = rc // SUBLANES
    for c in range(t_len // rc):
        t0 = c * rc
        g0 = t0 // SUBLANES
        hs = []
        for d, off in ((0, SUBLANES - 1), (1, 1)):
            entry = e_s[d, g0 + off:g0 + off + gc]
            hs.append(u_s[d, t0:t0 + rc].reshape(gc, SUBLANES, LANES)
                      + a_s[d, t0:t0 + rc].reshape(gc, SUBLANES, LANES) * entry[:, None, :])
        hsum = (hs[0] + hs[1]).reshape(rc, LANES)
        out_ref[0, 0, t0:t0 + rc] = (hsum * jax.nn.gelu(gr_ref[0, 0, t0:t0 + rc])).astype(BF16)


def _rnn(z, h0, conv_w, conv_b, wa, wx, b_a, b_x, lam):
    b, _, t, _ = z.shape
    nblk = D_RNN // LANES
    return pl.pallas_call(
        _rnn_body,
        grid=(b, nblk),
        in_specs=[pl.BlockSpec((1, 1, t, LANES), lambda i, j: (i, j, 0, 0)),
                  pl.BlockSpec((1, 1, t, LANES), lambda i, j: (i, nblk + j, 0, 0)),
                  pl.BlockSpec((1, 2, LANES), lambda i, j: (i, 0, j)),
                  pl.BlockSpec((CONV_W, LANES), lambda i, j: (0, j)),
                  pl.BlockSpec((1, LANES), lambda i, j: (0, j)),
                  pl.BlockSpec((2, 1, LANES, LANES), lambda i, j: (0, j, 0, 0)),
                  pl.BlockSpec((2, 1, LANES, LANES), lambda i, j: (0, j, 0, 0)),
                  pl.BlockSpec((2, LANES), lambda i, j: (0, j)),
                  pl.BlockSpec((2, LANES), lambda i, j: (0, j)),
                  pl.BlockSpec((2, LANES), lambda i, j: (0, j))],
        out_specs=[pl.BlockSpec((1, 1, t, LANES), lambda i, j: (i, j, 0, 0)),
                   pl.BlockSpec((1, 2, LANES), lambda i, j: (i, 0, j))],
        out_shape=[jax.ShapeDtypeStruct((b, nblk, t, LANES), BF16),
                   jax.ShapeDtypeStruct((b, 2, D_RNN), F32)],
        scratch_shapes=[pltpu.VMEM((t + 2 * PAD, LANES), F32),
                        pltpu.VMEM((2, t, LANES), F32),
                        pltpu.VMEM((2, t, LANES), F32),
                        pltpu.VMEM((2, t // SUBLANES, LANES), F32),
                        pltpu.VMEM((2, t // SUBLANES, LANES), F32),
                        pltpu.VMEM((2, t // SUBLANES + CARRY_PAD, LANES), F32)],
        compiler_params=_params("parallel", "parallel"),
        name="rglru",
    )(z, z, h0, conv_w, conv_b.reshape(1, D_RNN), wa, wx, b_a, b_x, lam)


def _block_diag_pairs(w):
    per = LANES // RNN_HEAD_DIM
    w = w.reshape(2, D_RNN // LANES, per, RNN_HEAD_DIM, RNN_HEAD_DIM)
    out = jnp.zeros((2, D_RNN // LANES, LANES, LANES), w.dtype)
    for p in range(per):
        sl = slice(p * RNN_HEAD_DIM, (p + 1) * RNN_HEAD_DIM)
        out = out.at[:, :, sl, sl].set(w[:, :, p])
    return out.astype(BF16)


POOL_PAD = 16


def _pool_body(xq_ref, w_ref, sc_ref, out_ref, ppad):
    t_len = xq_ref.shape[2]
    rc = min(ROW_CHUNK, t_len)
    zeros = jnp.zeros((POOL_PAD, D_POOL), F32)
    ppad[0:POOL_PAD] = zeros
    ppad[t_len + POOL_PAD:t_len + 2 * POOL_PAD] = zeros
    for g in range(len(POOL_WINDOWS)):
        ppad[POOL_PAD:t_len + POOL_PAD, g * POOL_GROUP_DIM:(g + 1) * POOL_GROUP_DIM] = xq_ref[0, g]
    for c in range(t_len // rc):
        t0 = c * rc
        tpos = t0 + lax.broadcasted_iota(jnp.int32, (rc, POOL_GROUP_DIM), 0)
        for g, w in enumerate(POOL_WINDOWS):
            cols = slice(g * POOL_GROUP_DIM, (g + 1) * POOL_GROUP_DIM)
            half = w // 2
            base = t0 + POOL_PAD - half
            s = ppad[base:base + rc, cols]
            for k in range(1, w):
                s = s + ppad[base + k:base + k + rc, cols]
            cnt = (jnp.minimum(tpos + half, t_len) - jnp.maximum(tpos - half, 0)).astype(F32)
            pooled = s / cnt - ppad[t0 + POOL_PAD:t0 + POOL_PAD + rc, cols]
            y = jnp.dot(pooled.astype(BF16), w_ref[g], preferred_element_type=F32)
            out_ref[0, g, t0:t0 + rc] = (y * sc_ref[:, cols]).astype(BF16)


def _pool(z, w_pool, pool_scale):
    b, _, t, _ = z.shape
    n_grp = len(POOL_WINDOWS)
    return pl.pallas_call(
        _pool_body,
        grid=(b,),
        in_specs=[pl.BlockSpec((1, n_grp, t, POOL_GROUP_DIM), lambda i: (i, 2 * D_RNN // D_POOL, 0, 0)),
                  pl.BlockSpec((len(POOL_WINDOWS), POOL_GROUP_DIM, POOL_GROUP_DIM), lambda i: (0, 0, 0)),
                  pl.BlockSpec((1, D_POOL), lambda i: (0, 0))],
        out_specs=pl.BlockSpec((1, n_grp, t, POOL_GROUP_DIM), lambda i: (i, 0, 0, 0)),
        out_shape=jax.ShapeDtypeStruct((b, n_grp, t, POOL_GROUP_DIM), BF16),
        scratch_shapes=[pltpu.VMEM((t + 2 * POOL_PAD, D_POOL), F32)],
        compiler_params=_params("parallel"),
        name="pool",
    )(z, w_pool, pool_scale.reshape(1, D_POOL))


def _mix_out_body(rnn_ref, pool_ref, x_ref, m_ref, w_ref, g_ref, b_ref, x1_ref, hq_ref):
    m = m_ref[0]
    rnn = jnp.concatenate([rnn_ref[0, c] for c in range(rnn_ref.shape[1])], axis=1)
    pool = jnp.concatenate([pool_ref[0, c] for c in range(pool_ref.shape[1])], axis=1)
    mix = (jnp.dot(rnn, w_ref[0:D_RNN], preferred_element_type=F32)
           + jnp.dot(pool, w_ref[D_RNN:D_RNN + D_POOL], preferred_element_type=F32))
    x1 = _layer_norm(DEEPNORM_ALPHA * x_ref[0] + m[2:3] * mix, g_ref[...], b_ref[...])
    x1_ref[0] = x1
    hq_ref[0] = (x1 * (1.0 + m[4:5]) + m[3:4]).astype(BF16)


def _mix_out(rnn, pool, x, mods, w_out, ln_g, ln_b):
    b, t, d = x.shape
    tm = min(t, TOKEN_TILE)
    tok = lambda i, j: (i, j, 0)
    return pl.pallas_call(
        _mix_out_body,
        grid=(b, t // tm),
        in_specs=[pl.BlockSpec((1, D_RNN // LANES, tm, LANES), lambda i, j: (i, 0, j, 0)),
                  pl.BlockSpec((1, D_POOL // LANES, tm, LANES), lambda i, j: (i, 0, j, 0)),
                  pl.BlockSpec((1, tm, d), tok),
                  _mod_spec(mods),
                  pl.BlockSpec((D_RNN + D_POOL, d), lambda i, j: (0, 0)),
                  pl.BlockSpec((1, d), lambda i, j: (0, 0)),
                  pl.BlockSpec((1, d), lambda i, j: (0, 0))],
        out_specs=[pl.BlockSpec((1, tm, d), tok), pl.BlockSpec((1, tm, d), tok)],
        out_shape=[jax.ShapeDtypeStruct(x.shape, F32), jax.ShapeDtypeStruct(x.shape, BF16)],
        compiler_params=_params("parallel", "parallel"),
        name="mix_out",
    )(rnn, pool, x, mods, w_out, ln_g.reshape(1, d), ln_b.reshape(1, d))


NEG_INF = float("-inf")


BIG_ID = 2 ** 30


def _sort_network(n):
    def merge(lo, hi, r):
        step = r * 2
        if step < hi - lo:
            yield from merge(lo, hi, step)
            yield from merge(lo + r, hi, step)
            for i in range(lo + r, hi - r, step):
                yield (i, i + r)
        else:
            yield (lo, lo + r)

    def sort(lo, hi):
        if hi - lo >= 1:
            mid = lo + (hi - lo) // 2
            yield from sort(lo, mid)
            yield from sort(mid + 1, hi)
            yield from merge(lo, hi, 1)

    return list(sort(0, n - 1))


def _sorted_levels(s_t):
    n_lvl = s_t.shape[0] // SUBLANES
    sub = lax.broadcasted_iota(jnp.int32, (SUBLANES, s_t.shape[1]), 0)
    vals = [s_t[l * SUBLANES:(l + 1) * SUBLANES] for l in range(n_lvl)]
    ids = [sub + l * SUBLANES for l in range(n_lvl)]
    untouched = [True] * n_lvl
    for i, j in _sort_network(n_lvl):
        a, b, ia, ib = vals[i], vals[j], ids[i], ids[j]
        swap = b > a
        if not (untouched[i] and untouched[j]):
            swap = swap | ((b == a) & (ib < ia))
        vals[i], vals[j] = jnp.maximum(a, b), jnp.minimum(a, b)
        ids[i], ids[j] = jnp.where(swap, ib, ia), jnp.where(swap, ia, ib)
        untouched[i] = untouched[j] = False
    return vals, ids


def _pop_top(vals, ids, k):
    vals, ids = list(vals), list(ids)
    out_v, out_i = [], []
    for it in range(k):
        m = jnp.max(vals[0], axis=0, keepdims=True)
        pick = jnp.min(jnp.where(vals[0] == m, ids[0], BIG_ID), axis=0, keepdims=True)
        sel = ids[0] == pick
        out_v.append(m)
        out_i.append(pick)
        for l in range(min(len(vals) - 1, k - 1 - it)):
            vals[l] = jnp.where(sel, vals[l + 1], vals[l])
            ids[l] = jnp.where(sel, ids[l + 1], ids[l])
    return jnp.concatenate(out_v, axis=0), jnp.concatenate(out_i, axis=0)


def _pair_top(s1, i1, s2, i2):
    k = PEER_TOPK
    tm = s1.shape[1]
    sub = lax.broadcasted_iota(jnp.int32, (SUBLANES, tm), 0)
    s2lo, i2lo = s2[0:SUBLANES], i2[0:SUBLANES]
    cand, expert = [], []
    for l in range(k):
        ok = (sub + 1) * (l + 1) <= k
        cand.append(jnp.where(ok, s1[l:l + 1] + s2lo, NEG_INF))
        expert.append(i1[l:l + 1] * PEER_NKEYS + i2lo)
    single = s1[0:1] + s2[SUBLANES:k]
    single_expert = i1[0:1] * PEER_NKEYS + i2[SUBLANES:k]
    single_flat = sub + SUBLANES
    pops = jnp.zeros((SUBLANES, tm), jnp.int32)
    out_s, out_e = [], []
    for it in range(k):
        flat = pops * k + sub
        m = jnp.max(jnp.maximum(cand[0], single), axis=0, keepdims=True)
        pick = jnp.min(jnp.minimum(jnp.where(cand[0] == m, flat, BIG_ID),
                                   jnp.where(single == m, single_flat, BIG_ID)), axis=0, keepdims=True)
        sel = flat == pick
        sel_single = single_flat == pick
        out_s.append(m)
        out_e.append(jnp.sum(jnp.where(sel, expert[0], 0) + jnp.where(sel_single, single_expert, 0),
                             axis=0, keepdims=True))
        single = jnp.where(sel_single, NEG_INF, single)
        pops = jnp.where(sel, pops + 1, pops)
        for l in range(k - 1 - it):
            cand[l] = jnp.where(sel, cand[l + 1], cand[l])
            expert[l] = jnp.where(sel, expert[l + 1], expert[l])
    return jnp.concatenate(out_s, axis=0), jnp.concatenate(out_e, axis=0)


def _route_body(hq_ref, wq_ref, keys_ref, idx_ref, gate_ref):
    q = jnp.dot(hq_ref[...], wq_ref[...], preferred_element_type=F32)
    gates, experts = [], []
    for h in range(PEER_HEADS):
        tops = []
        for p in range(2):
            c0 = (h * 2 + p) * PEER_HALF
            qs = q[:, c0:c0 + PEER_HALF].astype(BF16)
            s_t = lax.dot_general(keys_ref[h, p], qs, (((1,), (1,)), ((), ())),
                                  preferred_element_type=F32)
            tops.append(_pop_top(*_sorted_levels(s_t), PEER_TOPK))
        (s1, i1), (s2, i2) = tops
        sc, picked = _pair_top(s1, i1, s2, i2)
        e = jnp.exp(sc - sc[0:1])
        gates.append(e / jnp.sum(e, axis=0, keepdims=True))
        experts.append(picked)
    gate_ref[...] = jnp.concatenate(gates, axis=0).T
    idx_ref[...] = jnp.concatenate(experts, axis=0).T


def _route(hq, wq, keys):
    n, d = hq.shape
    tm = ROUTE_TILE
    return pl.pallas_call(
        _route_body,
        grid=(n // tm,),
        in_specs=[pl.BlockSpec((tm, d), lambda i: (i, 0)),
                  pl.BlockSpec((d, PEER_HEADS * PEER_QDIM), lambda i: (0, 0)),
                  pl.BlockSpec((PEER_HEADS, 2, PEER_NKEYS, PEER_HALF), lambda i: (0, 0, 0, 0))],
        out_specs=[pl.BlockSpec((tm, PEER_PICKS), lambda i: (i, 0)),
                   pl.BlockSpec((tm, PEER_PICKS), lambda i: (i, 0))],
        out_shape=[jax.ShapeDtypeStruct((n, PEER_PICKS), jnp.int32),
                   jax.ShapeDtypeStruct((n, PEER_PICKS), F32)],
        compiler_params=_params("parallel"),
        name="peer_route",
    )(hq, wq, keys)


def _final_body(x1_ref, f_ref, m_ref, g_ref, b_ref, o_ref):
    m = m_ref[0]
    o_ref[0] = _layer_norm(DEEPNORM_ALPHA * x1_ref[0] + m[5:6] * f_ref[0], g_ref[...], b_ref[...])


def _final(x1, ffn, mods, ln_g, ln_b):
    b, t, d = x1.shape
    tm = min(t, TOKEN_TILE)
    tok = lambda i, j: (i, j, 0)
    return pl.pallas_call(
        _final_body,
        grid=(b, t // tm),
        in_specs=[pl.BlockSpec((1, tm, d), tok), pl.BlockSpec((1, tm, d), tok), _mod_spec(mods),
                  pl.BlockSpec((1, d), lambda i, j: (0, 0)), pl.BlockSpec((1, d), lambda i, j: (0, 0))],
        out_specs=pl.BlockSpec((1, tm, d), tok),
        out_shape=jax.ShapeDtypeStruct(x1.shape, F32),
        compiler_params=_params("parallel", "parallel"),
        name="final_ln",
    )(x1, ffn, mods, ln_g.reshape(1, d), ln_b.reshape(1, d))


N_EXPERTS = PEER_NKEYS * PEER_NKEYS
GATE_ROWS = 4
GATE_TOKENS = 32
PEER_TB = 1024
PEER_EC = 2048


def _gate_matrix(idx, gates):
    n = idx.shape[0]
    info = plsc.get_sparse_core_info()
    n_workers = info.num_cores * info.num_subcores
    lanes = info.num_lanes
    per_w = n // n_workers
    assert n % (n_workers * GATE_TOKENS) == 0 and GATE_TOKENS % GATE_ROWS == 0 and PEER_PICKS % lanes == 0
    vecs = PEER_PICKS // lanes
    mesh = plsc.VectorSubcoreMesh(core_axis_name="core", subcore_axis_name="subcore")

    @functools.partial(
        pl.kernel, out_type=jax.ShapeDtypeStruct((n, N_EXPERTS), F32), mesh=mesh,
        scratch_types=[pltpu.VMEM((GATE_TOKENS * PEER_PICKS,), jnp.int32),
                       pltpu.VMEM((GATE_TOKENS * PEER_PICKS,), F32)]
                      + [pltpu.VMEM((N_EXPERTS,), F32)] * GATE_ROWS
                      + [pltpu.SemaphoreType.DMA] * GATE_ROWS,
        compiler_params=pltpu.CompilerParams(needs_layout_passes=False), name="sc_gate_matrix")
    def run(idx_hbm, gate_hbm, out_hbm, idx_v, g_v, *bufs):
        rows, sems = bufs[:GATE_ROWS], bufs[GATE_ROWS:]
        wid = lax.axis_index("subcore") * info.num_cores + lax.axis_index("core")
        tok_base = wid * per_w
        zeros = jnp.zeros((lanes,), F32)

        @pl.loop(0, N_EXPERTS // lanes)
        def _(i):
            for r in rows:
                r[pl.ds(pl.multiple_of(i * lanes, lanes), lanes)] = zeros

        @pl.loop(0, per_w // GATE_TOKENS)
        def _(blk):
            tok0 = tok_base + blk * GATE_TOKENS
            k_at = pl.ds(pl.multiple_of(tok0 * PEER_PICKS, GATE_TOKENS * PEER_PICKS), GATE_TOKENS * PEER_PICKS)
            pltpu.sync_copy(idx_hbm.at[k_at], idx_v)
            pltpu.sync_copy(gate_hbm.at[k_at], g_v)

            @pl.loop(0, GATE_TOKENS // GATE_ROWS)
            def _(q):
                def picks(b, c):
                    return pl.ds(pl.multiple_of(((q * GATE_ROWS + b) * vecs + c) * lanes, lanes), lanes)

                for b in range(GATE_ROWS):
                    for c in range(vecs):
                        plsc.addupdate_scatter(rows[b], [idx_v[picks(b, c)]], g_v[picks(b, c)])
                    pltpu.make_async_copy(rows[b], out_hbm.at[tok0 + q * GATE_ROWS + b], sems[b]).start()
                for b in range(GATE_ROWS):
                    pltpu.make_async_copy(rows[b], out_hbm.at[tok0 + q * GATE_ROWS + b], sems[b]).wait()
                    for c in range(vecs):
                        plsc.store_scatter(rows[b], [idx_v[picks(b, c)]], zeros)

    return run(idx.reshape(-1), gates.reshape(-1))


def _dense_body(x_ref, u_ref, v_ref, g_ref, o_ref, acc):
    j = pl.program_id(1)

    @pl.when(j == 0)
    def _():
        acc[...] = jnp.zeros_like(acc)

    s = lax.dot_general(x_ref[...], u_ref[...], (((1,), (1,)), ((), ())), preferred_element_type=F32)
    g = g_ref[...]
    a = jnp.where(g != 0.0, jax.nn.gelu(s) * g, 0.0).astype(BF16)
    acc[...] += jnp.dot(a, v_ref[...], preferred_element_type=F32)

    @pl.when(j == pl.num_programs(1) - 1)
    def _():
        o_ref[...] = acc[...]


def _experts(hq, gate_mat, u_tab, v_tab, layer):
    n, d = hq.shape
    tb = min(PEER_TB, n)
    return pl.pallas_call(
        _dense_body,
        grid=(n // tb, N_EXPERTS // PEER_EC),
        in_specs=[pl.BlockSpec((tb, d), lambda i, j: (i, 0)),
                  pl.BlockSpec((None, PEER_EC, d), lambda i, j: (layer, j, 0)),
                  pl.BlockSpec((None, PEER_EC, d), lambda i, j: (layer, j, 0)),
                  pl.BlockSpec((tb, PEER_EC), lambda i, j: (i, j))],
        out_specs=pl.BlockSpec((tb, d), lambda i, j: (i, 0)),
        out_shape=jax.ShapeDtypeStruct((n, d), F32),
        scratch_shapes=[pltpu.VMEM((tb, d), F32)],
        compiler_params=_params("parallel", "arbitrary"),
        name="peer_dense",
    )(hq, u_tab, v_tab, gate_mat)


def _grid_pos_embed(rows):
    t = jnp.arange(rows * GRID_W)
    r = (t // GRID_W).astype(F32)
    col = (t % GRID_W).astype(F32)
    n_freq = D_MODEL // 4
    omega = 1.0 / (10000.0 ** (jnp.arange(n_freq, dtype=F32) / n_freq))

    def enc(p):
        ang = p[:, None] * omega[None, :]
        return jnp.concatenate([jnp.sin(ang), jnp.cos(ang)], -1)
    return jnp.concatenate([enc(r), enc(col)], -1)


def _layer(x, mods, h0, p):
    b, t, d = x.shape
    z = _mix_in(x, mods, p["w_in"])
    rnn, st = _rnn(z, h0, p["conv_w"], p["conv_b"], p["wa"], p["wx"], p["b_a"], p["b_x"], p["lam"])
    pool = _pool(z, p["w_pool"], p["pool_scale"])
    x1, hq = _mix_out(rnn, pool, x, mods, p["w_out"], p["ln1_g"], p["ln1_b"])
    hq = hq.reshape(b * t, d)
    idx, gates = _route(hq, p["wq"], p["keys"])
    gate_mat = _gate_matrix(idx, gates)
    ffn = _experts(hq, gate_mat, p["peer_u"], p["peer_v"], p["layer"])
    x2 = _final(x1, ffn.reshape(b, t, d), mods, p["ln2_g"], p["ln2_b"])
    return x2, st


def kernel(x_prompt, x_sample, state_rglru, c, c_ctx, w_mod, b_mod, w_in, conv_w, conv_b,
           w_rg_a, b_rg_a, w_rg_x, b_rg_x, lru_lambda, w_pool, pool_scale, w_out,
           ln1_g, ln1_b, ln2_g, ln2_b, peer_wq, peer_keys, peer_u, peer_v):
    n_req = c.shape[0]
    cvecs = jnp.zeros((MOD_ROWS, D_MODEL), F32).at[:n_req].set(c).at[n_req].set(c_ctx)
    mod = _modulation(cvecs, w_mod, b_mod).reshape(DEPTH, MOD_ROWS, N_MOD, D_MODEL)
    mod = jnp.pad(mod, ((0, 0), (0, 0), (0, SUBLANES - N_MOD), (0, 0)))

    rows = x_sample.shape[1] // GRID_W
    xs = _add_pos(x_sample, _grid_pos_embed(rows))
    xp = x_prompt
    zero_state = jnp.zeros((x_prompt.shape[0], 2, D_RNN), F32)
    ctx_states = []
    u_bf16, v_bf16 = peer_u.astype(BF16), peer_v.astype(BF16)
    for l in range(DEPTH):
        p = dict(w_in=w_in[l].astype(BF16), conv_w=conv_w[l], conv_b=conv_b[l],
                 wa=_block_diag_pairs(w_rg_a[l]), wx=_block_diag_pairs(w_rg_x[l]),
                 b_a=b_rg_a[l], b_x=b_rg_x[l], lam=lru_lambda[l],
                 w_pool=w_pool[l].astype(BF16), pool_scale=pool_scale[l], w_out=w_out[l].astype(BF16),
                 ln1_g=ln1_g[l], ln1_b=ln1_b[l], ln2_g=ln2_g[l], ln2_b=ln2_b[l],
                 wq=peer_wq[l].astype(BF16), keys=peer_keys[l].astype(BF16),
                 peer_u=u_bf16, peer_v=v_bf16, layer=l)
        for stream in ("ctx", "latent") if l % 2 == 0 else ("latent", "ctx"):
            if stream == "ctx":
                xp, st = _layer(xp, mod[l, n_req:n_req + 1], zero_state, p)
                ctx_states.append(st)
            else:
                xs, _ = _layer(xs, mod[l, :n_req], state_rglru[:, l], p)
    return xp, xs, jnp.stack(ctx_states, axis=1)
```

```python
import functools

import jax
import jax.numpy as jnp
from jax import lax
from jax.experimental import pallas as pl
from jax.experimental.pallas import tpu as pltpu
from jax.experimental.pallas import tpu_sc as plsc

F32 = jnp.float32
BF16 = jnp.bfloat16

D_MODEL = 1024
DEPTH = 2
GRID_W = 64
D_RNN = 512
N_RNN_HEADS = 8
RNN_HEAD_DIM = D_RNN // N_RNN_HEADS
CONV_W = 4
LRU_C = 8.0
D_POOL = 512
POOL_WINDOWS = (2, 4, 8, 16)
POOL_GROUP_DIM = D_POOL // len(POOL_WINDOWS)
D_IN = 2 * D_RNN + D_POOL
PEER_HEADS = 8
PEER_NKEYS = 128
PEER_TOPK = 16
PEER_QDIM = 256
PEER_HALF = PEER_QDIM // 2
PEER_PICKS = PEER_HEADS * PEER_TOPK
N_MOD = 6
DEEPNORM_ALPHA = (2 * DEPTH) ** 0.25
LN_EPS = 1e-5

LANES = 128
SUBLANES = 8
VMEM_LIMIT_BYTES = 56 * 1024 * 1024
MOD_ROWS = 16
ROW_CHUNK = 256
TOKEN_TILE = 1024
ROUTE_TILE = 256
MOD_TILE = 1536


def _params(*sem):
    return pltpu.CompilerParams(dimension_semantics=sem, vmem_limit_bytes=VMEM_LIMIT_BYTES)


def _layer_norm(y, g, b):
    mu = jnp.mean(y, -1, keepdims=True)
    var = jnp.mean(jnp.square(y - mu), -1, keepdims=True)
    return (y - mu) * lax.rsqrt(var + LN_EPS) * g + b


def _mod_body(c_ref, w_ref, b_ref, o_ref):
    c = c_ref[...]
    s = (c * jax.nn.sigmoid(c)).astype(BF16)
    o_ref[0] = jnp.dot(s, w_ref[0].astype(BF16), preferred_element_type=F32) + b_ref[0]


def _modulation(cvecs, w_mod, b_mod):
    n_out = N_MOD * D_MODEL
    tn = MOD_TILE
    return pl.pallas_call(
        _mod_body,
        grid=(DEPTH, n_out // tn),
        in_specs=[pl.BlockSpec((MOD_ROWS, D_MODEL), lambda l, j: (0, 0)),
                  pl.BlockSpec((1, D_MODEL, tn), lambda l, j: (l, 0, j)),
                  pl.BlockSpec((1, 1, tn), lambda l, j: (l, 0, j))],
        out_specs=pl.BlockSpec((1, MOD_ROWS, tn), lambda l, j: (l, 0, j)),
        out_shape=jax.ShapeDtypeStruct((DEPTH, MOD_ROWS, n_out), F32),
        compiler_params=_params("parallel", "parallel"),
        name="modulation",
    )(cvecs, w_mod, b_mod.reshape(DEPTH, 1, n_out))


def _mix_in_body(x_ref, p_ref, m_ref, w_ref, z_ref):
    m = m_ref[0]
    h = (x_ref[0] + p_ref[...]) * (1.0 + m[1:2]) + m[0:1]
    z = jnp.dot(h.astype(BF16), w_ref[...], preferred_element_type=F32)
    for c in range(D_IN // LANES):
        z_ref[0, c] = z[:, c * LANES:(c + 1) * LANES]


def _mod_spec(mods, batch_axis=0):
    if mods.shape[0] == 1:
        return pl.BlockSpec((1, SUBLANES, D_MODEL), lambda *g: (0, 0, 0))
    return pl.BlockSpec((1, SUBLANES, D_MODEL), lambda *g: (g[batch_axis], 0, 0))


def _mix_in(x, pos, mods, w_in):
    b, t, d = x.shape
    tm = min(t, TOKEN_TILE)
    return pl.pallas_call(
        _mix_in_body,
        grid=(t // tm, b),
        in_specs=[pl.BlockSpec((1, tm, d), lambda j, i: (i, j, 0)),
                  pl.BlockSpec((tm, d), lambda j, i: (j, 0)),
                  _mod_spec(mods, batch_axis=1),
                  pl.BlockSpec((d, D_IN), lambda j, i: (0, 0))],
        out_specs=pl.BlockSpec((1, D_IN // LANES, tm, LANES), lambda j, i: (i, 0, j, 0)),
        out_shape=jax.ShapeDtypeStruct((b, D_IN // LANES, t, LANES), F32),
        compiler_params=_params("parallel", "parallel"),
        name="mix_in",
    )(x, pos, mods, w_in)


PAD = SUBLANES
CARRY_PAD = 2 * SUBLANES


def _group_scan(a, u, reverse):
    row = lax.broadcasted_iota(jnp.int32, a.shape, 1)
    for s in (1, 2, 4):
        if reverse:
            keep = row <= SUBLANES - 1 - s
            shift = SUBLANES - s
        else:
            keep = row >= s
            shift = s
        a_sh = jnp.where(keep, pltpu.roll(a, shift, 1), 1.0)
        u_sh = jnp.where(keep, pltpu.roll(u, shift, 1), 0.0)
        u = a * u_sh + u
        a = a * a_sh
    return a, u


def _rnn_body(xr_ref, gr_ref, h0_ref, cw_ref, cb_ref, wa_ref, wx_ref, ba_ref, bx_ref, lam_ref,
              out_ref, st_ref, xpad, a_s, u_s, a2_s, u2_s, e_s):
    t_len = xr_ref.shape[2]
    rc = min(ROW_CHUNK, t_len)
    zeros = jnp.zeros((PAD, LANES), F32)
    xpad[0:PAD] = zeros
    xpad[t_len + PAD:t_len + 2 * PAD] = zeros
    xpad[PAD:t_len + PAD] = xr_ref[0, 0]

    cw = cw_ref[...]
    lam = lam_ref[...]
    softplus_neg = jnp.maximum(-lam, 0.0) + jnp.log1p(jnp.exp(-jnp.abs(lam)))
    coef = -LRU_C * softplus_neg
    left = CONV_W // 2

    for c in range(t_len // rc):
        t0 = c * rc
        xc = cb_ref[...] + xpad[t0 + PAD - left:t0 + PAD - left + rc] * cw[0:1]
        for k in range(1, CONV_W):
            xc = xc + xpad[t0 + PAD - left + k:t0 + PAD - left + k + rc] * cw[k:k + 1]
        xcb = xc.astype(BF16)
        for d in range(2):
            r = jax.nn.sigmoid(jnp.dot(xcb, wa_ref[d, 0], preferred_element_type=F32) + ba_ref[d:d + 1])
            gi = jax.nn.sigmoid(jnp.dot(xcb, wx_ref[d, 0], preferred_element_type=F32) + bx_ref[d:d + 1])
            log_a = coef[d:d + 1] * r
            a = jnp.exp(log_a)
            u = jnp.sqrt(-jnp.tanh(log_a) * (a * a + 1.0)) * (gi * xc)
            a, u = _group_scan(a.reshape(rc // SUBLANES, SUBLANES, LANES),
                               u.reshape(rc // SUBLANES, SUBLANES, LANES), reverse=(d == 1))
            a_s[d, t0:t0 + rc] = a.reshape(rc, LANES)
            u_s[d, t0:t0 + rc] = u.reshape(rc, LANES)

    n_groups = t_len // SUBLANES
    n_super = n_groups // SUBLANES
    for d in range(2):
        edge = pl.ds(SUBLANES - 1 if d == 0 else 0, n_groups, stride=SUBLANES)
        a2, u2 = _group_scan(a_s[d, edge].reshape(n_super, SUBLANES, LANES),
                             u_s[d, edge].reshape(n_super, SUBLANES, LANES), reverse=(d == 1))
        a2_s[d] = a2.reshape(n_groups, LANES)
        u2_s[d] = u2.reshape(n_groups, LANES)

    h0 = h0_ref[0]
    e_s[0, SUBLANES - 1:SUBLANES] = h0[0:1]
    e_s[1, n_groups:n_groups + 1] = h0[1:2]
    unroll = min(SUBLANES, n_super)

    def carry_step(i, carry):
        hf, hb = carry
        for j in range(unroll):
            gf = pl.multiple_of((i * unroll + j) * SUBLANES, SUBLANES)
            gb = pl.multiple_of((n_super - 1 - i * unroll - j) * SUBLANES, SUBLANES)
            h_f = u2_s[0, pl.ds(gf, SUBLANES)] + a2_s[0, pl.ds(gf, SUBLANES)] * hf
            h_b = u2_s[1, pl.ds(gb, SUBLANES)] + a2_s[1, pl.ds(gb, SUBLANES)] * hb
            e_s[0, pl.ds(gf + SUBLANES, SUBLANES)] = h_f
            e_s[1, pl.ds(gb, SUBLANES)] = h_b
            hf = h_f[SUBLANES - 1:SUBLANES]
            hb = h_b[0:1]
        return hf, hb

    hf, hb = lax.fori_loop(0, n_super // unroll, carry_step, (h0[0:1], h0[1:2]))
    st_ref[0, 0:1] = hf
    st_ref[0, 1:2] = hb

    gc = rc // SUBLANES
    for c in range(t_len // rc):
        t0 = c * rc
        g0 = t0 // SUBLANES
        hs = []
        for d, off in ((0, SUBLANES - 1), (1, 1)):
            entry = e_s[d, g0 + off:g0 + off + gc]
            hs.append(u_s[d, t0:t0 + rc].reshape(gc, SUBLANES, LANES)
                      + a_s[d, t0:t0 + rc].reshape(gc, SUBLANES, LANES) * entry[:, None, :])
        hsum = (hs[0] + hs[1]).reshape(rc, LANES)
        out_ref[0, 0, t0:t0 + rc] = (hsum * jax.nn.gelu(gr_ref[0, 0, t0:t0 + rc])).astype(BF16)


def _rnn(z, h0, conv_w, conv_b, wa, wx, b_a, b_x, lam):
    b, _, t, _ = z.shape
    nblk = D_RNN // LANES
    return pl.pallas_call(
        _rnn_body,
        grid=(b, nblk),
        in_specs=[pl.BlockSpec((1, 1, t, LANES), lambda i, j: (i, j, 0, 0)),
                  pl.BlockSpec((1, 1, t, LANES), lambda i, j: (i, nblk + j, 0, 0)),
                  pl.BlockSpec((1, 2, LANES), lambda i, j: (i, 0, j)),
                  pl.BlockSpec((CONV_W, LANES), lambda i, j: (0, j)),
                  pl.BlockSpec((1, LANES), lambda i, j: (0, j)),
                  pl.BlockSpec((2, 1, LANES, LANES), lambda i, j: (0, j, 0, 0)),
                  pl.BlockSpec((2, 1, LANES, LANES), lambda i, j: (0, j, 0, 0)),
                  pl.BlockSpec((2, LANES), lambda i, j: (0, j)),
                  pl.BlockSpec((2, LANES), lambda i, j: (0, j)),
                  pl.BlockSpec((2, LANES), lambda i, j: (0, j))],
        out_specs=[pl.BlockSpec((1, 1, t, LANES), lambda i, j: (i, j, 0, 0)),
                   pl.BlockSpec((1, 2, LANES), lambda i, j: (i, 0, j))],
        out_shape=[jax.ShapeDtypeStruct((b, nblk, t, LANES), BF16),
                   jax.ShapeDtypeStruct((b, 2, D_RNN), F32)],
        scratch_shapes=[pltpu.VMEM((t + 2 * PAD, LANES), F32),
                        pltpu.VMEM((2, t, LANES), F32),
                        pltpu.VMEM((2, t, LANES), F32),
                        pltpu.VMEM((2, t // SUBLANES, LANES), F32),
                        pltpu.VMEM((2, t // SUBLANES, LANES), F32),
                        pltpu.VMEM((2, t // SUBLANES + CARRY_PAD, LANES), F32)],
        compiler_params=_params("parallel", "parallel"),
        name="rglru",
    )(z, z, h0, conv_w, conv_b.reshape(1, D_RNN), wa, wx, b_a, b_x, lam)


def _block_diag_pairs(w):
    per = LANES // RNN_HEAD_DIM
    w = w.reshape(2, D_RNN // LANES, per, RNN_HEAD_DIM, RNN_HEAD_DIM)
    out = jnp.zeros((2, D_RNN // LANES, LANES, LANES), w.dtype)
    for p in range(per):
        sl = slice(p * RNN_HEAD_DIM, (p + 1) * RNN_HEAD_DIM)
        out = out.at[:, :, sl, sl].set(w[:, :, p])
    return out.astype(BF16)


POOL_PAD = 16


def _pool_body(xq_ref, w_ref, sc_ref, out_ref, ppad):
    t_len = xq_ref.shape[2]
    rc = min(ROW_CHUNK, t_len)
    zeros = jnp.zeros((POOL_PAD, D_POOL), F32)
    ppad[0:POOL_PAD] = zeros
    ppad[t_len + POOL_PAD:t_len + 2 * POOL_PAD] = zeros
    for g in range(len(POOL_WINDOWS)):
        ppad[POOL_PAD:t_len + POOL_PAD, g * POOL_GROUP_DIM:(g + 1) * POOL_GROUP_DIM] = xq_ref[0, g]
    for c in range(t_len // rc):
        t0 = c * rc
        tpos = t0 + lax.broadcasted_iota(jnp.int32, (rc, POOL_GROUP_DIM), 0)
        for g, w in enumerate(POOL_WINDOWS):
            cols = slice(g * POOL_GROUP_DIM, (g + 1) * POOL_GROUP_DIM)
            half = w // 2
            base = t0 + POOL_PAD - half
            s = ppad[base:base + rc, cols]
            for k in range(1, w):
                s = s + ppad[base + k:base + k + rc, cols]
            cnt = (jnp.minimum(tpos + half, t_len) - jnp.maximum(tpos - half, 0)).astype(F32)
            pooled = s / cnt - ppad[t0 + POOL_PAD:t0 + POOL_PAD + rc, cols]
            y = jnp.dot(pooled.astype(BF16), w_ref[g], preferred_element_type=F32)
            out_ref[0, g, t0:t0 + rc] = (y * sc_ref[:, cols]).astype(BF16)


def _pool(z, w_pool, pool_scale):
    b, _, t, _ = z.shape
    n_grp = len(POOL_WINDOWS)
    return pl.pallas_call(
        _pool_body,
        grid=(b,),
        in_specs=[pl.BlockSpec((1, n_grp, t, POOL_GROUP_DIM), lambda i: (i, 2 * D_RNN // D_POOL, 0, 0)),
                  pl.BlockSpec((len(POOL_WINDOWS), POOL_GROUP_DIM, POOL_GROUP_DIM), lambda i: (0, 0, 0)),
                  pl.BlockSpec((1, D_POOL), lambda i: (0, 0))],
        out_specs=pl.BlockSpec((1, n_grp, t, POOL_GROUP_DIM), lambda i: (i, 0, 0, 0)),
        out_shape=jax.ShapeDtypeStruct((b, n_grp, t, POOL_GROUP_DIM), BF16),
        scratch_shapes=[pltpu.VMEM((t + 2 * POOL_PAD, D_POOL), F32)],
        compiler_params=_params("parallel"),
        name="pool",
    )(z, w_pool, pool_scale.reshape(1, D_POOL))


def _mix_out_body(rnn_ref, pool_ref, x_ref, p_ref, m_ref, w_ref, g_ref, b_ref, x1_ref, hq_ref):
    m = m_ref[0]
    rnn = jnp.concatenate([rnn_ref[0, c] for c in range(rnn_ref.shape[1])], axis=1)
    pool = jnp.concatenate([pool_ref[0, c] for c in range(pool_ref.shape[1])], axis=1)
    mix = (jnp.dot(rnn, w_ref[0:D_RNN], preferred_element_type=F32)
           + jnp.dot(pool, w_ref[D_RNN:D_RNN + D_POOL], preferred_element_type=F32))
    x1 = _layer_norm(DEEPNORM_ALPHA * (x_ref[0] + p_ref[...]) + m[2:3] * mix, g_ref[...], b_ref[...])
    x1_ref[0] = x1
    hq_ref[0] = (x1 * (1.0 + m[4:5]) + m[3:4]).astype(BF16)


def _mix_out(rnn, pool, x, pos, mods, w_out, ln_g, ln_b):
    b, t, d = x.shape
    tm = min(t, TOKEN_TILE)
    tok = lambda j, i: (i, j, 0)
    return pl.pallas_call(
        _mix_out_body,
        grid=(t // tm, b),
        in_specs=[pl.BlockSpec((1, D_RNN // LANES, tm, LANES), lambda j, i: (i, 0, j, 0)),
                  pl.BlockSpec((1, D_POOL // LANES, tm, LANES), lambda j, i: (i, 0, j, 0)),
                  pl.BlockSpec((1, tm, d), tok),
                  pl.BlockSpec((tm, d), lambda j, i: (j, 0)),
                  _mod_spec(mods, batch_axis=1),
                  pl.BlockSpec((D_RNN + D_POOL, d), lambda j, i: (0, 0)),
                  pl.BlockSpec((1, d), lambda j, i: (0, 0)),
                  pl.BlockSpec((1, d), lambda j, i: (0, 0))],
        out_specs=[pl.BlockSpec((1, tm, d), tok), pl.BlockSpec((1, tm, d), tok)],
        out_shape=[jax.ShapeDtypeStruct(x.shape, F32), jax.ShapeDtypeStruct(x.shape, BF16)],
        compiler_params=_params("parallel", "parallel"),
        name="mix_out",
    )(rnn, pool, x, pos, mods, w_out, ln_g.reshape(1, d), ln_b.reshape(1, d))


NEG_INF = float("-inf")


BIG_ID = 2 ** 30


def _sort_network(n):
    def merge(lo, hi, r):
        step = r * 2
        if step < hi - lo:
            yield from merge(lo, hi, step)
            yield from merge(lo + r, hi, step)
            for i in range(lo + r, hi - r, step):
                yield (i, i + r)
        else:
            yield (lo, lo + r)

    def sort(lo, hi):
        if hi - lo >= 1:
            mid = lo + (hi - lo) // 2
            yield from sort(lo, mid)
            yield from sort(mid + 1, hi)
            yield from merge(lo, hi, 1)

    return list(sort(0, n - 1))


def _sorted_levels(s_t):
    n_lvl = s_t.shape[0] // SUBLANES
    sub = lax.broadcasted_iota(jnp.int32, (SUBLANES, s_t.shape[1]), 0)
    vals = [s_t[l * SUBLANES:(l + 1) * SUBLANES] for l in range(n_lvl)]
    ids = [sub + l * SUBLANES for l in range(n_lvl)]
    untouched = [True] * n_lvl
    for i, j in _sort_network(n_lvl):
        a, b, ia, ib = vals[i], vals[j], ids[i], ids[j]
        swap = b > a
        if not (untouched[i] and untouched[j]):
            swap = swap | ((b == a) & (ib < ia))
        vals[i], vals[j] = jnp.maximum(a, b), jnp.minimum(a, b)
        ids[i], ids[j] = jnp.where(swap, ib, ia), jnp.where(swap, ia, ib)
        untouched[i] = untouched[j] = False
    return vals, ids


def _pop_top(vals, ids, k):
    vals, ids = list(vals), list(ids)
    out_v, out_i = [], []
    for it in range(k):
        m = jnp.max(vals[0], axis=0, keepdims=True)
        pick = jnp.min(jnp.where(vals[0] == m, ids[0], BIG_ID), axis=0, keepdims=True)
        sel = ids[0] == pick
        out_v.append(m)
        out_i.append(pick)
        for l in range(min(len(vals) - 1, k - 1 - it)):
            vals[l] = jnp.where(sel, vals[l + 1], vals[l])
            ids[l] = jnp.where(sel, ids[l + 1], ids[l])
    return jnp.concatenate(out_v, axis=0), jnp.concatenate(out_i, axis=0)


def _pair_top(s1, i1, s2, i2):
    k = PEER_TOPK
    tm = s1.shape[1]
    sub = lax.broadcasted_iota(jnp.int32, (SUBLANES, tm), 0)
    s2lo, i2lo = s2[0:SUBLANES], i2[0:SUBLANES]
    cand, expert = [], []
    for l in range(k):
        ok = (sub + 1) * (l + 1) <= k
        cand.append(jnp.where(ok, s1[l:l + 1] + s2lo, NEG_INF))
        expert.append(i1[l:l + 1] * PEER_NKEYS + i2lo)
    single = s1[0:1] + s2[SUBLANES:k]
    single_expert = i1[0:1] * PEER_NKEYS + i2[SUBLANES:k]
    single_flat = sub + SUBLANES
    pops = jnp.zeros((SUBLANES, tm), jnp.int32)
    out_s, out_e = [], []
    for it in range(k):
        flat = pops * k + sub
        m = jnp.max(jnp.maximum(cand[0], single), axis=0, keepdims=True)
        pick = jnp.min(jnp.minimum(jnp.where(cand[0] == m, flat, BIG_ID),
                                   jnp.where(single == m, single_flat, BIG_ID)), axis=0, keepdims=True)
        sel = flat == pick
        sel_single = single_flat == pick
        out_s.append(m)
        out_e.append(jnp.sum(jnp.where(sel, expert[0], 0) + jnp.where(sel_single, single_expert, 0),
                             axis=0, keepdims=True))
        single = jnp.where(sel_single, NEG_INF, single)
        pops = jnp.where(sel, pops + 1, pops)
        for l in range(k - 1 - it):
            cand[l] = jnp.where(sel, cand[l + 1], cand[l])
            expert[l] = jnp.where(sel, expert[l + 1], expert[l])
    return jnp.concatenate(out_s, axis=0), jnp.concatenate(out_e, axis=0)


def _route_body(hq_ref, wq_ref, keys_ref, idx_ref, gate_ref):
    q = jnp.dot(hq_ref[...], wq_ref[...], preferred_element_type=F32)
    gates, experts = [], []
    for h in range(PEER_HEADS):
        tops = []
        for p in range(2):
            c0 = (h * 2 + p) * PEER_HALF
            qs = q[:, c0:c0 + PEER_HALF].astype(BF16)
            s_t = lax.dot_general(keys_ref[h, p], qs, (((1,), (1,)), ((), ())),
                                  preferred_element_type=F32)
            tops.append(_pop_top(*_sorted_levels(s_t), PEER_TOPK))
        (s1, i1), (s2, i2) = tops
        sc, picked = _pair_top(s1, i1, s2, i2)
        e = jnp.exp(sc - sc[0:1])
        gates.append(e / jnp.sum(e, axis=0, keepdims=True))
        experts.append(picked)
    gate_ref[...] = jnp.concatenate(gates, axis=0).T
    idx_ref[...] = jnp.concatenate(experts, axis=0).T


def _route(hq, wq, keys):
    n, d = hq.shape
    tm = ROUTE_TILE
    return pl.pallas_call(
        _route_body,
        grid=(n // tm,),
        in_specs=[pl.BlockSpec((tm, d), lambda i: (i, 0)),
                  pl.BlockSpec((d, PEER_HEADS * PEER_QDIM), lambda i: (0, 0)),
                  pl.BlockSpec((PEER_HEADS, 2, PEER_NKEYS, PEER_HALF), lambda i: (0, 0, 0, 0))],
        out_specs=[pl.BlockSpec((tm, PEER_PICKS), lambda i: (i, 0)),
                   pl.BlockSpec((tm, PEER_PICKS), lambda i: (i, 0))],
        out_shape=[jax.ShapeDtypeStruct((n, PEER_PICKS), jnp.int32),
                   jax.ShapeDtypeStruct((n, PEER_PICKS), F32)],
        compiler_params=_params("parallel"),
        name="peer_route",
    )(hq, wq, keys)


def _final_body(x1_ref, f_ref, m_ref, g_ref, b_ref, o_ref):
    m = m_ref[0]
    o_ref[0] = _layer_norm(DEEPNORM_ALPHA * x1_ref[0] + m[5:6] * f_ref[0], g_ref[...], b_ref[...])


def _final(x1, ffn, mods, ln_g, ln_b):
    b, t, d = x1.shape
    tm = min(t, TOKEN_TILE)
    tok = lambda i, j: (i, j, 0)
    return pl.pallas_call(
        _final_body,
        grid=(b, t // tm),
        in_specs=[pl.BlockSpec((1, tm, d), tok), pl.BlockSpec((1, tm, d), tok), _mod_spec(mods),
                  pl.BlockSpec((1, d), lambda i, j: (0, 0)), pl.BlockSpec((1, d), lambda i, j: (0, 0))],
        out_specs=pl.BlockSpec((1, tm, d), tok),
        out_shape=jax.ShapeDtypeStruct(x1.shape, F32),
        compiler_params=_params("parallel", "parallel"),
        name="final_ln",
    )(x1, ffn, mods, ln_g.reshape(1, d), ln_b.reshape(1, d))


N_EXPERTS = PEER_NKEYS * PEER_NKEYS
GATE_ROWS = 4
GATE_TOKENS = 32
PEER_TB = 1024
PEER_EC = 2048


def _gate_matrix(idx, gates):
    n = idx.shape[0]
    info = plsc.get_sparse_core_info()
    n_workers = info.num_cores * info.num_subcores
    lanes = info.num_lanes
    per_w = n // n_workers
    assert n % (n_workers * GATE_TOKENS) == 0 and GATE_TOKENS % GATE_ROWS == 0 and PEER_PICKS % lanes == 0
    vecs = PEER_PICKS // lanes
    mesh = plsc.VectorSubcoreMesh(core_axis_name="core", subcore_axis_name="subcore")

    @functools.partial(
        pl.kernel, out_type=jax.ShapeDtypeStruct((n, N_EXPERTS), F32), mesh=mesh,
        scratch_types=[pltpu.VMEM((GATE_TOKENS * PEER_PICKS,), jnp.int32),
                       pltpu.VMEM((GATE_TOKENS * PEER_PICKS,), F32)]
                      + [pltpu.VMEM((N_EXPERTS,), F32)] * GATE_ROWS
                      + [pltpu.SemaphoreType.DMA] * GATE_ROWS,
        compiler_params=pltpu.CompilerParams(needs_layout_passes=False), name="sc_gate_matrix")
    def run(idx_hbm, gate_hbm, out_hbm, idx_v, g_v, *bufs):
        rows, sems = bufs[:GATE_ROWS], bufs[GATE_ROWS:]
        wid = lax.axis_index("subcore") * info.num_cores + lax.axis_index("core")
        tok_base = wid * per_w
        zeros = jnp.zeros((lanes,), F32)

        @pl.loop(0, N_EXPERTS // lanes)
        def _(i):
            for r in rows:
                r[pl.ds(pl.multiple_of(i * lanes, lanes), lanes)] = zeros

        @pl.loop(0, per_w // GATE_TOKENS)
        def _(blk):
            tok0 = tok_base + blk * GATE_TOKENS
            k_at = pl.ds(pl.multiple_of(tok0 * PEER_PICKS, GATE_TOKENS * PEER_PICKS), GATE_TOKENS * PEER_PICKS)
            pltpu.sync_copy(idx_hbm.at[k_at], idx_v)
            pltpu.sync_copy(gate_hbm.at[k_at], g_v)

            @pl.loop(0, GATE_TOKENS // GATE_ROWS)
            def _(q):
                def picks(b, c):
                    return pl.ds(pl.multiple_of(((q * GATE_ROWS + b) * vecs + c) * lanes, lanes), lanes)

                for b in range(GATE_ROWS):
                    for c in range(vecs):
                        plsc.addupdate_scatter(rows[b], [idx_v[picks(b, c)]], g_v[picks(b, c)])
                    pltpu.make_async_copy(rows[b], out_hbm.at[tok0 + q * GATE_ROWS + b], sems[b]).start()
                for b in range(GATE_ROWS):
                    pltpu.make_async_copy(rows[b], out_hbm.at[tok0 + q * GATE_ROWS + b], sems[b]).wait()
                    for c in range(vecs):
                        plsc.store_scatter(rows[b], [idx_v[picks(b, c)]], zeros)

    return run(idx.reshape(-1), gates.reshape(-1))


def _dense_body(x_ref, u_ref, v_ref, g_ref, o_ref, acc):
    j = pl.program_id(1)

    @pl.when(j == 0)
    def _():
        acc[...] = jnp.zeros_like(acc)

    s = lax.dot_general(x_ref[...], u_ref[...], (((1,), (1,)), ((), ())), preferred_element_type=F32)
    g = g_ref[...]
    a = jnp.where(g != 0.0, jax.nn.gelu(s) * g, 0.0).astype(BF16)
    acc[...] += jnp.dot(a, v_ref[...], preferred_element_type=F32)

    @pl.when(j == pl.num_programs(1) - 1)
    def _():
        o_ref[...] = acc[...]


def _experts(hq, gate_mat, u_tab, v_tab, layer):
    n, d = hq.shape
    tb = min(PEER_TB, n)
    return pl.pallas_call(
        _dense_body,
        grid=(n // tb, N_EXPERTS // PEER_EC),
        in_specs=[pl.BlockSpec((tb, d), lambda i, j: (i, 0)),
                  pl.BlockSpec((None, PEER_EC, d), lambda i, j: (layer, j, 0)),
                  pl.BlockSpec((None, PEER_EC, d), lambda i, j: (layer, j, 0)),
                  pl.BlockSpec((tb, PEER_EC), lambda i, j: (i, j))],
        out_specs=pl.BlockSpec((tb, d), lambda i, j: (i, 0)),
        out_shape=jax.ShapeDtypeStruct((n, d), F32),
        scratch_shapes=[pltpu.VMEM((tb, d), F32)],
        compiler_params=_params("parallel", "arbitrary"),
        name="peer_dense",
    )(hq, u_tab, v_tab, gate_mat)


def _grid_pos_embed(rows):
    t = jnp.arange(rows * GRID_W)
    r = (t // GRID_W).astype(F32)
    col = (t % GRID_W).astype(F32)
    n_freq = D_MODEL // 4
    omega = 1.0 / (10000.0 ** (jnp.arange(n_freq, dtype=F32) / n_freq))

    def enc(p):
        ang = p[:, None] * omega[None, :]
        return jnp.concatenate([jnp.sin(ang), jnp.cos(ang)], -1)
    return jnp.concatenate([enc(r), enc(col)], -1)


def _layer(x, pos, mods, h0, p):
    b, t, d = x.shape
    z = _mix_in(x, pos, mods, p["w_in"])
    rnn, st = _rnn(z, h0, p["conv_w"], p["conv_b"], p["wa"], p["wx"], p["b_a"], p["b_x"], p["lam"])
    pool = _pool(z, p["w_pool"], p["pool_scale"])
    x1, hq = _mix_out(rnn, pool, x, pos, mods, p["w_out"], p["ln1_g"], p["ln1_b"])
    hq = hq.reshape(b * t, d)
    idx, gates = _route(hq, p["wq"], p["keys"])
    gate_mat = _gate_matrix(idx, gates)
    ffn = _experts(hq, gate_mat, p["peer_u"], p["peer_v"], p["layer"])
    x2 = _final(x1, ffn.reshape(b, t, d), mods, p["ln2_g"], p["ln2_b"])
    return x2, st


def kernel(x_prompt, x_sample, state_rglru, c, c_ctx, w_mod, b_mod, w_in, conv_w, conv_b,
           w_rg_a, b_rg_a, w_rg_x, b_rg_x, lru_lambda, w_pool, pool_scale, w_out,
           ln1_g, ln1_b, ln2_g, ln2_b, peer_wq, peer_keys, peer_u, peer_v):
    n_req = c.shape[0]
    cvecs = jnp.zeros((MOD_ROWS, D_MODEL), F32).at[:n_req].set(c).at[n_req].set(c_ctx)
    mod = _modulation(cvecs, w_mod, b_mod).reshape(DEPTH, MOD_ROWS, N_MOD, D_MODEL)
    mod = jnp.pad(mod, ((0, 0), (0, 0), (0, SUBLANES - N_MOD), (0, 0)))

    rows = x_sample.shape[1] // GRID_W
    xs, xp = x_sample, x_prompt
    no_pos_s = jnp.zeros(x_sample.shape[1:], F32)
    no_pos_p = jnp.zeros(x_prompt.shape[1:], F32)
    zero_state = jnp.zeros((x_prompt.shape[0], 2, D_RNN), F32)
    ctx_states = []
    u_bf16, v_bf16 = peer_u.astype(BF16), peer_v.astype(BF16)
    for l in range(DEPTH):
        p = dict(w_in=w_in[l].astype(BF16), conv_w=conv_w[l], conv_b=conv_b[l],
                 wa=_block_diag_pairs(w_rg_a[l]), wx=_block_diag_pairs(w_rg_x[l]),
                 b_a=b_rg_a[l], b_x=b_rg_x[l], lam=lru_lambda[l],
                 w_pool=w_pool[l].astype(BF16), pool_scale=pool_scale[l], w_out=w_out[l].astype(BF16),
                 ln1_g=ln1_g[l], ln1_b=ln1_b[l], ln2_g=ln2_g[l], ln2_b=ln2_b[l],
                 wq=peer_wq[l].astype(BF16), keys=peer_keys[l].astype(BF16),
                 peer_u=u_bf16, peer_v=v_bf16, layer=l)
        for stream in ("ctx", "latent") if l % 2 == 0 else ("latent", "ctx"):
            if stream == "ctx":
                xp, st = _layer(xp, no_pos_p, mod[l, n_req:n_req + 1], zero_state, p)
                ctx_states.append(st)
            else:
                pos = _grid_pos_embed(rows) if l == 0 else no_pos_s
                xs, _ = _layer(xs, pos, mod[l, :n_req], state_rglru[:, l], p)
    return xp, xs, jnp.stack(ctx_states, axis=1)
```
